```python
import math
import numpy as np
import jax
import jax.numpy as jnp
from jax import lax

D_MODEL = 1024
BATCH = 4
SEQ = 4096
DEPTH = 1
DEC_BATCH = 128
DEC_SEQ = 4
PAST_LEN = 2048
PAGE_SIZE = 128

HEAD_DIM = 64
NSA_HEADS = D_MODEL // 128
NSA_KV_HEADS = 2
DSA_HEADS = D_MODEL // 128
DSA_KV_HEADS = 2
N_NSA_KV = 6
CMP_BLOCK = 32
CMP_STRIDE = 16
CMP_HIDDEN = 2 * HEAD_DIM
SEL_BLOCK = 64
N_SEL = 16
WINDOW = 512
IDX_HEADS = 4
IDX_DIM = 64
DSA_TOPK_MAX = 256
ROPE_THETA = 10000.0
Q_BLOCK = 128
EPS = 1e-6
NEG = -1e30
FORCED_SCORE = 1e4

IN_WIDTHS = (
    NSA_HEADS * HEAD_DIM,
    N_NSA_KV * NSA_KV_HEADS * HEAD_DIM,
    NSA_HEADS * 3,
    NSA_HEADS * HEAD_DIM,
    DSA_HEADS * HEAD_DIM,
    2 * DSA_KV_HEADS * HEAD_DIM,
    IDX_HEADS * IDX_DIM,
    IDX_DIM,
    IDX_HEADS,
    DSA_HEADS * HEAD_DIM,
    2 * D_MODEL,
)
IN_TOTAL = sum(IN_WIDTHS)

kernel_name = 'nsa_dsa_gated_hybrid_step'


def rms_norm(x, g):
    xf = x.astype(jnp.float32)
    y = xf * lax.rsqrt(jnp.mean(xf * xf, axis=-1, keepdims=True) + EPS)
    return (y * g.astype(jnp.float32)).astype(x.dtype)


def rope(x, pos):
    half = x.shape[-1] // 2
    inv = ROPE_THETA ** (-jnp.arange(half, dtype=jnp.float32) / half)
    ang = pos.astype(jnp.float32)[:, None] * inv[None, :]
    cos = jnp.cos(ang)[:, None, :]
    sin = jnp.sin(ang)[:, None, :]
    xf = x.astype(jnp.float32)
    x1, x2 = xf[..., :half], xf[..., half:]
    return jnp.concatenate([x1 * cos - x2 * sin, x2 * cos + x1 * sin], axis=-1).astype(x.dtype)


def masked_softmax(s, mask):
    s = jnp.where(mask, s, NEG)
    p = jnp.where(mask, jnp.exp(s - jnp.max(s, axis=-1, keepdims=True)), 0.0)
    return p / jnp.maximum(jnp.sum(p, axis=-1, keepdims=True), 1e-30)


def ada_norm(x, c, norm_g, w_ada, b_ada):
    mod = jax.nn.silu(c) @ w_ada + b_ada
    shift, scale, gate = jnp.split(mod, 3, axis=-1)
    h = rms_norm(x, norm_g) * (1.0 + scale[:, None, :]) + shift[:, None, :]
    return h, gate


def in_projection(h, w_in, pos):
    Bx, T, _ = h.shape
    cuts = [int(v) for v in np.cumsum(IN_WIDTHS)[:-1]]
    nq, nkv, ng, nz, dq, dkv, iq, ik, iw, dz, mg = jnp.split(h @ w_in, cuts, axis=-1)
    nsa_q = nq.reshape(Bx, T, NSA_HEADS, HEAD_DIM)
    nkv = nkv.reshape(Bx, T, N_NSA_KV, NSA_KV_HEADS, HEAD_DIM)
    nsa_rows = jnp.stack([nkv[:, :, 0], nkv[:, :, 1], rope(nkv[:, :, 2], pos), nkv[:, :, 3]], axis=2)
    win_rows = jnp.stack([rope(nkv[:, :, 4], pos), nkv[:, :, 5]], axis=2)
    dkv = dkv.reshape(Bx, T, 2, DSA_KV_HEADS, HEAD_DIM)
    dsa_rows = jnp.stack([rope(dkv[:, :, 0], pos), dkv[:, :, 1]], axis=2)
    idx_rows = rope(ik[:, :, None, :], pos)[:, :, 0]
    f = {
        'nsa_q': nsa_q,
        'nsa_q_rot': rope(nsa_q, pos),
        'nsa_gate': jax.nn.sigmoid(ng.astype(jnp.float32)).reshape(Bx, T, NSA_HEADS, 3),
        'nsa_z': nz,
        'dsa_q': rope(dq.reshape(Bx, T, DSA_HEADS, HEAD_DIM), pos),
        'idx_q': rope(iq.reshape(Bx, T, IDX_HEADS, IDX_DIM), pos),
        'idx_w': iw,
        'dsa_z': dz,
        'merge': mg,
    }
    return f, nsa_rows, win_rows, dsa_rows, idx_rows


def nsa_context(nsa_all, cmp_pe, cmp_w1, cmp_w2):
    Bx, L = nsa_all.shape[:2]
    G = NSA_KV_HEADS
    n_cmp = (L - CMP_BLOCK) // CMP_STRIDE + 1
    starts = np.arange(n_cmp, dtype=np.int32) * CMP_STRIDE
    blk_idx = starts[:, None] + np.arange(CMP_BLOCK, dtype=np.int32)[None, :]

    def compress(rows, pe, w1, w2):
        blk = rows[:, blk_idx] + pe[None, None, :, None, :]
        flat = jnp.swapaxes(blk, 2, 3).reshape(Bx, n_cmp, G, CMP_BLOCK * HEAD_DIM)
        return jax.nn.silu(flat @ w1) @ w2

    ck = compress(nsa_all[:, :, 0], cmp_pe[0], cmp_w1[0], cmp_w2[0])
    cv = compress(nsa_all[:, :, 1], cmp_pe[1], cmp_w1[1], cmp_w2[1])
    cmp_end = jnp.asarray(starts + CMP_BLOCK - 1)
    n_slc = -(-L // SEL_BLOCK)
    j = np.arange(n_slc, dtype=np.int32)[None, :]
    overlap = (starts[:, None] < (j + 1) * SEL_BLOCK) & (starts[:, None] + CMP_BLOCK > j * SEL_BLOCK)
    overlap = jnp.asarray(overlap, dtype=jnp.float32)
    pad = n_slc * SEL_BLOCK - L
    slc = jnp.pad(nsa_all[:, :, 2:4], ((0, 0), (0, pad), (0, 0), (0, 0), (0, 0)))
    slc = slc.reshape(Bx, n_slc, SEL_BLOCK, 2, G, HEAD_DIM).transpose(3, 0, 4, 1, 2, 5)
    return (ck, cv, cmp_end, overlap, slc[0], slc[1], min(N_SEL, n_slc))


def nsa_attend(q, q_rot, gate, qpos, ctx, win_k, win_v, win_pos):
    ck, cv, cmp_end, overlap, slc_kb, slc_vb, n_sel = ctx
    Bx, Tq = q.shape[:2]
    G, Hg = NSA_KV_HEADS, NSA_HEADS // NSA_KV_HEADS
    scale = HEAD_DIM ** -0.5
    qg = q.reshape(Bx, Tq, G, Hg, HEAD_DIM)
    qr = q_rot.reshape(Bx, Tq, G, Hg, HEAD_DIM)
    s = jnp.einsum('btghd,bngd->btghn', qg, ck).astype(jnp.float32) * scale
    vis = (cmp_end[None, :] <= qpos[:, None])[None, :, None, None, :]
    p_cmp = masked_softmax(s, vis)
    o_cmp = jnp.einsum('btghn,bngd->btghd', p_cmp.astype(cv.dtype), cv)
    imp = jnp.einsum('btgn,nj->btgj', jnp.sum(p_cmp, axis=3), overlap)
    n_slc = slc_kb.shape[2]
    jb = jnp.arange(n_slc, dtype=jnp.int32)[None, :]
    cur = (qpos // SEL_BLOCK)[:, None]
    forced = (jb == 0) | (jb == cur) | (jb == cur - 1)
    imp = jnp.where(forced[None, :, None, :], FORCED_SCORE, imp)
    imp = jnp.where((jb > cur)[None, :, None, :], NEG, imp)
    top_val, top_idx = lax.top_k(imp, n_sel)
    blk_ok = top_val > 0.5 * NEG
    bi = jnp.arange(Bx)[:, None, None, None]
    gi = jnp.arange(G)[None, None, :, None]
    sk = slc_kb[bi, gi, top_idx].reshape(Bx, Tq, G, n_sel * SEL_BLOCK, HEAD_DIM)
    sv = slc_vb[bi, gi, top_idx].reshape(Bx, Tq, G, n_sel * SEL_BLOCK, HEAD_DIM)
    kpos = top_idx[..., None] * SEL_BLOCK + jnp.arange(SEL_BLOCK, dtype=jnp.int32)
    ok = (blk_ok[..., None] & (kpos <= qpos[None, :, None, None, None])).reshape(Bx, Tq, G, 1, n_sel * SEL_BLOCK)
    s = jnp.einsum('btghd,btgmd->btghm', qr, sk).astype(jnp.float32) * scale
    o_slc = jnp.einsum('btghm,btgmd->btghd', masked_softmax(s, ok).astype(sv.dtype), sv)
    s = jnp.einsum('btghd,bwgd->btghw', qr, win_k).astype(jnp.float32) * scale
    dist = qpos[:, None] - win_pos[None, :]
    wok = ((dist >= 0) & (dist < WINDOW) & (win_pos[None, :] >= 0))[None, :, None, None, :]
    o_win = jnp.einsum('btghw,bwgd->btghd', masked_softmax(s, wok).astype(win_v.dtype), win_v)
    g = gate.reshape(Bx, Tq, G, Hg, 3)
    o = g[..., 0:1] * o_cmp + g[..., 1:2] * o_slc + g[..., 2:3] * o_win
    return o.reshape(Bx, Tq, NSA_HEADS * HEAD_DIM).astype(q.dtype)


def dsa_attend(q, idx_q, idx_w, qpos, k, v, idx_k, topk):
    Bx, Tq = q.shape[:2]
    L = k.shape[1]
    G, Hg = DSA_KV_HEADS, DSA_HEADS // DSA_KV_HEADS
    causal = jnp.arange(L, dtype=jnp.int32)[None, :] <= qpos[:, None]
    dots = jnp.einsum('bthd,bsd->bths', idx_q, idx_k).astype(jnp.float32) * (IDX_DIM ** -0.5)
    score = jnp.einsum('bths,bth->bts', jax.nn.relu(dots), idx_w.astype(jnp.float32)) * (IDX_HEADS ** -0.5)
    score = jnp.where(causal[None], score, NEG)
    _, sel = lax.top_k(score, topk)
    bi = jnp.arange(Bx)[:, None, None]
    sk = k[bi, sel]
    sv = v[bi, sel]
    ok = (sel <= qpos[None, :, None])[:, :, None, None, :]
    qg = q.reshape(Bx, Tq, G, Hg, HEAD_DIM)
    s = jnp.einsum('btghd,btkgd->btghk', qg, sk).astype(jnp.float32) * (HEAD_DIM ** -0.5)
    o = jnp.einsum('btghk,btkgd->btghd', masked_softmax(s, ok).astype(sv.dtype), sv)
    return o.reshape(Bx, Tq, DSA_HEADS * HEAD_DIM)


def mixer_out(x, gate_ada, f, o_n, o_d, w_up_nsa, w_up_dsa, w_out):
    u_n = (o_n * jax.nn.silu(f['nsa_z'])) @ w_up_nsa
    u_d = (o_d * jax.nn.silu(f['dsa_z'])) @ w_up_dsa
    g_n, g_d = jnp.split(jax.nn.sigmoid(f['merge']), 2, axis=-1)
    return x + gate_ada[:, None, :] * ((g_n * u_n + g_d * u_d) @ w_out)


def gather_pages(pool, page_table):
    g = pool[page_table]
    return g.reshape((page_table.shape[0], -1) + pool.shape[2:])


def prompt_layer(x, c, norm_g, w_ada, b_ada, w_in, cmp_pe, cmp_w1, cmp_w2, w_up_nsa, w_up_dsa, w_out):
    Bx, T, _ = x.shape
    pos = jnp.arange(T, dtype=jnp.int32)
    h, gate_ada = ada_norm(x, c, norm_g, w_ada, b_ada)
    f, nsa_rows, win_rows, dsa_rows, idx_rows = in_projection(h, w_in, pos)
    ctx = nsa_context(nsa_rows, cmp_pe, cmp_w1, cmp_w2)
    topk = min(DSA_TOPK_MAX, T // 4)
    win_pad = jnp.pad(win_rows, ((0, 0), (WINDOW, 0), (0, 0), (0, 0), (0, 0)))
    dsa_k, dsa_v = dsa_rows[:, :, 0], dsa_rows[:, :, 1]

    def block(i):
        qs = i * Q_BLOCK
        sl = lambda a: lax.dynamic_slice_in_dim(a, qs, Q_BLOCK, axis=1)
        qpos = qs + jnp.arange(Q_BLOCK, dtype=jnp.int32)
        wkv = lax.dynamic_slice_in_dim(win_pad, qs, WINDOW + Q_BLOCK, axis=1)
        wpos = qs - WINDOW + jnp.arange(WINDOW + Q_BLOCK, dtype=jnp.int32)
        o_n = nsa_attend(sl(f['nsa_q']), sl(f['nsa_q_rot']), sl(f['nsa_gate']), qpos, ctx,
                         wkv[:, :, 0], wkv[:, :, 1], wpos)
        o_d = dsa_attend(sl(f['dsa_q']), sl(f['idx_q']), sl(f['idx_w']), qpos, dsa_k, dsa_v, idx_rows, topk)
        return o_n, o_d

    o_n, o_d = lax.map(block, jnp.arange(T // Q_BLOCK, dtype=jnp.int32))
    o_n = jnp.swapaxes(o_n, 0, 1).reshape(Bx, T, NSA_HEADS * HEAD_DIM)
    o_d = jnp.swapaxes(o_d, 0, 1).reshape(Bx, T, DSA_HEADS * HEAD_DIM)
    y = mixer_out(x, gate_ada, f, o_n, o_d, w_up_nsa, w_up_dsa, w_out)
    return y, nsa_rows, dsa_rows, idx_rows, win_rows[:, T - min(WINDOW, T):]


def sample_layer(x, c, cache_nsa, cache_dsa, cache_idx, win_state, page_table,
                 norm_g, w_ada, b_ada, w_in, cmp_pe, cmp_w1, cmp_w2, w_up_nsa, w_up_dsa, w_out):
    Bx, T, _ = x.shape
    past = page_table.shape[1] * cache_nsa.shape[1]
    pos = past + jnp.arange(T, dtype=jnp.int32)
    h, gate_ada = ada_norm(x, c, norm_g, w_ada, b_ada)
    f, nsa_rows, win_rows, dsa_rows, idx_rows = in_projection(h, w_in, pos)
    nsa_all = jnp.concatenate([gather_pages(cache_nsa, page_table), nsa_rows], axis=1)
    dsa_all = jnp.concatenate([gather_pages(cache_dsa, page_table), dsa_rows], axis=1)
    idx_all = jnp.concatenate([gather_pages(cache_idx, page_table), idx_rows], axis=1)
    win_all = jnp.concatenate([win_state, win_rows], axis=1)
    wb = win_state.shape[1]
    wpos = past - wb + jnp.arange(wb + T, dtype=jnp.int32)
    ctx = nsa_context(nsa_all, cmp_pe, cmp_w1, cmp_w2)
    topk = min(DSA_TOPK_MAX, (past + T) // 4)
    o_n = nsa_attend(f['nsa_q'], f['nsa_q_rot'], f['nsa_gate'], pos, ctx, win_all[:, :, 0], win_all[:, :, 1], wpos)
    o_d = dsa_attend(f['dsa_q'], f['idx_q'], f['idx_w'], pos, dsa_all[:, :, 0], dsa_all[:, :, 1], idx_all, topk)
    y = mixer_out(x, gate_ada, f, o_n, o_d, w_up_nsa, w_up_dsa, w_out)
    return y, nsa_rows, dsa_rows, idx_rows, win_all[:, T:]


def setup_inputs(seed: int = 0) -> dict:
    key = jax.random.key(seed)
    ks = jax.random.split(key, 20)
    f32 = jnp.float32
    n_pages = PAST_LEN // PAGE_SIZE
    n_used = DEC_BATCH * n_pages
    n_pool = n_used + (n_used + 3) // 4
    wb = min(WINDOW, PAST_LEN)
    width = NSA_HEADS * HEAD_DIM

    def nrm(k, shape, s=1.0):
        return jax.random.normal(k, shape, f32) * s

    page_table = jax.random.permutation(ks[0], n_pool)[:n_used].reshape(DEC_BATCH, n_pages).astype(jnp.int32)
    return {
        'x_prompt': nrm(ks[1], (BATCH, SEQ, D_MODEL)),
        'x_sample': nrm(ks[2], (DEC_BATCH, DEC_SEQ, D_MODEL)),
        'cache_nsa_kv': nrm(ks[3], (DEPTH, n_pool, PAGE_SIZE, 4, NSA_KV_HEADS, HEAD_DIM)),
        'cache_dsa_kv': nrm(ks[4], (DEPTH, n_pool, PAGE_SIZE, 2, DSA_KV_HEADS, HEAD_DIM)),
        'cache_dsa_idx_k': nrm(ks[5], (DEPTH, n_pool, PAGE_SIZE, IDX_DIM)),
        'state_nsa_win_kv': nrm(ks[6], (DEPTH, DEC_BATCH, wb, 2, NSA_KV_HEADS, HEAD_DIM)),
        'page_table': page_table,
        'c_prompt': nrm(ks[7], (BATCH, D_MODEL)),
        'c_sample': nrm(ks[8], (DEC_BATCH, D_MODEL)),
        'norm_g': 1.0 + nrm(ks[9], (DEPTH, D_MODEL), 0.02),
        'w_ada': nrm(ks[10], (DEPTH, D_MODEL, 3 * D_MODEL), 0.5 * D_MODEL ** -0.5),
        'b_ada': nrm(ks[11], (DEPTH, 3 * D_MODEL), 0.01),
        'w_in': nrm(ks[12], (DEPTH, D_MODEL, IN_TOTAL), D_MODEL ** -0.5),
        'cmp_pe': nrm(ks[13], (DEPTH, 2, CMP_BLOCK, HEAD_DIM), 0.1),
        'cmp_w1': nrm(ks[14], (DEPTH, 2, CMP_BLOCK * HEAD_DIM, CMP_HIDDEN), (CMP_BLOCK * HEAD_DIM) ** -0.5),
        'cmp_w2': nrm(ks[15], (DEPTH, 2, CMP_HIDDEN, HEAD_DIM), CMP_HIDDEN ** -0.5),
        'w_up_nsa': nrm(ks[16], (DEPTH, width, D_MODEL), width ** -0.5),
        'w_up_dsa': nrm(ks[17], (DEPTH, DSA_HEADS * HEAD_DIM, D_MODEL), (DSA_HEADS * HEAD_DIM) ** -0.5),
        'w_out': nrm(ks[18], (DEPTH, D_MODEL, D_MODEL), D_MODEL ** -0.5),
        'final_g': 1.0 + nrm(ks[19], (D_MODEL,), 0.02),
    }


def reference(x_prompt, x_sample, cache_nsa_kv, cache_dsa_kv, cache_dsa_idx_k, state_nsa_win_kv, page_table,
              c_prompt, c_sample, norm_g, w_ada, b_ada, w_in, cmp_pe, cmp_w1, cmp_w2,
              w_up_nsa, w_up_dsa, w_out, final_g):
    yp, ys = x_prompt, x_sample
    l_pn, l_pd, l_pi, l_pw, l_sn, l_sd, l_si, l_sw = [], [], [], [], [], [], [], []
    for l in range(DEPTH):
        w = (norm_g[l], w_ada[l], b_ada[l], w_in[l], cmp_pe[l], cmp_w1[l], cmp_w2[l],
             w_up_nsa[l], w_up_dsa[l], w_out[l])
        yp, pn, pd, pi, pw = prompt_layer(yp, c_prompt, *w)
        ys, sn, sd, si, sw = sample_layer(ys, c_sample, cache_nsa_kv[l], cache_dsa_kv[l], cache_dsa_idx_k[l],
                                          state_nsa_win_kv[l], page_table, *w)
        l_pn.append(pn); l_pd.append(pd); l_pi.append(pi); l_pw.append(pw)
        l_sn.append(sn); l_sd.append(sd); l_si.append(si); l_sw.append(sw)
    y_prompt = rms_norm(yp, final_g)
    y_sample = rms_norm(ys, final_g)
    new_nsa_kv_prompt = jnp.stack(l_pn, axis=0)
    new_dsa_kv_prompt = jnp.stack(l_pd, axis=0)
    new_idx_k_prompt = jnp.stack(l_pi, axis=0)
    new_win_prompt = jnp.stack(l_pw, axis=0)
    new_nsa_kv_sample = jnp.stack(l_sn, axis=0)
    new_dsa_kv_sample = jnp.stack(l_sd, axis=0)
    new_idx_k_sample = jnp.stack(l_si, axis=0)
    new_win_sample = jnp.stack(l_sw, axis=0)
    return (y_prompt, y_sample, new_nsa_kv_prompt, new_dsa_kv_prompt, new_idx_k_prompt, new_win_prompt,
            new_nsa_kv_sample, new_dsa_kv_sample, new_idx_k_sample, new_win_sample)
```

```python
import functools
import math

import numpy as np
import jax
import jax.numpy as jnp
from jax import lax
from jax.experimental import pallas as pl
from jax.experimental.pallas import tpu as pltpu

F32 = jnp.float32
BF16 = jnp.bfloat16

HEAD_DIM = 64
NSA_HEADS = 8
NSA_KV_HEADS = 2
DSA_HEADS = 8
DSA_KV_HEADS = 2
N_NSA_KV = 6
CMP_BLOCK = 32
CMP_STRIDE = 16
SEL_BLOCK = 64
N_SEL = 16
WINDOW = 512
IDX_HEADS = 4
IDX_DIM = 64
DSA_TOPK_MAX = 256
ROPE_THETA = 10000.0
Q_BLOCK = 128
EPS = 1e-6
NEG = -1e30
FORCED_SCORE = 1e4

LANES = 128
VMEM_LIMIT = 56 * 1024 * 1024

W_NQ = NSA_HEADS * HEAD_DIM
W_NKV = N_NSA_KV * NSA_KV_HEADS * HEAD_DIM
W_NG = NSA_HEADS * 3
W_DQ = DSA_HEADS * HEAD_DIM
W_DKV = 2 * DSA_KV_HEADS * HEAD_DIM
W_IQ = IDX_HEADS * IDX_DIM


def _cuts(d_model):
    widths = (W_NQ, W_NKV, W_NG, W_NQ, W_DQ, W_DKV, W_IQ, IDX_DIM, IDX_HEADS, W_DQ, 2 * d_model)
    c = np.concatenate([[0], np.cumsum(widths)])
    names = ('nq', 'nkv', 'ng', 'nz', 'dq', 'dkv', 'iq', 'ik', 'iw', 'dz', 'mg')
    return {n: (int(c[i]), int(c[i + 1])) for i, n in enumerate(names)}


def _split3_rows(w):
    wh = w.astype(BF16)
    wl = (w - wh.astype(F32)).astype(BF16)
    return jnp.concatenate([wh, wl, wh], axis=0)


def _split3_cols(x):
    xh = x.astype(BF16)
    xl = (x - xh.astype(F32)).astype(BF16)
    return jnp.concatenate([xh, xh, xl], axis=1)


def _silu(v):
    return v * (1.0 / (1.0 + jnp.exp(-v)))


def _sigmoid(v):
    return 1.0 / (1.0 + jnp.exp(-v))


def _ada_kernel(c_ref, w_ref, b_ref, o_ref):
    c3 = _split3_cols(_silu(c_ref[...]))
    o_ref[...] = jnp.dot(c3, w_ref[...], preferred_element_type=F32) + b_ref[...]


def _ada_mod(c_all, w_ada, b_ada):
    m, d = c_all.shape
    n = w_ada.shape[1]
    tn = 512
    w3 = _split3_rows(w_ada)
    return pl.pallas_call(
        _ada_kernel,
        grid=(n // tn,),
        in_specs=[pl.BlockSpec((m, d), lambda j: (0, 0)),
                  pl.BlockSpec((3 * d, tn), lambda j: (0, j)),
                  pl.BlockSpec((1, tn), lambda j: (0, j))],
        out_specs=pl.BlockSpec((m, tn), lambda j: (0, j)),
        out_shape=jax.ShapeDtypeStruct((m, n), F32),
        name='ada_mod',
    )(c_all, w3, b_ada.reshape(1, n))


def _rope_tf(v, cos, sin_signed):
    first_half = (lax.broadcasted_iota(jnp.int32, (1, LANES), 1) % HEAD_DIM) < (HEAD_DIM // 2)
    outs = []
    for c in range(v.shape[1] // LANES):
        vc = v[:, c * LANES:(c + 1) * LANES]
        sw = jnp.where(first_half, pltpu.roll(vc, LANES - HEAD_DIM // 2, 1), pltpu.roll(vc, HEAD_DIM // 2, 1))
        outs.append(vc * cos + sw * sin_signed)
    return jnp.concatenate(outs, axis=1) if len(outs) > 1 else outs[0]


def _rope_ft(v, cos_t, sin_t):
    half = HEAD_DIM // 2
    outs = []
    for hd in range(v.shape[0] // HEAD_DIM):
        x1 = v[hd * HEAD_DIM:hd * HEAD_DIM + half]
        x2 = v[hd * HEAD_DIM + half:(hd + 1) * HEAD_DIM]
        outs.append(x1 * cos_t - x2 * sin_t)
        outs.append(x2 * cos_t + x1 * sin_t)
    return jnp.concatenate(outs, axis=0)


def _inproj_kernel(x_ref, g_ref, sc_ref, sh_ref, cos_ref, sin_ref, cost_ref, sint_ref,
                   wp_ref, wa_ref, wt_ref, wti_ref,
                   q_ref, qrot_ref, iq_ref, small_ref, ck_ref,
                   cv_ref, snz_ref, sdz_ref, dq_ref, mg_ref,
                   nsat_ref, wint_ref, dsat_ref, idxt_ref):
    x = x_ref[...]
    y = x * lax.rsqrt(jnp.mean(x * x, axis=-1, keepdims=True) + EPS) * g_ref[...]
    h = y * (1.0 + sc_ref[...]) + sh_ref[...]
    hh = h.astype(BF16)
    hl = (h - hh.astype(F32)).astype(BF16)
    h3 = jnp.concatenate([hh, hh, hl], axis=1)
    cos = cos_ref[...]
    sin = sin_ref[...]
    cos_t = cost_ref[...]
    sin_t = sint_ref[...]

    p = jnp.dot(h3, wp_ref[...], preferred_element_type=F32)
    q = p[:, 0:W_NQ]
    q_ref[...] = q
    qrot_ref[...] = _rope_tf(q, cos, sin)
    iq_ref[...] = _rope_tf(p[:, W_NQ:W_NQ + W_IQ], cos, sin)
    small_ref[...] = p[:, W_NQ + W_IQ:W_NQ + W_IQ + LANES]
    ck_ref[...] = p[:, W_NQ + W_IQ + LANES:W_NQ + W_IQ + 2 * LANES]

    a = jnp.dot(hh, wa_ref[...], preferred_element_type=F32)
    cv_ref[...] = a[:, 0:LANES]
    o = LANES
    snz_ref[...] = _silu(a[:, o:o + W_NQ])
    o += W_NQ
    sdz_ref[...] = _silu(a[:, o:o + W_DQ])
    o += W_DQ
    dq_ref[...] = _rope_tf(a[:, o:o + W_DQ], cos, sin)
    o += W_DQ
    mg_ref[...] = _sigmoid(a[:, o:])

    nt = (((1,), (1,)), ((), ()))
    t = lax.dot_general(wt_ref[...], hh, nt, preferred_element_type=F32)
    kv = 2 * HEAD_DIM
    nsat_ref[0:2 * kv, :] = t[0:2 * kv]
    nsat_ref[2 * kv:3 * kv, :] = _rope_ft(t[2 * kv:3 * kv], cos_t, sin_t)
    nsat_ref[3 * kv:4 * kv, :] = t[3 * kv:4 * kv]
    wint_ref[0:kv, :] = _rope_ft(t[4 * kv:5 * kv], cos_t, sin_t)
    wint_ref[kv:2 * kv, :] = t[5 * kv:6 * kv]
    dsat_ref[0:kv, :] = _rope_ft(t[6 * kv:7 * kv], cos_t, sin_t)
    dsat_ref[kv:2 * kv, :] = t[7 * kv:8 * kv]
    ti = lax.dot_general(wti_ref[...], h3, nt, preferred_element_type=F32)
    idxt_ref[...] = _rope_ft(ti, cos_t, sin_t)


def _prep_inproj_weights(w_in, d_model):
    c = _cuts(d_model)
    sl = lambda name: w_in[:, c[name][0]:c[name][1]]
    nkv = sl('nkv')
    kv = NSA_KV_HEADS * HEAD_DIM
    small = jnp.concatenate([sl('iw'), sl('ng'), jnp.zeros((d_model, LANES - IDX_HEADS - W_NG), F32)], axis=1)
    wp = jnp.concatenate([sl('nq'), sl('iq'), small, nkv[:, 0:kv]], axis=1)
    wa = jnp.concatenate([nkv[:, kv:2 * kv], sl('nz'), sl('dz'), sl('dq'), sl('mg')], axis=1)
    wt = jnp.concatenate([nkv, sl('dkv')], axis=1).T
    wti = sl('ik')
    return _split3_rows(wp), wa.astype(BF16), wt.astype(BF16), _split3_rows(wti).T


def _rope_tables(pos):
    half = HEAD_DIM // 2
    inv = ROPE_THETA ** (-jnp.arange(half, dtype=F32) / half)
    ang = pos.astype(F32)[:, None] * inv[None, :]
    cos = jnp.cos(ang)
    sin = jnp.sin(ang)
    cos_tf = jnp.tile(cos, (1, LANES // half))
    sin_tf = jnp.tile(jnp.concatenate([-sin, sin], axis=1), (1, LANES // HEAD_DIM))
    return cos_tf, sin_tf, cos.T, sin.T


def _in_projection(x2d, norm_g, scale, shift, rows_per_mod, pos, weights, tm):
    r, d = x2d.shape
    wp, wa, wt, wti = weights
    n_tiles = r // tm
    cos_tf, sin_tf, cos_t, sin_t = _rope_tables(pos)
    p_tiles = pos.shape[0] // tm
    if rows_per_mod == 1:
        mod_spec = pl.BlockSpec((tm, d), lambda i: (i, 0))
        sc, sh = scale, shift
    else:
        per = rows_per_mod // tm
        mod_spec = pl.BlockSpec((None, 1, d), lambda i: (i // per, 0, 0))
        sc, sh = scale[:, None, :], shift[:, None, :]
    const = lambda shape: pl.BlockSpec(shape, lambda i: (0,) * len(shape), pipeline_mode=pl.Buffered(1))
    tf = lambda w: pl.BlockSpec((tm, w), lambda i: (i, 0))
    ft = lambda w: pl.BlockSpec((w, tm), lambda i: (0, i))
    tf_widths = (W_NQ, W_NQ, W_IQ, LANES, LANES, LANES, W_NQ, W_DQ, W_DQ, 2 * d)
    ft_widths = (4 * LANES, 2 * LANES, 2 * LANES, IDX_DIM)
    outs = pl.pallas_call(
        _inproj_kernel,
        grid=(n_tiles,),
        in_specs=[pl.BlockSpec((tm, d), lambda i: (i, 0)),
                  const((1, d)), mod_spec, mod_spec,
                  pl.BlockSpec((tm, LANES), lambda i: (i % p_tiles, 0)),
                  pl.BlockSpec((tm, LANES), lambda i: (i % p_tiles, 0)),
                  pl.BlockSpec((HEAD_DIM // 2, tm), lambda i: (0, i % p_tiles)),
                  pl.BlockSpec((HEAD_DIM // 2, tm), lambda i: (0, i % p_tiles)),
                  const(wp.shape), const(wa.shape), const(wt.shape), const(wti.shape)],
        out_specs=[tf(w) for w in tf_widths] + [ft(w) for w in ft_widths],
        out_shape=[jax.ShapeDtypeStruct((r, w), F32) for w in tf_widths]
        + [jax.ShapeDtypeStruct((w, r), F32) for w in ft_widths],
        compiler_params=pltpu.CompilerParams(dimension_semantics=('arbitrary',), vmem_limit_bytes=VMEM_LIMIT),
        name='in_projection',
    )(x2d, norm_g.reshape(1, d), sc, sh, cos_tf, sin_tf, cos_t, sin_t, wp, wa, wt, wti)
    names = ('nsa_q', 'nsa_q_rot', 'idx_q', 'small', 'cmp_k', 'cmp_v', 'silu_nz', 'silu_dz', 'dsa_q', 'merge',
             'nsa_t', 'win_t', 'dsa_t', 'idx_t')
    return dict(zip(names, outs))


def _mixer_out_kernel(x_ref, on_ref, od_ref, snz_ref, sdz_ref, mg_ref, ga_ref, fg_ref,
                      wun_ref, wud_ref, wo_ref, y_ref):
    d = x_ref.shape[1]
    u_n = jnp.dot((on_ref[...] * snz_ref[...]).astype(BF16), wun_ref[...], preferred_element_type=F32)
    u_d = jnp.dot((od_ref[...] * sdz_ref[...]).astype(BF16), wud_ref[...], preferred_element_type=F32)
    mg = mg_ref[...]
    mix = mg[:, 0:d] * u_n + mg[:, d:2 * d] * u_d
    z = x_ref[...] + ga_ref[...] * jnp.dot(mix.astype(BF16), wo_ref[...], preferred_element_type=F32)
    y_ref[...] = z * lax.rsqrt(jnp.mean(z * z, axis=-1, keepdims=True) + EPS) * fg_ref[...]


def _mixer_out(x2d, o_n, o_d, f, gate, rows_per_mod, final_g, w_up_nsa, w_up_dsa, w_out, tm):
    r, d = x2d.shape
    if rows_per_mod == 1:
        mod_spec = pl.BlockSpec((tm, d), lambda i: (i, 0))
        ga = gate
    else:
        per = rows_per_mod // tm
        mod_spec = pl.BlockSpec((None, 1, d), lambda i: (i // per, 0, 0))
        ga = gate[:, None, :]
    const = lambda shape: pl.BlockSpec(shape, lambda i: (0,) * len(shape), pipeline_mode=pl.Buffered(1))
    tf = lambda w: pl.BlockSpec((tm, w), lambda i: (i, 0))
    return pl.pallas_call(
        _mixer_out_kernel,
        grid=(r // tm,),
        in_specs=[tf(d), tf(W_NQ), tf(W_DQ), tf(W_NQ), tf(W_DQ), tf(2 * d), mod_spec, const((1, d)),
                  const(w_up_nsa.shape), const(w_up_dsa.shape), const(w_out.shape)],
        out_specs=tf(d),
        out_shape=jax.ShapeDtypeStruct((r, d), F32),
        compiler_params=pltpu.CompilerParams(dimension_semantics=('arbitrary',), vmem_limit_bytes=VMEM_LIMIT),
        name='mixer_out',
    )(x2d, o_n, o_d, f['silu_nz'], f['silu_dz'], f['merge'], ga, final_g.reshape(1, d),
      w_up_nsa.astype(BF16), w_up_dsa.astype(BF16), w_out.astype(BF16))


def _masked_softmax(s, mask):
    s = jnp.where(mask, s, NEG)
    p = jnp.where(mask, jnp.exp(s - jnp.max(s, axis=-1, keepdims=True)), 0.0)
    return p / jnp.maximum(jnp.sum(p, axis=-1, keepdims=True), 1e-30)


def _nsa_context(nsa_all, cmp_pe, cmp_w1, cmp_w2):
    Bx, L = nsa_all.shape[:2]
    G = NSA_KV_HEADS
    n_cmp = (L - CMP_BLOCK) // CMP_STRIDE + 1
    starts = np.arange(n_cmp, dtype=np.int32) * CMP_STRIDE
    blk_idx = starts[:, None] + np.arange(CMP_BLOCK, dtype=np.int32)[None, :]

    def compress(rows, pe, w1, w2):
        blk = rows[:, blk_idx] + pe[None, None, :, None, :]
        flat = jnp.swapaxes(blk, 2, 3).reshape(Bx, n_cmp, G, CMP_BLOCK * HEAD_DIM)
        hp = jnp.dot(flat, w1, precision=lax.Precision.HIGHEST)
        return jnp.dot(jax.nn.silu(hp), w2, precision=lax.Precision.HIGHEST)

    ck = compress(nsa_all[:, :, 0], cmp_pe[0], cmp_w1[0], cmp_w2[0])
    cv = compress(nsa_all[:, :, 1], cmp_pe[1], cmp_w1[1], cmp_w2[1])
    cmp_end = jnp.asarray(starts + CMP_BLOCK - 1)
    n_slc = -(-L // SEL_BLOCK)
    j = np.arange(n_slc, dtype=np.int32)[None, :]
    overlap = (starts[:, None] < (j + 1) * SEL_BLOCK) & (starts[:, None] + CMP_BLOCK > j * SEL_BLOCK)
    overlap = jnp.asarray(overlap, dtype=F32)
    pad = n_slc * SEL_BLOCK - L
    slc = jnp.pad(nsa_all[:, :, 2:4], ((0, 0), (0, pad), (0, 0), (0, 0), (0, 0)))
    slc = slc.reshape(Bx, n_slc, SEL_BLOCK, 2, G, HEAD_DIM).transpose(3, 0, 4, 1, 2, 5)
    return (ck, cv, cmp_end, overlap, slc[0], slc[1], min(N_SEL, n_slc))


def _nsa_attend(q, q_rot, gate, qpos, ctx, win_k, win_v, win_pos):
    hp = lax.Precision.HIGHEST
    ck, cv, cmp_end, overlap, slc_kb, slc_vb, n_sel = ctx
    Bx, Tq = q.shape[:2]
    G, Hg = NSA_KV_HEADS, NSA_HEADS // NSA_KV_HEADS
    scale = HEAD_DIM ** -0.5
    qg = q.reshape(Bx, Tq, G, Hg, HEAD_DIM)
    qr = q_rot.reshape(Bx, Tq, G, Hg, HEAD_DIM)
    s = jnp.einsum('btghd,bngd->btghn', qg, ck, precision=hp).astype(F32) * scale
    vis = (cmp_end[None, :] <= qpos[:, None])[None, :, None, None, :]
    p_cmp = _masked_softmax(s, vis)
    o_cmp = jnp.einsum('btghn,bngd->btghd', p_cmp, cv)
    imp = jnp.einsum('btgn,nj->btgj', jnp.sum(p_cmp, axis=3), overlap, precision=hp)
    n_slc = slc_kb.shape[2]
    jb = jnp.arange(n_slc, dtype=jnp.int32)[None, :]
    cur = (qpos // SEL_BLOCK)[:, None]
    forced = (jb == 0) | (jb == cur) | (jb == cur - 1)
    imp = jnp.where(forced[None, :, None, :], FORCED_SCORE, imp)
    imp = jnp.where((jb > cur)[None, :, None, :], NEG, imp)
    top_val, top_idx = lax.top_k(imp, n_sel)
    blk_ok = top_val > 0.5 * NEG
    bi = jnp.arange(Bx)[:, None, None, None]
    gi = jnp.arange(G)[None, None, :, None]
    sk = slc_kb[bi, gi, top_idx].reshape(Bx, Tq, G, n_sel * SEL_BLOCK, HEAD_DIM)
    sv = slc_vb[bi, gi, top_idx].reshape(Bx, Tq, G, n_sel * SEL_BLOCK, HEAD_DIM)
    kpos = top_idx[..., None] * SEL_BLOCK + jnp.arange(SEL_BLOCK, dtype=jnp.int32)
    ok = (blk_ok[..., None] & (kpos <= qpos[None, :, None, None, None])).reshape(Bx, Tq, G, 1, n_sel * SEL_BLOCK)
    s = jnp.einsum('btghd,btgmd->btghm', qr, sk).astype(F32) * scale
    o_slc = jnp.einsum('btghm,btgmd->btghd', _masked_softmax(s, ok), sv)
    s = jnp.einsum('btghd,bwgd->btghw', qr, win_k).astype(F32) * scale
    dist = qpos[:, None] - win_pos[None, :]
    wok = ((dist >= 0) & (dist < WINDOW) & (win_pos[None, :] >= 0))[None, :, None, None, :]
    o_win = jnp.einsum('btghw,bwgd->btghd', _masked_softmax(s, wok), win_v)
    g = gate.reshape(Bx, Tq, G, Hg, 3)
    o = g[..., 0:1] * o_cmp + g[..., 1:2] * o_slc + g[..., 2:3] * o_win
    return o.reshape(Bx, Tq, NSA_HEADS * HEAD_DIM)


def _dsa_attend(q, idx_q, idx_w, qpos, k, v, idx_k, topk):
    hp = lax.Precision.HIGHEST
    Bx, Tq = q.shape[:2]
    L = k.shape[1]
    G, Hg = DSA_KV_HEADS, DSA_HEADS // DSA_KV_HEADS
    causal = jnp.arange(L, dtype=jnp.int32)[None, :] <= qpos[:, None]
    dots = jnp.einsum('bthd,bsd->bths', idx_q, idx_k, precision=hp).astype(F32) * (IDX_DIM ** -0.5)
    score = jnp.einsum('bths,bth->bts', jax.nn.relu(dots), idx_w, precision=hp) * (IDX_HEADS ** -0.5)
    score = jnp.where(causal[None], score, NEG)
    _, sel = lax.top_k(score, topk)
    bi = jnp.arange(Bx)[:, None, None]
    sk = k[bi, sel]
    sv = v[bi, sel]
    ok = (sel <= qpos[None, :, None])[:, :, None, None, :]
    qg = q.reshape(Bx, Tq, G, Hg, HEAD_DIM)
    s = jnp.einsum('btghd,btkgd->btghk', qg, sk).astype(F32) * (HEAD_DIM ** -0.5)
    o = jnp.einsum('btghk,btkgd->btghd', _masked_softmax(s, ok), sv)
    return o.reshape(Bx, Tq, DSA_HEADS * HEAD_DIM)


def _rows_from_t(a_t, bx, t, lead):
    return a_t.reshape(lead + (bx, t)).transpose((len(lead), len(lead) + 1) + tuple(range(len(lead))))


def kernel(x_prompt, x_sample, cache_nsa_kv, cache_dsa_kv, cache_dsa_idx_k, state_nsa_win_kv, page_table,
           c_prompt, c_sample, norm_g, w_ada, b_ada, w_in, cmp_pe, cmp_w1, cmp_w2,
           w_up_nsa, w_up_dsa, w_out, final_g):
    depth = norm_g.shape[0]
    assert depth == 1
    B, T, D = x_prompt.shape
    Bd, Td, _ = x_sample.shape
    l = 0
    past = page_table.shape[1] * cache_nsa_kv.shape[2]

    c_all = jnp.concatenate([c_prompt, c_sample], axis=0)
    pad = (-c_all.shape[0]) % 8
    c_all = jnp.pad(c_all, ((0, pad), (0, 0)))
    mod = _ada_mod(c_all, w_ada[l], b_ada[l])
    shift, scale, gate = mod[:, 0:D], mod[:, D:2 * D], mod[:, 2 * D:3 * D]

    weights = _prep_inproj_weights(w_in[l], D)

    tm = 256
    pos_p = jnp.arange(T, dtype=jnp.int32)
    fp = _in_projection(x_prompt.reshape(B * T, D), norm_g[l], scale[:B], shift[:B], T, pos_p, weights, tm)
    nsa_rows = _rows_from_t(fp['nsa_t'], B, T, (4, NSA_KV_HEADS, HEAD_DIM))
    win_rows = _rows_from_t(fp['win_t'], B, T, (2, NSA_KV_HEADS, HEAD_DIM))
    dsa_rows = _rows_from_t(fp['dsa_t'], B, T, (2, DSA_KV_HEADS, HEAD_DIM))
    idx_rows = _rows_from_t(fp['idx_t'], B, T, (IDX_DIM,))
    nsa_ctx_rows = nsa_rows.at[:, :, 0].set(fp['cmp_k'].reshape(B, T, NSA_KV_HEADS, HEAD_DIM))
    ctx = _nsa_context(nsa_ctx_rows, cmp_pe[l], cmp_w1[l], cmp_w2[l])
    topk = min(DSA_TOPK_MAX, T // 4)
    win_pad = jnp.pad(win_rows, ((0, 0), (WINDOW, 0), (0, 0), (0, 0), (0, 0)))
    nsa_q = fp['nsa_q'].reshape(B, T, NSA_HEADS, HEAD_DIM)
    nsa_q_rot = fp['nsa_q_rot'].reshape(B, T, NSA_HEADS, HEAD_DIM)
    nsa_gate = _sigmoid(fp['small'][:, IDX_HEADS:IDX_HEADS + W_NG]).reshape(B, T, NSA_HEADS, 3)
    idx_w = fp['small'][:, 0:IDX_HEADS].reshape(B, T, IDX_HEADS)
    dsa_q = fp['dsa_q'].reshape(B, T, DSA_HEADS, HEAD_DIM)
    idx_q = fp['idx_q'].reshape(B, T, IDX_HEADS, IDX_DIM)

    def block(i):
        qs = i * Q_BLOCK
        sl = lambda a: lax.dynamic_slice_in_dim(a, qs, Q_BLOCK, axis=1)
        qpos = qs + jnp.arange(Q_BLOCK, dtype=jnp.int32)
        wkv = lax.dynamic_slice_in_dim(win_pad, qs, WINDOW + Q_BLOCK, axis=1)
        wpos = qs - WINDOW + jnp.arange(WINDOW + Q_BLOCK, dtype=jnp.int32)
        o_n = _nsa_attend(sl(nsa_q), sl(nsa_q_rot), sl(nsa_gate), qpos, ctx, wkv[:, :, 0], wkv[:, :, 1], wpos)
        o_d = _dsa_attend(sl(dsa_q), sl(idx_q), sl(idx_w), qpos, dsa_rows[:, :, 0], dsa_rows[:, :, 1], idx_rows, topk)
        return o_n, o_d

    o_n, o_d = lax.map(block, jnp.arange(T // Q_BLOCK, dtype=jnp.int32))
    o_n = jnp.swapaxes(o_n, 0, 1).reshape(B * T, W_NQ)
    o_d = jnp.swapaxes(o_d, 0, 1).reshape(B * T, W_DQ)
    y_prompt = _mixer_out(x_prompt.reshape(B * T, D), o_n, o_d, fp, gate[:B], T, final_g,
                          w_up_nsa[l], w_up_dsa[l], w_out[l], tm).reshape(B, T, D)
    new_win_prompt = win_rows[:, T - min(WINDOW, T):]

    Rs = Bd * Td
    tms = min(256, Rs)
    pos_s = past + jnp.arange(Td, dtype=jnp.int32)
    pos_rows = jnp.tile(pos_s, tms // Td)
    rep = lambda a: jnp.repeat(a[B:B + Bd], Td, axis=0)
    fs = _in_projection(x_sample.reshape(Rs, D), norm_g[l], rep(scale), rep(shift), 1, pos_rows, weights, tms)
    s_nsa_rows = _rows_from_t(fs['nsa_t'], Bd, Td, (4, NSA_KV_HEADS, HEAD_DIM))
    s_win_rows = _rows_from_t(fs['win_t'], Bd, Td, (2, NSA_KV_HEADS, HEAD_DIM))
    s_dsa_rows = _rows_from_t(fs['dsa_t'], Bd, Td, (2, DSA_KV_HEADS, HEAD_DIM))
    s_idx_rows = _rows_from_t(fs['idx_t'], Bd, Td, (IDX_DIM,))

    def gather_pages(pool):
        g = pool[page_table]
        return g.reshape((page_table.shape[0], -1) + pool.shape[2:])

    nsa_all = jnp.concatenate([gather_pages(cache_nsa_kv[l]), s_nsa_rows], axis=1)
    dsa_all = jnp.concatenate([gather_pages(cache_dsa_kv[l]), s_dsa_rows], axis=1)
    idx_all = jnp.concatenate([gather_pages(cache_dsa_idx_k[l]), s_idx_rows], axis=1)
    win_state = state_nsa_win_kv[l]
    win_all = jnp.concatenate([win_state, s_win_rows], axis=1)
    wb = win_state.shape[1]
    wpos = past - wb + jnp.arange(wb + Td, dtype=jnp.int32)
    sctx = _nsa_context(nsa_all, cmp_pe[l], cmp_w1[l], cmp_w2[l])
    topk_s = min(DSA_TOPK_MAX, (past + Td) // 4)
    so_n = _nsa_attend(fs['nsa_q'].reshape(Bd, Td, NSA_HEADS, HEAD_DIM),
                       fs['nsa_q_rot'].reshape(Bd, Td, NSA_HEADS, HEAD_DIM),
                       _sigmoid(fs['small'][:, IDX_HEADS:IDX_HEADS + W_NG]).reshape(Bd, Td, NSA_HEADS, 3),
                       pos_s, sctx, win_all[:, :, 0], win_all[:, :, 1], wpos)
    so_d = _dsa_attend(fs['dsa_q'].reshape(Bd, Td, DSA_HEADS, HEAD_DIM),
                       fs['idx_q'].reshape(Bd, Td, IDX_HEADS, IDX_DIM),
                       fs['small'][:, 0:IDX_HEADS].reshape(Bd, Td, IDX_HEADS),
                       pos_s, dsa_all[:, :, 0], dsa_all[:, :, 1], idx_all, topk_s)
    y_sample = _mixer_out(x_sample.reshape(Rs, D), so_n.reshape(Rs, W_NQ), so_d.reshape(Rs, W_DQ), fs,
                          rep(gate), 1, final_g, w_up_nsa[l], w_up_dsa[l], w_out[l], tms).reshape(Bd, Td, D)
    new_win_sample = win_all[:, Td:]

    return (y_prompt, y_sample, nsa_rows[None], dsa_rows[None], idx_rows[None], new_win_prompt[None],
            s_nsa_rows[None], s_dsa_rows[None], s_idx_rows[None], new_win_sample[None])
```

```python
import functools
import math

import numpy as np
import jax
import jax.numpy as jnp
from jax import lax
from jax.experimental import pallas as pl
from jax.experimental.pallas import tpu as pltpu

F32 = jnp.float32
BF16 = jnp.bfloat16

HEAD_DIM = 64
NSA_HEADS = 8
NSA_KV_HEADS = 2
DSA_HEADS = 8
DSA_KV_HEADS = 2
N_NSA_KV = 6
CMP_BLOCK = 32
CMP_STRIDE = 16
SEL_BLOCK = 64
N_SEL = 16
WINDOW = 512
IDX_HEADS = 4
IDX_DIM = 64
DSA_TOPK_MAX = 256
ROPE_THETA = 10000.0
Q_BLOCK = 128
EPS = 1e-6
NEG = -1e30
FORCED_SCORE = 1e4

LANES = 128
VMEM_LIMIT = 56 * 1024 * 1024

W_NQ = NSA_HEADS * HEAD_DIM
W_NKV = N_NSA_KV * NSA_KV_HEADS * HEAD_DIM
W_NG = NSA_HEADS * 3
W_DQ = DSA_HEADS * HEAD_DIM
W_DKV = 2 * DSA_KV_HEADS * HEAD_DIM
W_IQ = IDX_HEADS * IDX_DIM


def _cuts(d_model):
    widths = (W_NQ, W_NKV, W_NG, W_NQ, W_DQ, W_DKV, W_IQ, IDX_DIM, IDX_HEADS, W_DQ, 2 * d_model)
    c = np.concatenate([[0], np.cumsum(widths)])
    names = ('nq', 'nkv', 'ng', 'nz', 'dq', 'dkv', 'iq', 'ik', 'iw', 'dz', 'mg')
    return {n: (int(c[i]), int(c[i + 1])) for i, n in enumerate(names)}


def _split3_rows(w):
    wh = w.astype(BF16)
    wl = (w - wh.astype(F32)).astype(BF16)
    return jnp.concatenate([wh, wl, wh], axis=0)


def _split3_cols(x):
    xh = x.astype(BF16)
    xl = (x - xh.astype(F32)).astype(BF16)
    return jnp.concatenate([xh, xh, xl], axis=1)


def _silu(v):
    return v * (1.0 / (1.0 + jnp.exp(-v)))


def _sigmoid(v):
    return 1.0 / (1.0 + jnp.exp(-v))


def _ada_kernel(c_ref, w_ref, b_ref, o_ref):
    c3 = _split3_cols(_silu(c_ref[...]))
    o_ref[...] = jnp.dot(c3, w_ref[...], preferred_element_type=F32) + b_ref[...]


def _ada_mod(c_all, w_ada, b_ada):
    m, d = c_all.shape
    n = w_ada.shape[1]
    tn = 512
    w3 = _split3_rows(w_ada)
    return pl.pallas_call(
        _ada_kernel,
        grid=(n // tn,),
        in_specs=[pl.BlockSpec((m, d), lambda j: (0, 0)),
                  pl.BlockSpec((3 * d, tn), lambda j: (0, j)),
                  pl.BlockSpec((1, tn), lambda j: (0, j))],
        out_specs=pl.BlockSpec((m, tn), lambda j: (0, j)),
        out_shape=jax.ShapeDtypeStruct((m, n), F32),
        name='ada_mod',
    )(c_all, w3, b_ada.reshape(1, n))


def _rope_tf(v, cos, sin_signed):
    first_half = (lax.broadcasted_iota(jnp.int32, (1, LANES), 1) % HEAD_DIM) < (HEAD_DIM // 2)
    outs = []
    for c in range(v.shape[1] // LANES):
        vc = v[:, c * LANES:(c + 1) * LANES]
        sw = jnp.where(first_half, pltpu.roll(vc, LANES - HEAD_DIM // 2, 1), pltpu.roll(vc, HEAD_DIM // 2, 1))
        outs.append(vc * cos + sw * sin_signed)
    return jnp.concatenate(outs, axis=1) if len(outs) > 1 else outs[0]


def _rope_ft(v, cos_t, sin_t):
    half = HEAD_DIM // 2
    outs = []
    for hd in range(v.shape[0] // HEAD_DIM):
        x1 = v[hd * HEAD_DIM:hd * HEAD_DIM + half]
        x2 = v[hd * HEAD_DIM + half:(hd + 1) * HEAD_DIM]
        outs.append(x1 * cos_t - x2 * sin_t)
        outs.append(x2 * cos_t + x1 * sin_t)
    return jnp.concatenate(outs, axis=0)


def _inproj_kernel(x_ref, g_ref, sc_ref, sh_ref, cos_ref, sin_ref, cost_ref, sint_ref,
                   wp_ref, wa_ref, wt_ref, wti_ref,
                   q_ref, qrot_ref, iq_ref, small_ref, ck_ref,
                   cv_ref, snz_ref, sdz_ref, dq_ref, mg_ref,
                   nsat_ref, wint_ref, dsat_ref, idxt_ref, idx3_ref):
    x = x_ref[...]
    y = x * lax.rsqrt(jnp.mean(x * x, axis=-1, keepdims=True) + EPS) * g_ref[...]
    h = y * (1.0 + sc_ref[...]) + sh_ref[...]
    hh = h.astype(BF16)
    hl = (h - hh.astype(F32)).astype(BF16)
    h3 = jnp.concatenate([hh, hh, hl], axis=1)
    cos = cos_ref[...]
    sin = sin_ref[...]
    cos_t = cost_ref[...]
    sin_t = sint_ref[...]

    p = jnp.dot(h3, wp_ref[...], preferred_element_type=F32)
    q = p[:, 0:W_NQ]
    q_ref[...] = q
    qrot_ref[...] = _rope_tf(q, cos, sin)
    iq_ref[...] = _rope_tf(p[:, W_NQ:W_NQ + W_IQ], cos, sin)
    small_ref[...] = p[:, W_NQ + W_IQ:W_NQ + W_IQ + LANES]
    ck_ref[...] = p[:, W_NQ + W_IQ + LANES:W_NQ + W_IQ + 2 * LANES]

    a = jnp.dot(hh, wa_ref[...], preferred_element_type=F32)
    cv_ref[...] = a[:, 0:LANES]
    o = LANES
    snz_ref[...] = _silu(a[:, o:o + W_NQ])
    o += W_NQ
    sdz_ref[...] = _silu(a[:, o:o + W_DQ])
    o += W_DQ
    dq_ref[...] = _rope_tf(a[:, o:o + W_DQ], cos, sin)
    o += W_DQ
    mg_ref[...] = _sigmoid(a[:, o:])

    nt = (((1,), (1,)), ((), ()))
    t = lax.dot_general(wt_ref[...], hh, nt, preferred_element_type=F32)
    kv = 2 * HEAD_DIM
    nsat_ref[0:2 * kv, :] = t[0:2 * kv]
    nsat_ref[2 * kv:3 * kv, :] = _rope_ft(t[2 * kv:3 * kv], cos_t, sin_t)
    nsat_ref[3 * kv:4 * kv, :] = t[3 * kv:4 * kv]
    wint_ref[0:kv, :] = _rope_ft(t[4 * kv:5 * kv], cos_t, sin_t)
    wint_ref[kv:2 * kv, :] = t[5 * kv:6 * kv]
    dsat_ref[0:kv, :] = _rope_ft(t[6 * kv:7 * kv], cos_t, sin_t)
    dsat_ref[kv:2 * kv, :] = t[7 * kv:8 * kv]
    ti = lax.dot_general(wti_ref[...], h3, nt, preferred_element_type=F32)
    ti = _rope_ft(ti, cos_t, sin_t)
    idxt_ref[...] = ti
    ti_hi = ti.astype(BF16)
    ti_lo = (ti - ti_hi.astype(F32)).astype(BF16)
    idx3_ref[...] = jnp.concatenate([ti_hi, ti_lo, ti_hi], axis=0)


def _prep_inproj_weights(w_in, d_model):
    c = _cuts(d_model)
    sl = lambda name: w_in[:, c[name][0]:c[name][1]]
    nkv = sl('nkv')
    kv = NSA_KV_HEADS * HEAD_DIM
    small = jnp.concatenate([sl('iw'), sl('ng'), jnp.zeros((d_model, LANES - IDX_HEADS - W_NG), F32)], axis=1)
    wp = jnp.concatenate([sl('nq'), sl('iq'), small, nkv[:, 0:kv]], axis=1)
    wa = jnp.concatenate([nkv[:, kv:2 * kv], sl('nz'), sl('dz'), sl('dq'), sl('mg')], axis=1)
    wt = jnp.concatenate([nkv, sl('dkv')], axis=1).T
    wti = sl('ik')
    return _split3_rows(wp), wa.astype(BF16), wt.astype(BF16), _split3_rows(wti).T


def _rope_tables(pos):
    half = HEAD_DIM // 2
    inv = ROPE_THETA ** (-jnp.arange(half, dtype=F32) / half)
    ang = pos.astype(F32)[:, None] * inv[None, :]
    cos = jnp.cos(ang)
    sin = jnp.sin(ang)
    cos_tf = jnp.tile(cos, (1, LANES // half))
    sin_tf = jnp.tile(jnp.concatenate([-sin, sin], axis=1), (1, LANES // HEAD_DIM))
    return cos_tf, sin_tf, cos.T, sin.T


def _in_projection(x2d, norm_g, scale, shift, rows_per_mod, pos, weights, tm):
    r, d = x2d.shape
    wp, wa, wt, wti = weights
    n_tiles = r // tm
    cos_tf, sin_tf, cos_t, sin_t = _rope_tables(pos)
    p_tiles = pos.shape[0] // tm
    if rows_per_mod == 1:
        mod_spec = pl.BlockSpec((tm, d), lambda i: (i, 0))
        sc, sh = scale, shift
    else:
        per = rows_per_mod // tm
        mod_spec = pl.BlockSpec((None, 1, d), lambda i: (i // per, 0, 0))
        sc, sh = scale[:, None, :], shift[:, None, :]
    const = lambda shape: pl.BlockSpec(shape, lambda i: (0,) * len(shape), pipeline_mode=pl.Buffered(1))
    tf = lambda w: pl.BlockSpec((tm, w), lambda i: (i, 0))
    ft = lambda w: pl.BlockSpec((w, tm), lambda i: (0, i))
    tf_widths = (W_NQ, W_NQ, W_IQ, LANES, LANES, LANES, W_NQ, W_DQ, W_DQ, 2 * d)
    ft_widths = (4 * LANES, 2 * LANES, 2 * LANES, IDX_DIM)
    outs = pl.pallas_call(
        _inproj_kernel,
        grid=(n_tiles,),
        in_specs=[pl.BlockSpec((tm, d), lambda i: (i, 0)),
                  const((1, d)), mod_spec, mod_spec,
                  pl.BlockSpec((tm, LANES), lambda i: (i % p_tiles, 0)),
                  pl.BlockSpec((tm, LANES), lambda i: (i % p_tiles, 0)),
                  pl.BlockSpec((HEAD_DIM // 2, tm), lambda i: (0, i % p_tiles)),
                  pl.BlockSpec((HEAD_DIM // 2, tm), lambda i: (0, i % p_tiles)),
                  const(wp.shape), const(wa.shape), const(wt.shape), const(wti.shape)],
        out_specs=[tf(w) for w in tf_widths] + [ft(w) for w in ft_widths] + [ft(3 * IDX_DIM)],
        out_shape=[jax.ShapeDtypeStruct((r, w), F32) for w in tf_widths]
        + [jax.ShapeDtypeStruct((w, r), F32) for w in ft_widths]
        + [jax.ShapeDtypeStruct((3 * IDX_DIM, r), BF16)],
        compiler_params=pltpu.CompilerParams(dimension_semantics=('arbitrary',), vmem_limit_bytes=VMEM_LIMIT),
        name='in_projection',
    )(x2d, norm_g.reshape(1, d), sc, sh, cos_tf, sin_tf, cos_t, sin_t, wp, wa, wt, wti)
    names = ('nsa_q', 'nsa_q_rot', 'idx_q', 'small', 'cmp_k', 'cmp_v', 'silu_nz', 'silu_dz', 'dsa_q', 'merge',
             'nsa_t', 'win_t', 'dsa_t', 'idx_t', 'idx3_t')
    return dict(zip(names, outs))


def _mixer_out_kernel(x_ref, on_ref, od_ref, snz_ref, sdz_ref, mg_ref, ga_ref, fg_ref,
                      wun_ref, wud_ref, wo_ref, y_ref):
    d = x_ref.shape[1]
    u_n = jnp.dot((on_ref[...] * snz_ref[...]).astype(BF16), wun_ref[...], preferred_element_type=F32)
    u_d = jnp.dot((od_ref[...] * sdz_ref[...]).astype(BF16), wud_ref[...], preferred_element_type=F32)
    mg = mg_ref[...]
    mix = mg[:, 0:d] * u_n + mg[:, d:2 * d] * u_d
    z = x_ref[...] + ga_ref[...] * jnp.dot(mix.astype(BF16), wo_ref[...], preferred_element_type=F32)
    y_ref[...] = z * lax.rsqrt(jnp.mean(z * z, axis=-1, keepdims=True) + EPS) * fg_ref[...]


def _mixer_out(x2d, o_n, o_d, f, gate, rows_per_mod, final_g, w_up_nsa, w_up_dsa, w_out, tm):
    r, d = x2d.shape
    if rows_per_mod == 1:
        mod_spec = pl.BlockSpec((tm, d), lambda i: (i, 0))
        ga = gate
    else:
        per = rows_per_mod // tm
        mod_spec = pl.BlockSpec((None, 1, d), lambda i: (i // per, 0, 0))
        ga = gate[:, None, :]
    const = lambda shape: pl.BlockSpec(shape, lambda i: (0,) * len(shape), pipeline_mode=pl.Buffered(1))
    tf = lambda w: pl.BlockSpec((tm, w), lambda i: (i, 0))
    return pl.pallas_call(
        _mixer_out_kernel,
        grid=(r // tm,),
        in_specs=[tf(d), tf(W_NQ), tf(W_DQ), tf(W_NQ), tf(W_DQ), tf(2 * d), mod_spec, const((1, d)),
                  const(w_up_nsa.shape), const(w_up_dsa.shape), const(w_out.shape)],
        out_specs=tf(d),
        out_shape=jax.ShapeDtypeStruct((r, d), F32),
        compiler_params=pltpu.CompilerParams(dimension_semantics=('arbitrary',), vmem_limit_bytes=VMEM_LIMIT),
        name='mixer_out',
    )(x2d, o_n, o_d, f['silu_nz'], f['silu_dz'], f['merge'], ga, final_g.reshape(1, d),
      w_up_nsa.astype(BF16), w_up_dsa.astype(BF16), w_out.astype(BF16))


POS_INF = float('inf')


def _count_ge(s_scr, x, nk, kc):
    rows = s_scr.shape[0]
    xb = jnp.broadcast_to(x, (rows, LANES))

    def body(c, acc):
        k0 = pl.multiple_of(c * kc, kc)
        for j in range(kc // LANES):
            sj = s_scr[:, pl.ds(k0 + j * LANES, LANES)]
            acc = acc + jnp.where(sj >= xb, 1.0, 0.0)
        return acc

    acc = lax.fori_loop(0, nk, body, jnp.zeros((rows, LANES), F32))
    return jnp.sum(acc, axis=1, keepdims=True)


_MAG_BITS = 0x7FFFFFFF


def _ordered_key(f):
    b = lax.bitcast_convert_type(f, jnp.int32)
    return jnp.where(b < 0, b ^ _MAG_BITS, b)


def _from_ordered_key(key):
    return lax.bitcast_convert_type(jnp.where(key < 0, key ^ _MAG_BITS, key), F32)


def _topk_threshold(s_scr, row_min, row_max, n_valid, k, nk, kc):
    kf = float(k)

    def cond(st):
        _, _, _, _, done, it = st
        return jnp.logical_and(it < 40, jnp.min(done) < 0.5)

    def body(st):
        lo, hi, cnt_lo, x, _, it = st
        c = _count_ge(s_scr, _from_ordered_key(x), nk, kc)
        up = c >= kf
        lo = jnp.where(up, x, lo)
        cnt_lo = jnp.where(up, c, cnt_lo)
        hi = jnp.where(up, hi, x)
        mid = (lo >> 1) + (hi >> 1) + (lo & hi & 1)
        done = jnp.where(jnp.logical_or(cnt_lo <= kf, mid == lo), 1.0, 0.0)
        return lo, hi, cnt_lo, mid, done, it + 1

    done0 = jnp.where(n_valid <= kf, 1.0, 0.0)
    hi0 = _ordered_key(row_max)
    st = (_ordered_key(row_min), hi0, n_valid, hi0, done0, jnp.int32(0))
    lo, _, cnt_lo, _, _, _ = lax.while_loop(cond, body, st)
    return _from_ordered_key(lo), cnt_lo


def _write_topk_cap(s_scr, t, cnt_ge, k, nk, kc):
    rows = s_scr.shape[0]
    kf = float(k)
    tb = jnp.broadcast_to(t, (rows, LANES))
    has_tie = jnp.max(cnt_ge) > kf

    @pl.when(jnp.logical_not(has_tie))
    def _():
        def body(c, _):
            k0 = pl.multiple_of(c * kc, kc)
            for j in range(kc // LANES):
                sl = pl.ds(k0 + j * LANES, LANES)
                s_scr[:, sl] = jnp.where(s_scr[:, sl] >= tb, POS_INF, NEG)
            return 0
        lax.fori_loop(0, nk, body, 0)

    @pl.when(has_tie)
    def _():
        def cnt_body(c, acc):
            k0 = pl.multiple_of(c * kc, kc)
            for j in range(kc // LANES):
                acc = acc + jnp.where(s_scr[:, pl.ds(k0 + j * LANES, LANES)] > tb, 1.0, 0.0)
            return acc
        cnt_gt = jnp.sum(lax.fori_loop(0, nk, cnt_body, jnp.zeros((rows, LANES), F32)), axis=1, keepdims=True)
        need = kf - cnt_gt
        r_i = lax.broadcasted_iota(jnp.int32, (LANES, LANES), 0)
        c_i = lax.broadcasted_iota(jnp.int32, (LANES, LANES), 1)
        upper = jnp.where(r_i < c_i, 1.0, 0.0).astype(BF16)

        def body(c, run):
            k0 = pl.multiple_of(c * kc, kc)
            for j in range(kc // LANES):
                sl = pl.ds(k0 + j * LANES, LANES)
                sj = s_scr[:, sl]
                eq = jnp.where(sj == tb, 1.0, 0.0)
                before = run + jnp.dot(eq.astype(BF16), upper, preferred_element_type=F32)
                keep = jnp.logical_or(sj > tb, jnp.logical_and(sj == tb, before < need))
                s_scr[:, sl] = jnp.where(keep, POS_INF, NEG)
                run = run + jnp.sum(eq, axis=1, keepdims=True)
            return run
        lax.fori_loop(0, nk, body, jnp.zeros((rows, 1), F32))


def _flash_group(q_rows, kt_ref, krow0, vt_ref, vrow0, cap_fn, m_scr, l_scr, acc_scr, c0, c1, kc, n_rep):
    m_scr[...] = jnp.full(m_scr.shape, NEG, F32)
    l_scr[...] = jnp.zeros(l_scr.shape, F32)
    acc_scr[...] = jnp.zeros(acc_scr.shape, F32)
    nt = (((1,), (1,)), ((), ()))

    def body(c, _):
        k0 = pl.multiple_of(c * kc, kc)
        kt = kt_ref[pl.ds(krow0, HEAD_DIM), pl.ds(k0, kc)].astype(BF16)
        vt = vt_ref[pl.ds(vrow0, HEAD_DIM), pl.ds(k0, kc)].astype(BF16)
        s = jnp.dot(q_rows, kt, preferred_element_type=F32)
        cap = cap_fn(k0)
        s = jnp.minimum(s, jnp.concatenate([cap] * n_rep, axis=0))
        m_old = m_scr[...]
        m_new = jnp.maximum(m_old, jnp.max(s, axis=1, keepdims=True))
        p = jnp.exp(s - m_new)
        alpha = jnp.exp(m_old - m_new)
        l_scr[...] = alpha * l_scr[...] + jnp.sum(p, axis=1, keepdims=True)
        acc_scr[...] = alpha * acc_scr[...] + lax.dot_general(p.astype(BF16), vt, nt, preferred_element_type=F32)
        m_scr[...] = m_new
        return 0

    lax.fori_loop(c0, c1, body, 0)
    return acc_scr[...] / jnp.maximum(l_scr[...], 1e-30)


def _index_scores(iq, w, idx3_ref, s_scr, qpos, nk, kc, n_keys):
    rows = iq.shape[0]
    q3 = [_split3_cols(iq[:, h * IDX_DIM:(h + 1) * IDX_DIM]) for h in range(IDX_HEADS)]
    wh = [w[:, h:h + 1] for h in range(IDX_HEADS)]
    out_scale = (IDX_DIM ** -0.5) * (IDX_HEADS ** -0.5)

    def body(c, carry):
        mn, mx = carry
        k0 = pl.multiple_of(c * kc, kc)
        k3 = idx3_ref[:, pl.ds(k0, kc)]
        sc = jnp.zeros((rows, kc), F32)
        for h in range(IDX_HEADS):
            d = jnp.dot(q3[h], k3, preferred_element_type=F32)
            sc = sc + jnp.maximum(d, 0.0) * wh[h]
        sc = sc * out_scale
        kpos = k0 + lax.broadcasted_iota(jnp.int32, (1, kc), 1)
        ok = jnp.logical_and(kpos <= qpos, kpos < n_keys)
        s_scr[:, pl.ds(k0, kc)] = jnp.where(ok, sc, NEG)
        mn = jnp.minimum(mn, jnp.min(jnp.where(ok, sc, POS_INF), axis=1, keepdims=True))
        mx = jnp.maximum(mx, jnp.max(jnp.where(ok, sc, NEG), axis=1, keepdims=True))
        return mn, mx

    return lax.fori_loop(0, nk, body, (jnp.full((rows, 1), POS_INF, F32), jnp.full((rows, 1), NEG, F32)))


def _dsa_prompt_kernel(iq_ref, small_ref, dq_ref, idx3_ref, kvt_ref, o_ref,
                       s_scr, m_scr, l_scr, acc_scr, *, tq, kc, topk):
    qs = pl.program_id(1) * tq
    nk = (qs + tq + kc - 1) // kc
    t_len = s_scr.shape[1]
    qpos = qs + lax.broadcasted_iota(jnp.int32, (tq, 1), 0)
    mn, mx = _index_scores(iq_ref[...], small_ref[...], idx3_ref, s_scr, qpos, nk, kc, t_len)
    n_valid = (qpos + 1).astype(F32)
    t, cnt = _topk_threshold(s_scr, mn, mx, n_valid, topk, nk, kc)
    _write_topk_cap(s_scr, t, cnt, topk, nk, kc)
    hg = DSA_HEADS // DSA_KV_HEADS
    scale = HEAD_DIM ** -0.5
    dq = dq_ref[...]
    for g in range(DSA_KV_HEADS):
        q_rows = jnp.concatenate(
            [dq[:, (g * hg + j) * HEAD_DIM:(g * hg + j + 1) * HEAD_DIM] for j in range(hg)], axis=0)
        o = _flash_group((q_rows * scale).astype(BF16), kvt_ref, g * HEAD_DIM,
                         kvt_ref, (DSA_KV_HEADS + g) * HEAD_DIM, lambda k0: s_scr[:, pl.ds(k0, kc)],
                         m_scr, l_scr, acc_scr, 0, nk, kc, hg)
        for j in range(hg):
            h = g * hg + j
            o_ref[:, h * HEAD_DIM:(h + 1) * HEAD_DIM] = o[j * tq:(j + 1) * tq]


def _dsa_prompt(f, bx, t_len, tq, kc):
    topk = min(DSA_TOPK_MAX, t_len // 4)
    nq = t_len // tq
    hg = DSA_HEADS // DSA_KV_HEADS
    row = lambda w: pl.BlockSpec((tq, w), lambda b, i: (b * nq + i, 0))
    per_b = lambda rows: pl.BlockSpec((rows, t_len), lambda b, i: (0, b))
    return pl.pallas_call(
        functools.partial(_dsa_prompt_kernel, tq=tq, kc=kc, topk=topk),
        grid=(bx, nq),
        in_specs=[row(W_IQ), row(LANES), row(W_DQ), per_b(3 * IDX_DIM), per_b(W_DKV)],
        out_specs=row(W_DQ),
        out_shape=jax.ShapeDtypeStruct((bx * t_len, W_DQ), F32),
        scratch_shapes=[pltpu.VMEM((tq, t_len), F32),
                        pltpu.VMEM((hg * tq, 1), F32), pltpu.VMEM((hg * tq, 1), F32),
                        pltpu.VMEM((hg * tq, HEAD_DIM), F32)],
        compiler_params=pltpu.CompilerParams(dimension_semantics=('arbitrary', 'arbitrary'),
                                             vmem_limit_bytes=VMEM_LIMIT),
        name='dsa_prompt',
    )(f['idx_q'], f['small'], f['dsa_q'], f['idx3_t'], f['dsa_t'])


GROUP_TOKENS = CMP_STRIDE
N_HALVES = CMP_BLOCK // CMP_STRIDE
CMP_HID = 2 * HEAD_DIM


def _prep_compress_weights(cmp_pe, cmp_w1, cmp_w2):
    g_n = NSA_KV_HEADS
    eye = jnp.eye(g_n, dtype=F32)
    out = []
    for s in range(2):
        w1 = cmp_w1[s].reshape(N_HALVES, GROUP_TOKENS, HEAD_DIM, CMP_HID)
        w1e = jnp.einsum('ardj,gh->arghdj', w1, eye)
        w1e = w1e.transpose(0, 1, 2, 4, 3, 5).reshape(N_HALVES, GROUP_TOKENS * g_n * HEAD_DIM, g_n * CMP_HID)
        pe = cmp_pe[s].reshape(N_HALVES, GROUP_TOKENS, 1, HEAD_DIM)
        pe = jnp.broadcast_to(pe, (N_HALVES, GROUP_TOKENS, g_n, HEAD_DIM)).reshape(N_HALVES, 1, -1)
        w2e = jnp.einsum('jd,gh->gjhd', cmp_w2[s], eye).reshape(g_n * CMP_HID, g_n * HEAD_DIM)
        out.append((w1e, pe, w2e))
    (w1k, pek, w2k), (w1v, pev, w2v) = out
    w1k3 = jnp.stack([_split3_rows(w1k[a]) for a in range(N_HALVES)])
    return (w1k3, pek, _split3_rows(w2k), w1v.astype(BF16), pev, w2v.astype(BF16))


def _compress_slot(x_ref, m, pe_ref, w1_ref, w2_ref, shift_scr, precise):
    xr = jnp.concatenate([x_ref[pl.ds(r, m, stride=GROUP_TOKENS), :] for r in range(GROUP_TOKENS)], axis=1)
    halves = []
    for a in range(N_HALVES):
        xa = xr + pe_ref[a]
        lhs = _split3_cols(xa) if precise else xa.astype(BF16)
        halves.append(jnp.dot(lhs, w1_ref[a], preferred_element_type=F32))
    shift_scr[0:m, :] = halves[1]
    shift_scr[m:m + 8, :] = jnp.zeros((8, shift_scr.shape[1]), F32)
    hid = _silu(halves[0] + shift_scr[pl.ds(1, m), :])
    lhs = _split3_cols(hid) if precise else hid.astype(BF16)
    return jnp.dot(lhs, w2_ref[...], preferred_element_type=F32)


def _compress_kernel(xk_ref, xv_ref, w1k_ref, pek_ref, w2k_ref, w1v_ref, pev_ref, w2v_ref,
                     ck_ref, cv_ref, shift_scr, *, m):
    n_pad = ck_ref.shape[0]
    if n_pad > m:
        ck_ref[m:n_pad, :] = jnp.zeros((n_pad - m, ck_ref.shape[1]), F32)
        cv_ref[m:n_pad, :] = jnp.zeros((n_pad - m, cv_ref.shape[1]), F32)
    ck_ref[0:m, :] = _compress_slot(xk_ref, m, pek_ref, w1k_ref, w2k_ref, shift_scr, True)
    cv_ref[0:m, :] = _compress_slot(xv_ref, m, pev_ref, w1v_ref, w2v_ref, shift_scr, False)


def _compress_prompt(cmp_k, cmp_v, cw, bx, t_len):
    m = t_len // GROUP_TOKENS
    n_pad = -(-m // LANES) * LANES
    kvw = NSA_KV_HEADS * HEAD_DIM
    const = lambda a: pl.BlockSpec(a.shape, lambda b: (0,) * a.ndim, pipeline_mode=pl.Buffered(1))
    x_spec = pl.BlockSpec((t_len, kvw), lambda b: (b, 0))
    o_spec = pl.BlockSpec((None, n_pad, kvw), lambda b: (b, 0, 0))
    return pl.pallas_call(
        functools.partial(_compress_kernel, m=m),
        grid=(bx,),
        in_specs=[x_spec, x_spec] + [const(a) for a in cw],
        out_specs=[o_spec, o_spec],
        out_shape=[jax.ShapeDtypeStruct((bx, n_pad, kvw), F32)] * 2,
        scratch_shapes=[pltpu.VMEM((m + 8, NSA_KV_HEADS * CMP_HID), F32)],
        compiler_params=pltpu.CompilerParams(dimension_semantics=('arbitrary',), vmem_limit_bytes=VMEM_LIMIT),
        name='compress_prompt',
    )(cmp_k, cmp_v, *cw)


SEL_SHIFT = 6


def _stack_heads(x, g, hg, scale):
    return jnp.concatenate(
        [x[:, (g * hg + j) * HEAD_DIM:(g * hg + j + 1) * HEAD_DIM] for j in range(hg)], axis=0) * scale


def _cmp_branch(q_rows, ck_g, cv_g, qpos_rep, hg, tq):
    n_pad = ck_g.shape[0]
    nt = (((1,), (1,)), ((), ()))
    ck_hi = ck_g.astype(BF16)
    ck_lo = (ck_g - ck_hi.astype(F32)).astype(BF16)
    ck3 = jnp.concatenate([ck_hi, ck_lo, ck_hi], axis=1)
    s = lax.dot_general(_split3_cols(q_rows), ck3, nt, preferred_element_type=F32)
    cmp_end = lax.broadcasted_iota(jnp.int32, (1, n_pad), 1) * CMP_STRIDE + (CMP_BLOCK - 1)
    vis = cmp_end <= qpos_rep
    s = jnp.where(vis, s, NEG)
    p = jnp.where(vis, jnp.exp(s - jnp.max(s, axis=1, keepdims=True)), 0.0)
    p = p / jnp.maximum(jnp.sum(p, axis=1, keepdims=True), 1e-30)
    o_cmp = jnp.dot(p.astype(BF16), cv_g.astype(BF16), preferred_element_type=F32)
    p_sum = p[0:tq]
    for j in range(1, hg):
        p_sum = p_sum + p[j * tq:(j + 1) * tq]
    return o_cmp, p_sum


def _select_blocks(p_sum, qpos, n_slc, n_sel):
    tq, n_pad = p_sum.shape
    n_i = lax.broadcasted_iota(jnp.int32, (n_pad, LANES), 0) * CMP_STRIDE
    j_i = lax.broadcasted_iota(jnp.int32, (n_pad, LANES), 1)
    overlap = jnp.logical_and(n_i < (j_i + 1) * SEL_BLOCK, n_i + CMP_BLOCK > j_i * SEL_BLOCK)
    overlap = jnp.where(jnp.logical_and(overlap, j_i < n_slc), 1.0, 0.0).astype(BF16)
    p_hi = p_sum.astype(BF16)
    r1 = p_sum - p_hi.astype(F32)
    p_mid = r1.astype(BF16)
    p_lo = (r1 - p_mid.astype(F32)).astype(BF16)
    imp = (jnp.dot(p_hi, overlap, preferred_element_type=F32)
           + jnp.dot(p_mid, overlap, preferred_element_type=F32)
           + jnp.dot(p_lo, overlap, preferred_element_type=F32))
    jb = lax.broadcasted_iota(jnp.int32, (1, LANES), 1)
    cur = qpos >> SEL_SHIFT
    forced = jnp.logical_or(jb == 0, jnp.logical_or(jb == cur, jb == cur - 1))
    imp = jnp.where(forced, FORCED_SCORE, imp)
    imp = jnp.where(jb > cur, NEG, imp)
    rank = jnp.zeros((tq, LANES), F32)
    for i in range(n_slc):
        vi = imp[:, i:i + 1]
        ahead = jnp.logical_or(vi > imp, jnp.logical_and(vi == imp, jb > i))
        rank = rank + jnp.where(ahead, 1.0, 0.0)
    return jnp.where(jnp.logical_and(rank < float(n_sel), imp > 0.5 * NEG), 1.0, 0.0)


def _nsa_prompt_kernel(q_ref, qr_ref, small_ref, ck_ref, cv_ref, slc_ref, win_ref, o_ref,
                       cap_scr, m_scr, l_scr, acc_scr, *, tq, kc, kcw, n_slc, n_sel):
    qs = pl.program_id(1) * tq
    nk = (qs + tq + kc - 1) // kc
    hg = NSA_HEADS // NSA_KV_HEADS
    kvw = NSA_KV_HEADS * HEAD_DIM
    scale = HEAD_DIM ** -0.5
    qpos = qs + lax.broadcasted_iota(jnp.int32, (tq, 1), 0)
    qpos_rep = jnp.concatenate([qpos] * hg, axis=0)
    q = q_ref[...]
    qr = qr_ref[...]
    gates = _sigmoid(small_ref[...])
    ck = ck_ref[...]
    cv = cv_ref[...]
    blk_i = lax.broadcasted_iota(jnp.int32, (LANES, 1), 0)

    def win_cap(k0):
        dist = qpos - (k0 + lax.broadcasted_iota(jnp.int32, (1, kcw), 1))
        return jnp.where(jnp.logical_and(dist >= 0, dist < WINDOW), POS_INF, NEG)

    cw0 = jnp.maximum(qs - WINDOW, 0) // kcw
    cw1 = (qs + tq + kcw - 1) // kcw

    for g in range(NSA_KV_HEADS):
        o_cmp, p_sum = _cmp_branch(_stack_heads(q, g, hg, scale), ck[:, g * HEAD_DIM:(g + 1) * HEAD_DIM],
                                   cv[:, g * HEAD_DIM:(g + 1) * HEAD_DIM], qpos_rep, hg, tq)
        sel = _select_blocks(p_sum, qpos, n_slc, n_sel).astype(BF16)

        def cap_body(c, _):
            k0 = pl.multiple_of(c * kc, kc)
            kpos = k0 + lax.broadcasted_iota(jnp.int32, (1, kc), 1)
            expand = jnp.where((kpos >> SEL_SHIFT) == blk_i, 1.0, 0.0).astype(BF16)
            sel_k = jnp.dot(sel, expand, preferred_element_type=F32)
            ok = jnp.logical_and(sel_k > 0.5, kpos <= qpos)
            cap_scr[:, pl.ds(k0, kc)] = jnp.where(ok, POS_INF, NEG)
            return 0

        lax.fori_loop(0, nk, cap_body, 0)
        qr_rows = _stack_heads(qr, g, hg, scale).astype(BF16)
        o_slc = _flash_group(qr_rows, slc_ref, g * HEAD_DIM, slc_ref, kvw + g * HEAD_DIM,
                             lambda k0: cap_scr[:, pl.ds(k0, kc)], m_scr, l_scr, acc_scr, 0, nk, kc, hg)
        o_win = _flash_group(qr_rows, win_ref, g * HEAD_DIM, win_ref, kvw + g * HEAD_DIM,
                             win_cap, m_scr, l_scr, acc_scr, cw0, cw1, kcw, hg)
        for j in range(hg):
            h = g * hg + j
            lane = IDX_HEADS + 3 * h
            rows = slice(j * tq, (j + 1) * tq)
            o_ref[:, h * HEAD_DIM:(h + 1) * HEAD_DIM] = (gates[:, lane:lane + 1] * o_cmp[rows]
                                                         + gates[:, lane + 1:lane + 2] * o_slc[rows]
                                                         + gates[:, lane + 2:lane + 3] * o_win[rows])


def _nsa_prompt(f, ck, cv, bx, t_len, tq, kc, kcw):
    nq = t_len // tq
    hg = NSA_HEADS // NSA_KV_HEADS
    n_pad = ck.shape[1]
    n_slc = -(-t_len // SEL_BLOCK)
    kvw = NSA_KV_HEADS * HEAD_DIM
    row = lambda w: pl.BlockSpec((tq, w), lambda b, i: (b * nq + i, 0))
    cmp_spec = pl.BlockSpec((None, n_pad, kvw), lambda b, i: (b, 0, 0))
    return pl.pallas_call(
        functools.partial(_nsa_prompt_kernel, tq=tq, kc=kc, kcw=kcw, n_slc=n_slc, n_sel=min(N_SEL, n_slc)),
        grid=(bx, nq),
        in_specs=[row(W_NQ), row(W_NQ), row(LANES), cmp_spec, cmp_spec,
                  pl.BlockSpec((2 * kvw, t_len), lambda b, i: (1, b)),
                  pl.BlockSpec((2 * kvw, t_len), lambda b, i: (0, b))],
        out_specs=row(W_NQ),
        out_shape=jax.ShapeDtypeStruct((bx * t_len, W_NQ), F32),
        scratch_shapes=[pltpu.VMEM((tq, t_len), F32),
                        pltpu.VMEM((hg * tq, 1), F32), pltpu.VMEM((hg * tq, 1), F32),
                        pltpu.VMEM((hg * tq, HEAD_DIM), F32)],
        compiler_params=pltpu.CompilerParams(dimension_semantics=('arbitrary', 'arbitrary'),
                                             vmem_limit_bytes=VMEM_LIMIT),
        name='nsa_prompt',
    )(f['nsa_q'], f['nsa_q_rot'], f['small'], ck, cv, f['nsa_t'], f['win_t'])


def _masked_softmax(s, mask):
    s = jnp.where(mask, s, NEG)
    p = jnp.where(mask, jnp.exp(s - jnp.max(s, axis=-1, keepdims=True)), 0.0)
    return p / jnp.maximum(jnp.sum(p, axis=-1, keepdims=True), 1e-30)


def _nsa_context(nsa_all, cmp_pe, cmp_w1, cmp_w2):
    Bx, L = nsa_all.shape[:2]
    G = NSA_KV_HEADS
    n_cmp = (L - CMP_BLOCK) // CMP_STRIDE + 1
    starts = np.arange(n_cmp, dtype=np.int32) * CMP_STRIDE
    blk_idx = starts[:, None] + np.arange(CMP_BLOCK, dtype=np.int32)[None, :]

    def compress(rows, pe, w1, w2):
        blk = rows[:, blk_idx] + pe[None, None, :, None, :]
        flat = jnp.swapaxes(blk, 2, 3).reshape(Bx, n_cmp, G, CMP_BLOCK * HEAD_DIM)
        hp = jnp.dot(flat, w1, precision=lax.Precision.HIGHEST)
        return jnp.dot(jax.nn.silu(hp), w2, precision=lax.Precision.HIGHEST)

    ck = compress(nsa_all[:, :, 0], cmp_pe[0], cmp_w1[0], cmp_w2[0])
    cv = compress(nsa_all[:, :, 1], cmp_pe[1], cmp_w1[1], cmp_w2[1])
    cmp_end = jnp.asarray(starts + CMP_BLOCK - 1)
    n_slc = -(-L // SEL_BLOCK)
    j = np.arange(n_slc, dtype=np.int32)[None, :]
    overlap = (starts[:, None] < (j + 1) * SEL_BLOCK) & (starts[:, None] + CMP_BLOCK > j * SEL_BLOCK)
    overlap = jnp.asarray(overlap, dtype=F32)
    pad = n_slc * SEL_BLOCK - L
    slc = jnp.pad(nsa_all[:, :, 2:4], ((0, 0), (0, pad), (0, 0), (0, 0), (0, 0)))
    slc = slc.reshape(Bx, n_slc, SEL_BLOCK, 2, G, HEAD_DIM).transpose(3, 0, 4, 1, 2, 5)
    return (ck, cv, cmp_end, overlap, slc[0], slc[1], min(N_SEL, n_slc))


def _nsa_attend(q, q_rot, gate, qpos, ctx, win_k, win_v, win_pos):
    hp = lax.Precision.HIGHEST
    ck, cv, cmp_end, overlap, slc_kb, slc_vb, n_sel = ctx
    Bx, Tq = q.shape[:2]
    G, Hg = NSA_KV_HEADS, NSA_HEADS // NSA_KV_HEADS
    scale = HEAD_DIM ** -0.5
    qg = q.reshape(Bx, Tq, G, Hg, HEAD_DIM)
    qr = q_rot.reshape(Bx, Tq, G, Hg, HEAD_DIM)
    s = jnp.einsum('btghd,bngd->btghn', qg, ck, precision=hp).astype(F32) * scale
    vis = (cmp_end[None, :] <= qpos[:, None])[None, :, None, None, :]
    p_cmp = _masked_softmax(s, vis)
    o_cmp = jnp.einsum('btghn,bngd->btghd', p_cmp, cv)
    imp = jnp.einsum('btgn,nj->btgj', jnp.sum(p_cmp, axis=3), overlap, precision=hp)
    n_slc = slc_kb.shape[2]
    jb = jnp.arange(n_slc, dtype=jnp.int32)[None, :]
    cur = (qpos // SEL_BLOCK)[:, None]
    forced = (jb == 0) | (jb == cur) | (jb == cur - 1)
    imp = jnp.where(forced[None, :, None, :], FORCED_SCORE, imp)
    imp = jnp.where((jb > cur)[None, :, None, :], NEG, imp)
    top_val, top_idx = lax.top_k(imp, n_sel)
    blk_ok = top_val > 0.5 * NEG
    bi = jnp.arange(Bx)[:, None, None, None]
    gi = jnp.arange(G)[None, None, :, None]
    sk = slc_kb[bi, gi, top_idx].reshape(Bx, Tq, G, n_sel * SEL_BLOCK, HEAD_DIM)
    sv = slc_vb[bi, gi, top_idx].reshape(Bx, Tq, G, n_sel * SEL_BLOCK, HEAD_DIM)
    kpos = top_idx[..., None] * SEL_BLOCK + jnp.arange(SEL_BLOCK, dtype=jnp.int32)
    ok = (blk_ok[..., None] & (kpos <= qpos[None, :, None, None, None])).reshape(Bx, Tq, G, 1, n_sel * SEL_BLOCK)
    s = jnp.einsum('btghd,btgmd->btghm', qr, sk).astype(F32) * scale
    o_slc = jnp.einsum('btghm,btgmd->btghd', _masked_softmax(s, ok), sv)
    s = jnp.einsum('btghd,bwgd->btghw', qr, win_k).astype(F32) * scale
    dist = qpos[:, None] - win_pos[None, :]
    wok = ((dist >= 0) & (dist < WINDOW) & (win_pos[None, :] >= 0))[None, :, None, None, :]
    o_win = jnp.einsum('btghw,bwgd->btghd', _masked_softmax(s, wok), win_v)
    g = gate.reshape(Bx, Tq, G, Hg, 3)
    o = g[..., 0:1] * o_cmp + g[..., 1:2] * o_slc + g[..., 2:3] * o_win
    return o.reshape(Bx, Tq, NSA_HEADS * HEAD_DIM)


def _dsa_attend(q, idx_q, idx_w, qpos, k, v, idx_k, topk):
    hp = lax.Precision.HIGHEST
    Bx, Tq = q.shape[:2]
    L = k.shape[1]
    G, Hg = DSA_KV_HEADS, DSA_HEADS // DSA_KV_HEADS
    causal = jnp.arange(L, dtype=jnp.int32)[None, :] <= qpos[:, None]
    dots = jnp.einsum('bthd,bsd->bths', idx_q, idx_k, precision=hp).astype(F32) * (IDX_DIM ** -0.5)
    score = jnp.einsum('bths,bth->bts', jax.nn.relu(dots), idx_w, precision=hp) * (IDX_HEADS ** -0.5)
    score = jnp.where(causal[None], score, NEG)
    _, sel = lax.top_k(score, topk)
    bi = jnp.arange(Bx)[:, None, None]
    sk = k[bi, sel]
    sv = v[bi, sel]
    ok = (sel <= qpos[None, :, None])[:, :, None, None, :]
    qg = q.reshape(Bx, Tq, G, Hg, HEAD_DIM)
    s = jnp.einsum('btghd,btkgd->btghk', qg, sk).astype(F32) * (HEAD_DIM ** -0.5)
    o = jnp.einsum('btghk,btkgd->btghd', _masked_softmax(s, ok), sv)
    return o.reshape(Bx, Tq, DSA_HEADS * HEAD_DIM)


def _rows_from_t(a_t, bx, t, lead):
    return a_t.reshape(lead + (bx, t)).transpose((len(lead), len(lead) + 1) + tuple(range(len(lead))))


def kernel(x_prompt, x_sample, cache_nsa_kv, cache_dsa_kv, cache_dsa_idx_k, state_nsa_win_kv, page_table,
           c_prompt, c_sample, norm_g, w_ada, b_ada, w_in, cmp_pe, cmp_w1, cmp_w2,
           w_up_nsa, w_up_dsa, w_out, final_g):
    depth = norm_g.shape[0]
    assert depth == 1
    B, T, D = x_prompt.shape
    Bd, Td, _ = x_sample.shape
    l = 0
    past = page_table.shape[1] * cache_nsa_kv.shape[2]

    c_all = jnp.concatenate([c_prompt, c_sample], axis=0)
    pad = (-c_all.shape[0]) % 8
    c_all = jnp.pad(c_all, ((0, pad), (0, 0)))
    mod = _ada_mod(c_all, w_ada[l], b_ada[l])
    shift, scale, gate = mod[:, 0:D], mod[:, D:2 * D], mod[:, 2 * D:3 * D]

    weights = _prep_inproj_weights(w_in[l], D)

    tm = 256
    pos_p = jnp.arange(T, dtype=jnp.int32)
    fp = _in_projection(x_prompt.reshape(B * T, D), norm_g[l], scale[:B], shift[:B], T, pos_p, weights, tm)
    nsa_rows = _rows_from_t(fp['nsa_t'], B, T, (4, NSA_KV_HEADS, HEAD_DIM))
    win_rows = _rows_from_t(fp['win_t'], B, T, (2, NSA_KV_HEADS, HEAD_DIM))
    dsa_rows = _rows_from_t(fp['dsa_t'], B, T, (2, DSA_KV_HEADS, HEAD_DIM))
    idx_rows = _rows_from_t(fp['idx_t'], B, T, (IDX_DIM,))
    cw = _prep_compress_weights(cmp_pe[l], cmp_w1[l], cmp_w2[l])
    ck, cv = _compress_prompt(fp['cmp_k'], fp['cmp_v'], cw, B, T)
    tq = 128
    kc = min(512, T)
    o_n = _nsa_prompt(fp, ck, cv, B, T, tq, kc, 128)
    o_d = _dsa_prompt(fp, B, T, tq, kc)
    y_prompt = _mixer_out(x_prompt.reshape(B * T, D), o_n, o_d, fp, gate[:B], T, final_g,
                          w_up_nsa[l], w_up_dsa[l], w_out[l], tm).reshape(B, T, D)
    new_win_prompt = win_rows[:, T - min(WINDOW, T):]

    Rs = Bd * Td
    tms = min(256, Rs)
    pos_s = past + jnp.arange(Td, dtype=jnp.int32)
    pos_rows = jnp.tile(pos_s, tms // Td)
    rep = lambda a: jnp.repeat(a[B:B + Bd], Td, axis=0)
    fs = _in_projection(x_sample.reshape(Rs, D), norm_g[l], rep(scale), rep(shift), 1, pos_rows, weights, tms)
    s_nsa_rows = _rows_from_t(fs['nsa_t'], Bd, Td, (4, NSA_KV_HEADS, HEAD_DIM))
    s_win_rows = _rows_from_t(fs['win_t'], Bd, Td, (2, NSA_KV_HEADS, HEAD_DIM))
    s_dsa_rows = _rows_from_t(fs['dsa_t'], Bd, Td, (2, DSA_KV_HEADS, HEAD_DIM))
    s_idx_rows = _rows_from_t(fs['idx_t'], Bd, Td, (IDX_DIM,))

    def gather_pages(pool):
        g = pool[page_table]
        return g.reshape((page_table.shape[0], -1) + pool.shape[2:])

    nsa_all = jnp.concatenate([gather_pages(cache_nsa_kv[l]), s_nsa_rows], axis=1)
    dsa_all = jnp.concatenate([gather_pages(cache_dsa_kv[l]), s_dsa_rows], axis=1)
    idx_all = jnp.concatenate([gather_pages(cache_dsa_idx_k[l]), s_idx_rows], axis=1)
    win_state = state_nsa_win_kv[l]
    win_all = jnp.concatenate([win_state, s_win_rows], axis=1)
    wb = win_state.shape[1]
    wpos = past - wb + jnp.arange(wb + Td, dtype=jnp.int32)
    sctx = _nsa_context(nsa_all, cmp_pe[l], cmp_w1[l], cmp_w2[l])
    topk_s = min(DSA_TOPK_MAX, (past + Td) // 4)
    so_n = _nsa_attend(fs['nsa_q'].reshape(Bd, Td, NSA_HEADS, HEAD_DIM),
                       fs['nsa_q_rot'].reshape(Bd, Td, NSA_HEADS, HEAD_DIM),
                       _sigmoid(fs['small'][:, IDX_HEADS:IDX_HEADS + W_NG]).reshape(Bd, Td, NSA_HEADS, 3),
                       pos_s, sctx, win_all[:, :, 0], win_all[:, :, 1], wpos)
    so_d = _dsa_attend(fs['dsa_q'].reshape(Bd, Td, DSA_HEADS, HEAD_DIM),
                       fs['idx_q'].reshape(Bd, Td, IDX_HEADS, IDX_DIM),
                       fs['small'][:, 0:IDX_HEADS].reshape(Bd, Td, IDX_HEADS),
                       pos_s, dsa_all[:, :, 0], dsa_all[:, :, 1], idx_all, topk_s)
    y_sample = _mixer_out(x_sample.reshape(Rs, D), so_n.reshape(Rs, W_NQ), so_d.reshape(Rs, W_DQ), fs,
                          rep(gate), 1, final_g, w_up_nsa[l], w_up_dsa[l], w_out[l], tms).reshape(Bd, Td, D)
    new_win_sample = win_all[:, Td:]

    return (y_prompt, y_sample, nsa_rows[None], dsa_rows[None], idx_rows[None], new_win_prompt[None],
            s_nsa_rows[None], s_dsa_rows[None], s_idx_rows[None], new_win_sample[None])
```

```python
import functools
import math

import numpy as np
import jax
import jax.numpy as jnp
from jax import lax
from jax.experimental import pallas as pl
from jax.experimental.pallas import tpu as pltpu

F32 = jnp.float32
BF16 = jnp.bfloat16

HEAD_DIM = 64
NSA_HEADS = 8
NSA_KV_HEADS = 2
DSA_HEADS = 8
DSA_KV_HEADS = 2
N_NSA_KV = 6
CMP_BLOCK = 32
CMP_STRIDE = 16
SEL_BLOCK = 64
N_SEL = 16
WINDOW = 512
IDX_HEADS = 4
IDX_DIM = 64
DSA_TOPK_MAX = 256
ROPE_THETA = 10000.0
Q_BLOCK = 128
EPS = 1e-6
NEG = -1e30
FORCED_SCORE = 1e4

LANES = 128
VMEM_LIMIT = 56 * 1024 * 1024

W_NQ = NSA_HEADS * HEAD_DIM
W_NKV = N_NSA_KV * NSA_KV_HEADS * HEAD_DIM
W_NG = NSA_HEADS * 3
W_DQ = DSA_HEADS * HEAD_DIM
W_DKV = 2 * DSA_KV_HEADS * HEAD_DIM
W_IQ = IDX_HEADS * IDX_DIM


def _cuts(d_model):
    widths = (W_NQ, W_NKV, W_NG, W_NQ, W_DQ, W_DKV, W_IQ, IDX_DIM, IDX_HEADS, W_DQ, 2 * d_model)
    c = np.concatenate([[0], np.cumsum(widths)])
    names = ('nq', 'nkv', 'ng', 'nz', 'dq', 'dkv', 'iq', 'ik', 'iw', 'dz', 'mg')
    return {n: (int(c[i]), int(c[i + 1])) for i, n in enumerate(names)}


def _split3_rows(w):
    wh = w.astype(BF16)
    wl = (w - wh.astype(F32)).astype(BF16)
    return jnp.concatenate([wh, wl, wh], axis=0)


def _split3_cols(x):
    xh = x.astype(BF16)
    xl = (x - xh.astype(F32)).astype(BF16)
    return jnp.concatenate([xh, xh, xl], axis=1)


def _silu(v):
    return v * (1.0 / (1.0 + jnp.exp(-v)))


def _sigmoid(v):
    return 1.0 / (1.0 + jnp.exp(-v))


def _ada_kernel(c_ref, w_ref, b_ref, o_ref):
    c3 = _split3_cols(_silu(c_ref[...]))
    o_ref[...] = jnp.dot(c3, w_ref[...], preferred_element_type=F32) + b_ref[...]


def _ada_mod(c_all, w_ada, b_ada):
    m, d = c_all.shape
    n = w_ada.shape[1]
    tn = 512
    w3 = _split3_rows(w_ada)
    return pl.pallas_call(
        _ada_kernel,
        grid=(n // tn,),
        in_specs=[pl.BlockSpec((m, d), lambda j: (0, 0)),
                  pl.BlockSpec((3 * d, tn), lambda j: (0, j)),
                  pl.BlockSpec((1, tn), lambda j: (0, j))],
        out_specs=pl.BlockSpec((m, tn), lambda j: (0, j)),
        out_shape=jax.ShapeDtypeStruct((m, n), F32),
        name='ada_mod',
    )(c_all, w3, b_ada.reshape(1, n))


def _rope_tf(v, cos, sin_signed):
    first_half = (lax.broadcasted_iota(jnp.int32, (1, LANES), 1) % HEAD_DIM) < (HEAD_DIM // 2)
    outs = []
    for c in range(v.shape[1] // LANES):
        vc = v[:, c * LANES:(c + 1) * LANES]
        sw = jnp.where(first_half, pltpu.roll(vc, LANES - HEAD_DIM // 2, 1), pltpu.roll(vc, HEAD_DIM // 2, 1))
        outs.append(vc * cos + sw * sin_signed)
    return jnp.concatenate(outs, axis=1) if len(outs) > 1 else outs[0]


def _rope_ft(v, cos_t, sin_t):
    half = HEAD_DIM // 2
    outs = []
    for hd in range(v.shape[0] // HEAD_DIM):
        x1 = v[hd * HEAD_DIM:hd * HEAD_DIM + half]
        x2 = v[hd * HEAD_DIM + half:(hd + 1) * HEAD_DIM]
        outs.append(x1 * cos_t - x2 * sin_t)
        outs.append(x2 * cos_t + x1 * sin_t)
    return jnp.concatenate(outs, axis=0)


def _inproj_kernel(x_ref, g_ref, sc_ref, sh_ref, cos_ref, sin_ref, cost_ref, sint_ref,
                   wp_ref, wa_ref, wt_ref, wti_ref,
                   q_ref, qrot_ref, iq_ref, small_ref, ck_ref,
                   cv_ref, snz_ref, sdz_ref, dq_ref, mg_ref,
                   nsat_ref, wint_ref, dsat_ref, idxt_ref, idx3_ref):
    x = x_ref[...]
    y = x * lax.rsqrt(jnp.mean(x * x, axis=-1, keepdims=True) + EPS) * g_ref[...]
    h = y * (1.0 + sc_ref[...]) + sh_ref[...]
    hh = h.astype(BF16)
    hl = (h - hh.astype(F32)).astype(BF16)
    h3 = jnp.concatenate([hh, hh, hl], axis=1)
    cos = cos_ref[...]
    sin = sin_ref[...]
    cos_t = cost_ref[...]
    sin_t = sint_ref[...]

    p = jnp.dot(h3, wp_ref[...], preferred_element_type=F32)
    q = p[:, 0:W_NQ]
    q_ref[...] = q
    qrot_ref[...] = _rope_tf(q, cos, sin)
    iq_ref[...] = _rope_tf(p[:, W_NQ:W_NQ + W_IQ], cos, sin)
    small_ref[...] = p[:, W_NQ + W_IQ:W_NQ + W_IQ + LANES]
    ck_ref[...] = p[:, W_NQ + W_IQ + LANES:W_NQ + W_IQ + 2 * LANES]

    a = jnp.dot(hh, wa_ref[...], preferred_element_type=F32)
    cv_ref[...] = a[:, 0:LANES]
    o = LANES
    snz_ref[...] = _silu(a[:, o:o + W_NQ])
    o += W_NQ
    sdz_ref[...] = _silu(a[:, o:o + W_DQ])
    o += W_DQ
    dq_ref[...] = _rope_tf(a[:, o:o + W_DQ], cos, sin)
    o += W_DQ
    mg_ref[...] = _sigmoid(a[:, o:])

    nt = (((1,), (1,)), ((), ()))
    t = lax.dot_general(wt_ref[...], hh, nt, preferred_element_type=F32)
    kv = 2 * HEAD_DIM
    nsat_ref[0:2 * kv, :] = t[0:2 * kv]
    nsat_ref[2 * kv:3 * kv, :] = _rope_ft(t[2 * kv:3 * kv], cos_t, sin_t)
    nsat_ref[3 * kv:4 * kv, :] = t[3 * kv:4 * kv]
    wint_ref[0:kv, :] = _rope_ft(t[4 * kv:5 * kv], cos_t, sin_t)
    wint_ref[kv:2 * kv, :] = t[5 * kv:6 * kv]
    dsat_ref[0:kv, :] = _rope_ft(t[6 * kv:7 * kv], cos_t, sin_t)
    dsat_ref[kv:2 * kv, :] = t[7 * kv:8 * kv]
    ti = lax.dot_general(wti_ref[...], h3, nt, preferred_element_type=F32)
    ti = _rope_ft(ti, cos_t, sin_t)
    idxt_ref[...] = ti
    ti_hi = ti.astype(BF16)
    ti_lo = (ti - ti_hi.astype(F32)).astype(BF16)
    idx3_ref[...] = jnp.concatenate([ti_hi, ti_lo, ti_hi], axis=0)


def _prep_inproj_weights(w_in, d_model):
    c = _cuts(d_model)
    sl = lambda name: w_in[:, c[name][0]:c[name][1]]
    nkv = sl('nkv')
    kv = NSA_KV_HEADS * HEAD_DIM
    small = jnp.concatenate([sl('iw'), sl('ng'), jnp.zeros((d_model, LANES - IDX_HEADS - W_NG), F32)], axis=1)
    wp = jnp.concatenate([sl('nq'), sl('iq'), small, nkv[:, 0:kv]], axis=1)
    wa = jnp.concatenate([nkv[:, kv:2 * kv], sl('nz'), sl('dz'), sl('dq'), sl('mg')], axis=1)
    wt = jnp.concatenate([nkv, sl('dkv')], axis=1).T
    wti = sl('ik')
    return _split3_rows(wp), wa.astype(BF16), wt.astype(BF16), _split3_rows(wti).T


def _rope_tables(pos):
    half = HEAD_DIM // 2
    inv = ROPE_THETA ** (-jnp.arange(half, dtype=F32) / half)
    ang = pos.astype(F32)[:, None] * inv[None, :]
    cos = jnp.cos(ang)
    sin = jnp.sin(ang)
    cos_tf = jnp.tile(cos, (1, LANES // half))
    sin_tf = jnp.tile(jnp.concatenate([-sin, sin], axis=1), (1, LANES // HEAD_DIM))
    return cos_tf, sin_tf, cos.T, sin.T


def _in_projection(x2d, norm_g, scale, shift, rows_per_mod, pos, weights, tm, nb):
    r, d = x2d.shape
    tpb = (r // nb) // tm
    wp, wa, wt, wti = weights
    n_tiles = r // tm
    cos_tf, sin_tf, cos_t, sin_t = _rope_tables(pos)
    p_tiles = pos.shape[0] // tm
    if rows_per_mod == 1:
        mod_spec = pl.BlockSpec((tm, d), lambda i: (i, 0))
        sc, sh = scale, shift
    else:
        per = rows_per_mod // tm
        mod_spec = pl.BlockSpec((None, 1, d), lambda i: (i // per, 0, 0))
        sc, sh = scale[:, None, :], shift[:, None, :]
    const = lambda shape: pl.BlockSpec(shape, lambda i: (0,) * len(shape), pipeline_mode=pl.Buffered(1))
    tf = lambda w: pl.BlockSpec((tm, w), lambda i: (i, 0))
    ft = lambda w: pl.BlockSpec((None, w, tm), lambda i: (i // tpb, 0, i % tpb))
    tf_widths = (W_NQ, W_NQ, W_IQ, LANES, LANES, LANES, W_NQ, W_DQ, W_DQ, 2 * d)
    ft_widths = (4 * LANES, 2 * LANES, 2 * LANES, IDX_DIM)
    outs = pl.pallas_call(
        _inproj_kernel,
        grid=(n_tiles,),
        in_specs=[pl.BlockSpec((tm, d), lambda i: (i, 0)),
                  const((1, d)), mod_spec, mod_spec,
                  pl.BlockSpec((tm, LANES), lambda i: (i % p_tiles, 0)),
                  pl.BlockSpec((tm, LANES), lambda i: (i % p_tiles, 0)),
                  pl.BlockSpec((HEAD_DIM // 2, tm), lambda i: (0, i % p_tiles)),
                  pl.BlockSpec((HEAD_DIM // 2, tm), lambda i: (0, i % p_tiles)),
                  const(wp.shape), const(wa.shape), const(wt.shape), const(wti.shape)],
        out_specs=[tf(w) for w in tf_widths] + [ft(w) for w in ft_widths] + [ft(3 * IDX_DIM)],
        out_shape=[jax.ShapeDtypeStruct((r, w), F32) for w in tf_widths]
        + [jax.ShapeDtypeStruct((nb, w, r // nb), F32) for w in ft_widths]
        + [jax.ShapeDtypeStruct((nb, 3 * IDX_DIM, r // nb), BF16)],
        compiler_params=pltpu.CompilerParams(dimension_semantics=('arbitrary',), vmem_limit_bytes=VMEM_LIMIT),
        name='in_projection',
    )(x2d, norm_g.reshape(1, d), sc, sh, cos_tf, sin_tf, cos_t, sin_t, wp, wa, wt, wti)
    names = ('nsa_q', 'nsa_q_rot', 'idx_q', 'small', 'cmp_k', 'cmp_v', 'silu_nz', 'silu_dz', 'dsa_q', 'merge',
             'nsa_t', 'win_t', 'dsa_t', 'idx_t', 'idx3_t')
    return dict(zip(names, outs))


def _mixer_out_kernel(x_ref, on_ref, od_ref, snz_ref, sdz_ref, mg_ref, ga_ref, fg_ref,
                      wun_ref, wud_ref, wo_ref, y_ref):
    d = x_ref.shape[1]
    u_n = jnp.dot((on_ref[...] * snz_ref[...]).astype(BF16), wun_ref[...], preferred_element_type=F32)
    u_d = jnp.dot((od_ref[...] * sdz_ref[...]).astype(BF16), wud_ref[...], preferred_element_type=F32)
    mg = mg_ref[...]
    mix = mg[:, 0:d] * u_n + mg[:, d:2 * d] * u_d
    z = x_ref[...] + ga_ref[...] * jnp.dot(mix.astype(BF16), wo_ref[...], preferred_element_type=F32)
    y_ref[...] = z * lax.rsqrt(jnp.mean(z * z, axis=-1, keepdims=True) + EPS) * fg_ref[...]


def _mixer_out(x2d, o_n, o_d, f, gate, rows_per_mod, final_g, w_up_nsa, w_up_dsa, w_out, tm):
    r, d = x2d.shape
    if rows_per_mod == 1:
        mod_spec = pl.BlockSpec((tm, d), lambda i: (i, 0))
        ga = gate
    else:
        per = rows_per_mod // tm
        mod_spec = pl.BlockSpec((None, 1, d), lambda i: (i // per, 0, 0))
        ga = gate[:, None, :]
    const = lambda shape: pl.BlockSpec(shape, lambda i: (0,) * len(shape), pipeline_mode=pl.Buffered(1))
    tf = lambda w: pl.BlockSpec((tm, w), lambda i: (i, 0))
    return pl.pallas_call(
        _mixer_out_kernel,
        grid=(r // tm,),
        in_specs=[tf(d), tf(W_NQ), tf(W_DQ), tf(W_NQ), tf(W_DQ), tf(2 * d), mod_spec, const((1, d)),
                  const(w_up_nsa.shape), const(w_up_dsa.shape), const(w_out.shape)],
        out_specs=tf(d),
        out_shape=jax.ShapeDtypeStruct((r, d), F32),
        compiler_params=pltpu.CompilerParams(dimension_semantics=('arbitrary',), vmem_limit_bytes=VMEM_LIMIT),
        name='mixer_out',
    )(x2d, o_n, o_d, f['silu_nz'], f['silu_dz'], f['merge'], ga, final_g.reshape(1, d),
      w_up_nsa.astype(BF16), w_up_dsa.astype(BF16), w_out.astype(BF16))


POS_INF = float('inf')


def _count_ge(s_scr, x, nk, kc):
    rows = s_scr.shape[0]
    xb = jnp.broadcast_to(x, (rows, LANES))

    def body(c, acc):
        k0 = pl.multiple_of(c * kc, kc)
        for j in range(kc // LANES):
            sj = s_scr[:, pl.ds(k0 + j * LANES, LANES)]
            acc = acc + jnp.where(sj >= xb, 1.0, 0.0)
        return acc

    acc = lax.fori_loop(0, nk, body, jnp.zeros((rows, LANES), F32))
    return jnp.sum(acc, axis=1, keepdims=True)


_MAG_BITS = 0x7FFFFFFF


def _ordered_key(f):
    b = lax.bitcast_convert_type(f, jnp.int32)
    return jnp.where(b < 0, b ^ _MAG_BITS, b)


def _from_ordered_key(key):
    return lax.bitcast_convert_type(jnp.where(key < 0, key ^ _MAG_BITS, key), F32)


def _topk_threshold(s_scr, row_min, row_max, n_valid, k, nk, kc):
    kf = float(k)

    def cond(st):
        _, _, _, _, done, it = st
        return jnp.logical_and(it < 40, jnp.min(done) < 0.5)

    def body(st):
        lo, hi, cnt_lo, x, _, it = st
        c = _count_ge(s_scr, _from_ordered_key(x), nk, kc)
        up = c >= kf
        lo = jnp.where(up, x, lo)
        cnt_lo = jnp.where(up, c, cnt_lo)
        hi = jnp.where(up, hi, x)
        mid = (lo >> 1) + (hi >> 1) + (lo & hi & 1)
        done = jnp.where(jnp.logical_or(cnt_lo <= kf, mid == lo), 1.0, 0.0)
        return lo, hi, cnt_lo, mid, done, it + 1

    done0 = jnp.where(n_valid <= kf, 1.0, 0.0)
    hi0 = _ordered_key(row_max)
    st = (_ordered_key(row_min), hi0, n_valid, hi0, done0, jnp.int32(0))
    lo, _, cnt_lo, _, _, _ = lax.while_loop(cond, body, st)
    return jnp.where(n_valid <= kf, NEG, _from_ordered_key(lo)), cnt_lo


def _write_topk_cap(s_scr, t, cnt_ge, k, nk, kc):
    rows = s_scr.shape[0]
    kf = float(k)
    tb = jnp.broadcast_to(t, (rows, LANES))
    has_tie = jnp.max(cnt_ge) > kf

    @pl.when(jnp.logical_not(has_tie))
    def _():
        def body(c, _):
            k0 = pl.multiple_of(c * kc, kc)
            for j in range(kc // LANES):
                sl = pl.ds(k0 + j * LANES, LANES)
                s_scr[:, sl] = jnp.where(s_scr[:, sl] >= tb, POS_INF, NEG)
            return 0
        lax.fori_loop(0, nk, body, 0)

    @pl.when(has_tie)
    def _():
        def cnt_body(c, acc):
            k0 = pl.multiple_of(c * kc, kc)
            for j in range(kc // LANES):
                acc = acc + jnp.where(s_scr[:, pl.ds(k0 + j * LANES, LANES)] > tb, 1.0, 0.0)
            return acc
        cnt_gt = jnp.sum(lax.fori_loop(0, nk, cnt_body, jnp.zeros((rows, LANES), F32)), axis=1, keepdims=True)
        need = kf - cnt_gt
        r_i = lax.broadcasted_iota(jnp.int32, (LANES, LANES), 0)
        c_i = lax.broadcasted_iota(jnp.int32, (LANES, LANES), 1)
        upper = jnp.where(r_i < c_i, 1.0, 0.0).astype(BF16)

        def body(c, run):
            k0 = pl.multiple_of(c * kc, kc)
            for j in range(kc // LANES):
                sl = pl.ds(k0 + j * LANES, LANES)
                sj = s_scr[:, sl]
                eq = jnp.where(sj == tb, 1.0, 0.0)
                before = run + jnp.dot(eq.astype(BF16), upper, preferred_element_type=F32)
                keep = jnp.logical_or(sj > tb, jnp.logical_and(sj == tb, before < need))
                s_scr[:, sl] = jnp.where(keep, POS_INF, NEG)
                run = run + jnp.sum(eq, axis=1, keepdims=True)
            return run
        lax.fori_loop(0, nk, body, jnp.zeros((rows, 1), F32))


def _flash_group(q_rows, kt_ref, krow0, vt_ref, vrow0, cap_fn, m_scr, l_scr, acc_scr, c0, c1, kc, n_rep):
    m_scr[...] = jnp.full(m_scr.shape, NEG, F32)
    l_scr[...] = jnp.zeros(l_scr.shape, F32)
    acc_scr[...] = jnp.zeros(acc_scr.shape, F32)
    nt = (((1,), (1,)), ((), ()))

    def body(c, _):
        k0 = pl.multiple_of(c * kc, kc)
        kt = kt_ref[pl.ds(krow0, HEAD_DIM), pl.ds(k0, kc)].astype(BF16)
        vt = vt_ref[pl.ds(vrow0, HEAD_DIM), pl.ds(k0, kc)].astype(BF16)
        s = jnp.dot(q_rows, kt, preferred_element_type=F32)
        cap = cap_fn(k0)
        s = jnp.minimum(s, jnp.concatenate([cap] * n_rep, axis=0))
        m_old = m_scr[...]
        m_new = jnp.maximum(m_old, jnp.max(s, axis=1, keepdims=True))
        p = jnp.exp(s - m_new)
        alpha = jnp.exp(m_old - m_new)
        l_scr[...] = alpha * l_scr[...] + jnp.sum(p, axis=1, keepdims=True)
        acc_scr[...] = alpha * acc_scr[...] + lax.dot_general(p.astype(BF16), vt, nt, preferred_element_type=F32)
        m_scr[...] = m_new
        return 0

    lax.fori_loop(c0, c1, body, 0)
    return acc_scr[...] / jnp.maximum(l_scr[...], 1e-30)


def _index_scores(iq, w, idx3_ref, s_scr, qpos, nk, kc, n_keys):
    rows = iq.shape[0]
    q3 = [_split3_cols(iq[:, h * IDX_DIM:(h + 1) * IDX_DIM]) for h in range(IDX_HEADS)]
    wh = [w[:, h:h + 1] for h in range(IDX_HEADS)]
    out_scale = (IDX_DIM ** -0.5) * (IDX_HEADS ** -0.5)

    def body(c, carry):
        mn, mx = carry
        k0 = pl.multiple_of(c * kc, kc)
        k3 = idx3_ref[:, pl.ds(k0, kc)]
        sc = jnp.zeros((rows, kc), F32)
        for h in range(IDX_HEADS):
            d = jnp.dot(q3[h], k3, preferred_element_type=F32)
            sc = sc + jnp.maximum(d, 0.0) * wh[h]
        sc = sc * out_scale
        kpos = k0 + lax.broadcasted_iota(jnp.int32, (1, kc), 1)
        ok = jnp.logical_and(kpos <= qpos, kpos < n_keys)
        s_scr[:, pl.ds(k0, kc)] = jnp.where(ok, sc, -POS_INF)
        mn = jnp.minimum(mn, jnp.min(jnp.where(ok, sc, POS_INF), axis=1, keepdims=True))
        mx = jnp.maximum(mx, jnp.max(jnp.where(ok, sc, NEG), axis=1, keepdims=True))
        return mn, mx

    return lax.fori_loop(0, nk, body, (jnp.full((rows, 1), POS_INF, F32), jnp.full((rows, 1), NEG, F32)))


def _dsa_prompt_kernel(iq_ref, small_ref, dq_ref, idx3_ref, kvt_ref, o_ref,
                       s_scr, m_scr, l_scr, acc_scr, *, tq, kc, topk):
    qs = pl.program_id(1) * tq
    nk = (qs + tq + kc - 1) // kc
    t_len = s_scr.shape[1]
    qpos = qs + lax.broadcasted_iota(jnp.int32, (tq, 1), 0)
    mn, mx = _index_scores(iq_ref[...], small_ref[...], idx3_ref, s_scr, qpos, nk, kc, t_len)
    n_valid = (qpos + 1).astype(F32)
    t, cnt = _topk_threshold(s_scr, mn, mx, n_valid, topk, nk, kc)
    _write_topk_cap(s_scr, t, cnt, topk, nk, kc)
    hg = DSA_HEADS // DSA_KV_HEADS
    scale = HEAD_DIM ** -0.5
    dq = dq_ref[...]
    for g in range(DSA_KV_HEADS):
        q_rows = jnp.concatenate(
            [dq[:, (g * hg + j) * HEAD_DIM:(g * hg + j + 1) * HEAD_DIM] for j in range(hg)], axis=0)
        o = _flash_group((q_rows * scale).astype(BF16), kvt_ref, g * HEAD_DIM,
                         kvt_ref, (DSA_KV_HEADS + g) * HEAD_DIM, lambda k0: s_scr[:, pl.ds(k0, kc)],
                         m_scr, l_scr, acc_scr, 0, nk, kc, hg)
        for j in range(hg):
            h = g * hg + j
            o_ref[:, h * HEAD_DIM:(h + 1) * HEAD_DIM] = o[j * tq:(j + 1) * tq]


def _dsa_prompt(f, bx, t_len, tq, kc):
    topk = min(DSA_TOPK_MAX, t_len // 4)
    nq = t_len // tq
    hg = DSA_HEADS // DSA_KV_HEADS
    row = lambda w: pl.BlockSpec((tq, w), lambda b, i: (b * nq + i, 0))
    per_b = lambda rows: pl.BlockSpec((None, rows, t_len), lambda b, i: (b, 0, 0))
    return pl.pallas_call(
        functools.partial(_dsa_prompt_kernel, tq=tq, kc=kc, topk=topk),
        grid=(bx, nq),
        in_specs=[row(W_IQ), row(LANES), row(W_DQ), per_b(3 * IDX_DIM), per_b(W_DKV)],
        out_specs=row(W_DQ),
        out_shape=jax.ShapeDtypeStruct((bx * t_len, W_DQ), F32),
        scratch_shapes=[pltpu.VMEM((tq, t_len), F32),
                        pltpu.VMEM((hg * tq, 1), F32), pltpu.VMEM((hg * tq, 1), F32),
                        pltpu.VMEM((hg * tq, HEAD_DIM), F32)],
        compiler_params=pltpu.CompilerParams(dimension_semantics=('arbitrary', 'arbitrary'),
                                             vmem_limit_bytes=VMEM_LIMIT),
        name='dsa_prompt',
    )(f['idx_q'], f['small'], f['dsa_q'], f['idx3_t'], f['dsa_t'])


GROUP_TOKENS = CMP_STRIDE
N_HALVES = CMP_BLOCK // CMP_STRIDE
CMP_HID = 2 * HEAD_DIM


def _prep_compress_weights(cmp_pe, cmp_w1, cmp_w2):
    g_n = NSA_KV_HEADS
    eye = jnp.eye(g_n, dtype=F32)
    out = []
    for s in range(2):
        w1 = cmp_w1[s].reshape(N_HALVES, GROUP_TOKENS, HEAD_DIM, CMP_HID)
        w1e = jnp.einsum('ardj,gh->arghdj', w1, eye)
        w1e = w1e.transpose(0, 1, 2, 4, 3, 5).reshape(N_HALVES, GROUP_TOKENS * g_n * HEAD_DIM, g_n * CMP_HID)
        pe = cmp_pe[s].reshape(N_HALVES, GROUP_TOKENS, 1, HEAD_DIM)
        pe = jnp.broadcast_to(pe, (N_HALVES, GROUP_TOKENS, g_n, HEAD_DIM)).reshape(N_HALVES, 1, -1)
        w2e = jnp.einsum('jd,gh->gjhd', cmp_w2[s], eye).reshape(g_n * CMP_HID, g_n * HEAD_DIM)
        out.append((w1e, pe, w2e))
    (w1k, pek, w2k), (w1v, pev, w2v) = out
    w1k3 = jnp.stack([_split3_rows(w1k[a]) for a in range(N_HALVES)])
    return (w1k3, pek, _split3_rows(w2k), w1v.astype(BF16), pev, w2v.astype(BF16))


def _compress_slot(x_ref, m, pe_ref, w1_ref, w2_ref, shift_scr, precise):
    xr = jnp.concatenate([x_ref[pl.ds(r, m, stride=GROUP_TOKENS), :] for r in range(GROUP_TOKENS)], axis=1)
    halves = []
    for a in range(N_HALVES):
        xa = xr + pe_ref[a]
        lhs = _split3_cols(xa) if precise else xa.astype(BF16)
        halves.append(jnp.dot(lhs, w1_ref[a], preferred_element_type=F32))
    shift_scr[0:m, :] = halves[1]
    shift_scr[m:m + 8, :] = jnp.zeros((8, shift_scr.shape[1]), F32)
    hid = _silu(halves[0] + shift_scr[pl.ds(1, m), :])
    lhs = _split3_cols(hid) if precise else hid.astype(BF16)
    return jnp.dot(lhs, w2_ref[...], preferred_element_type=F32)


def _compress_kernel(xk_ref, xv_ref, w1k_ref, pek_ref, w2k_ref, w1v_ref, pev_ref, w2v_ref,
                     ck_ref, cv_ref, shift_scr, *, m):
    n_pad = ck_ref.shape[0]
    if n_pad > m:
        ck_ref[m:n_pad, :] = jnp.zeros((n_pad - m, ck_ref.shape[1]), F32)
        cv_ref[m:n_pad, :] = jnp.zeros((n_pad - m, cv_ref.shape[1]), F32)
    ck_ref[0:m, :] = _compress_slot(xk_ref, m, pek_ref, w1k_ref, w2k_ref, shift_scr, True)
    cv_ref[0:m, :] = _compress_slot(xv_ref, m, pev_ref, w1v_ref, w2v_ref, shift_scr, False)


def _compress_prompt(cmp_k, cmp_v, cw, bx, t_len):
    m = t_len // GROUP_TOKENS
    n_pad = -(-m // LANES) * LANES
    kvw = NSA_KV_HEADS * HEAD_DIM
    const = lambda a: pl.BlockSpec(a.shape, lambda b: (0,) * a.ndim, pipeline_mode=pl.Buffered(1))
    x_spec = pl.BlockSpec((t_len, kvw), lambda b: (b, 0))
    o_spec = pl.BlockSpec((None, n_pad, kvw), lambda b: (b, 0, 0))
    return pl.pallas_call(
        functools.partial(_compress_kernel, m=m),
        grid=(bx,),
        in_specs=[x_spec, x_spec] + [const(a) for a in cw],
        out_specs=[o_spec, o_spec],
        out_shape=[jax.ShapeDtypeStruct((bx, n_pad, kvw), F32)] * 2,
        scratch_shapes=[pltpu.VMEM((m + 8, NSA_KV_HEADS * CMP_HID), F32)],
        compiler_params=pltpu.CompilerParams(dimension_semantics=('arbitrary',), vmem_limit_bytes=VMEM_LIMIT),
        name='compress_prompt',
    )(cmp_k, cmp_v, *cw)


SEL_SHIFT = 6


def _stack_heads(x, g, hg, scale):
    return jnp.concatenate(
        [x[:, (g * hg + j) * HEAD_DIM:(g * hg + j + 1) * HEAD_DIM] for j in range(hg)], axis=0) * scale


def _cmp_branch(q_rows, ck_g, cv_g, qpos_rep, hg, tq):
    n_pad = ck_g.shape[0]
    nt = (((1,), (1,)), ((), ()))
    ck_hi = ck_g.astype(BF16)
    ck_lo = (ck_g - ck_hi.astype(F32)).astype(BF16)
    ck3 = jnp.concatenate([ck_hi, ck_lo, ck_hi], axis=1)
    s = lax.dot_general(_split3_cols(q_rows), ck3, nt, preferred_element_type=F32)
    cmp_end = lax.broadcasted_iota(jnp.int32, (1, n_pad), 1) * CMP_STRIDE + (CMP_BLOCK - 1)
    vis = cmp_end <= qpos_rep
    s = jnp.where(vis, s, NEG)
    p = jnp.where(vis, jnp.exp(s - jnp.max(s, axis=1, keepdims=True)), 0.0)
    p = p / jnp.maximum(jnp.sum(p, axis=1, keepdims=True), 1e-30)
    o_cmp = jnp.dot(p.astype(BF16), cv_g.astype(BF16), preferred_element_type=F32)
    p_sum = p[0:tq]
    for j in range(1, hg):
        p_sum = p_sum + p[j * tq:(j + 1) * tq]
    return o_cmp, p_sum


def _select_blocks(p_sum, qpos, n_slc, n_sel):
    tq, n_pad = p_sum.shape
    n_i = lax.broadcasted_iota(jnp.int32, (n_pad, LANES), 0) * CMP_STRIDE
    j_i = lax.broadcasted_iota(jnp.int32, (n_pad, LANES), 1)
    overlap = jnp.logical_and(n_i < (j_i + 1) * SEL_BLOCK, n_i + CMP_BLOCK > j_i * SEL_BLOCK)
    overlap = jnp.where(jnp.logical_and(overlap, j_i < n_slc), 1.0, 0.0).astype(BF16)
    p_hi = p_sum.astype(BF16)
    r1 = p_sum - p_hi.astype(F32)
    p_mid = r1.astype(BF16)
    p_lo = (r1 - p_mid.astype(F32)).astype(BF16)
    imp = (jnp.dot(p_hi, overlap, preferred_element_type=F32)
           + jnp.dot(p_mid, overlap, preferred_element_type=F32)
           + jnp.dot(p_lo, overlap, preferred_element_type=F32))
    jb = lax.broadcasted_iota(jnp.int32, (1, LANES), 1)
    cur = qpos >> SEL_SHIFT
    forced = jnp.logical_or(jb == 0, jnp.logical_or(jb == cur, jb == cur - 1))
    imp = jnp.where(forced, FORCED_SCORE, imp)
    imp = jnp.where(jb > cur, NEG, imp)
    rank = jnp.zeros((tq, LANES), F32)
    for i in range(n_slc):
        vi = imp[:, i:i + 1]
        ahead = jnp.logical_or(vi > imp, jnp.logical_and(vi == imp, jb > i))
        rank = rank + jnp.where(ahead, 1.0, 0.0)
    return jnp.where(jnp.logical_and(rank < float(n_sel), imp > 0.5 * NEG), 1.0, 0.0)


def _nsa_prompt_kernel(q_ref, qr_ref, small_ref, ck_ref, cv_ref, slc_ref, win_ref, o_ref,
                       cap_scr, m_scr, l_scr, acc_scr, *, tq, kc, kcw, n_slc, n_sel):
    qs = pl.program_id(1) * tq
    nk = (qs + tq + kc - 1) // kc
    hg = NSA_HEADS // NSA_KV_HEADS
    kvw = NSA_KV_HEADS * HEAD_DIM
    scale = HEAD_DIM ** -0.5
    qpos = qs + lax.broadcasted_iota(jnp.int32, (tq, 1), 0)
    qpos_rep = jnp.concatenate([qpos] * hg, axis=0)
    q = q_ref[...]
    qr = qr_ref[...]
    gates = _sigmoid(small_ref[...])
    ck = ck_ref[...]
    cv = cv_ref[...]
    blk_i = lax.broadcasted_iota(jnp.int32, (LANES, 1), 0)

    def win_cap(k0):
        dist = qpos - (k0 + lax.broadcasted_iota(jnp.int32, (1, kcw), 1))
        return jnp.where(jnp.logical_and(dist >= 0, dist < WINDOW), POS_INF, NEG)

    cw0 = jnp.maximum(qs - WINDOW, 0) // kcw
    cw1 = (qs + tq + kcw - 1) // kcw

    for g in range(NSA_KV_HEADS):
        o_cmp, p_sum = _cmp_branch(_stack_heads(q, g, hg, scale), ck[:, g * HEAD_DIM:(g + 1) * HEAD_DIM],
                                   cv[:, g * HEAD_DIM:(g + 1) * HEAD_DIM], qpos_rep, hg, tq)
        sel = _select_blocks(p_sum, qpos, n_slc, n_sel).astype(BF16)

        def cap_body(c, _):
            k0 = pl.multiple_of(c * kc, kc)
            kpos = k0 + lax.broadcasted_iota(jnp.int32, (1, kc), 1)
            expand = jnp.where((kpos >> SEL_SHIFT) == blk_i, 1.0, 0.0).astype(BF16)
            sel_k = jnp.dot(sel, expand, preferred_element_type=F32)
            ok = jnp.logical_and(sel_k > 0.5, kpos <= qpos)
            cap_scr[:, pl.ds(k0, kc)] = jnp.where(ok, POS_INF, NEG)
            return 0

        lax.fori_loop(0, nk, cap_body, 0)
        qr_rows = _stack_heads(qr, g, hg, scale).astype(BF16)
        o_slc = _flash_group(qr_rows, slc_ref, g * HEAD_DIM, slc_ref, kvw + g * HEAD_DIM,
                             lambda k0: cap_scr[:, pl.ds(k0, kc)], m_scr, l_scr, acc_scr, 0, nk, kc, hg)
        o_win = _flash_group(qr_rows, win_ref, g * HEAD_DIM, win_ref, kvw + g * HEAD_DIM,
                             win_cap, m_scr, l_scr, acc_scr, cw0, cw1, kcw, hg)
        for j in range(hg):
            h = g * hg + j
            lane = IDX_HEADS + 3 * h
            rows = slice(j * tq, (j + 1) * tq)
            o_ref[:, h * HEAD_DIM:(h + 1) * HEAD_DIM] = (gates[:, lane:lane + 1] * o_cmp[rows]
                                                         + gates[:, lane + 1:lane + 2] * o_slc[rows]
                                                         + gates[:, lane + 2:lane + 3] * o_win[rows])


def _nsa_prompt(f, ck, cv, bx, t_len, tq, kc, kcw):
    nq = t_len // tq
    hg = NSA_HEADS // NSA_KV_HEADS
    n_pad = ck.shape[1]
    n_slc = -(-t_len // SEL_BLOCK)
    kvw = NSA_KV_HEADS * HEAD_DIM
    row = lambda w: pl.BlockSpec((tq, w), lambda b, i: (b * nq + i, 0))
    cmp_spec = pl.BlockSpec((None, n_pad, kvw), lambda b, i: (b, 0, 0))
    return pl.pallas_call(
        functools.partial(_nsa_prompt_kernel, tq=tq, kc=kc, kcw=kcw, n_slc=n_slc, n_sel=min(N_SEL, n_slc)),
        grid=(bx, nq),
        in_specs=[row(W_NQ), row(W_NQ), row(LANES), cmp_spec, cmp_spec,
                  pl.BlockSpec((None, 2 * kvw, t_len), lambda b, i: (b, 1, 0)),
                  pl.BlockSpec((None, 2 * kvw, t_len), lambda b, i: (b, 0, 0))],
        out_specs=row(W_NQ),
        out_shape=jax.ShapeDtypeStruct((bx * t_len, W_NQ), F32),
        scratch_shapes=[pltpu.VMEM((tq, t_len), F32),
                        pltpu.VMEM((hg * tq, 1), F32), pltpu.VMEM((hg * tq, 1), F32),
                        pltpu.VMEM((hg * tq, HEAD_DIM), F32)],
        compiler_params=pltpu.CompilerParams(dimension_semantics=('arbitrary', 'arbitrary'),
                                             vmem_limit_bytes=VMEM_LIMIT),
        name='nsa_prompt',
    )(f['nsa_q'], f['nsa_q_rot'], f['small'], ck, cv, f['nsa_t'], f['win_t'])


DEC_ROWS = 8


def _page_copies(cache_hbm, pt_ref, b, n_pages, page, buf, sem, slot):
    return [pltpu.make_async_copy(cache_hbm.at[pt_ref[b * n_pages + j]],
                                  buf.at[slot, :, pl.ds(j * page, page)], sem.at[slot])
            for j in range(n_pages)]


def _gather_step(caches, pt_ref, n_pages, page, bufs, sems):
    b = pl.program_id(0)
    nb = pl.num_programs(0)
    slot = b % 2

    @pl.when(b == 0)
    def _():
        for cache, buf, sem in zip(caches, bufs, sems):
            for cp in _page_copies(cache, pt_ref, 0, n_pages, page, buf, sem, 0):
                cp.start()

    @pl.when(b + 1 < nb)
    def _():
        for cache, buf, sem in zip(caches, bufs, sems):
            for cp in _page_copies(cache, pt_ref, b + 1, n_pages, page, buf, sem, 1 - slot):
                cp.start()

    for cache, buf, sem in zip(caches, bufs, sems):
        for cp in _page_copies(cache, pt_ref, b, n_pages, page, buf, sem, slot):
            cp.wait()
    return slot


def _attend_full(q_rows, kt, vt, cap, n_rep):
    nt = (((1,), (1,)), ((), ()))
    s = jnp.dot(q_rows, kt.astype(BF16), preferred_element_type=F32)
    s = jnp.minimum(s, jnp.concatenate([cap] * n_rep, axis=0))
    p = jnp.exp(s - jnp.max(s, axis=1, keepdims=True))
    den = jnp.maximum(jnp.sum(p, axis=1, keepdims=True), 1e-30)
    return lax.dot_general(p.astype(BF16), vt.astype(BF16), nt, preferred_element_type=F32) / den


def _decode_nsa_kernel(pt_ref, q_ref, qr_ref, small_ref, cache_hbm, new_ref, wst_ref, wnew_ref,
                       w1k_ref, pek_ref, w2k_ref, w1v_ref, pev_ref, w2v_ref, o_ref,
                       buf, sem, xk_scr, xv_scr, shift_scr, *, past, t_dec, n_pages, page):
    slot = _gather_step([cache_hbm], pt_ref, n_pages, page, [buf], [sem])
    rows = DEC_ROWS
    hg = NSA_HEADS // NSA_KV_HEADS
    kvw = NSA_KV_HEADS * HEAD_DIM
    scale = HEAD_DIM ** -0.5
    n_keys = buf.shape[2]
    buf[slot, :, pl.ds(past, LANES)] = new_ref[...]
    qpos = past + jnp.minimum(lax.broadcasted_iota(jnp.int32, (rows, 1), 0), t_dec - 1)
    qpos_rep = jnp.concatenate([qpos] * hg, axis=0)

    for j in range(n_pages):
        xk_scr[j * page:(j + 1) * page, :] = buf[slot, 0:kvw, j * page:(j + 1) * page].T
        xv_scr[j * page:(j + 1) * page, :] = buf[slot, kvw:2 * kvw, j * page:(j + 1) * page].T
    m = past // GROUP_TOKENS
    ck = _compress_slot(xk_scr, m, pek_ref, w1k_ref, w2k_ref, shift_scr, True)
    cv = _compress_slot(xv_scr, m, pev_ref, w1v_ref, w2v_ref, shift_scr, False)

    q = q_ref[...]
    qr = qr_ref[...]
    gates = _sigmoid(small_ref[...])
    n_slc = -(-(past + t_dec) // SEL_BLOCK)
    kpos = lax.broadcasted_iota(jnp.int32, (1, n_keys), 1)
    blk_i = lax.broadcasted_iota(jnp.int32, (LANES, 1), 0)
    expand = jnp.where((kpos >> SEL_SHIFT) == blk_i, 1.0, 0.0).astype(BF16)
    wb = wst_ref.shape[1]
    wpos = past - wb + lax.broadcasted_iota(jnp.int32, (1, wb + LANES), 1)
    wdist = qpos - wpos
    win_cap = jnp.where(jnp.logical_and(wdist >= 0, wdist < WINDOW), POS_INF, NEG)

    for g in range(NSA_KV_HEADS):
        gs = slice(g * HEAD_DIM, (g + 1) * HEAD_DIM)
        o_cmp, p_sum = _cmp_branch(_stack_heads(q, g, hg, scale), ck[:, gs], cv[:, gs], qpos_rep, hg, rows)
        sel = _select_blocks(p_sum, qpos, n_slc, min(N_SEL, n_slc)).astype(BF16)
        sel_k = jnp.dot(sel, expand, preferred_element_type=F32)
        slc_cap = jnp.where(jnp.logical_and(sel_k > 0.5, kpos <= qpos), POS_INF, NEG)
        qr_rows = _stack_heads(qr, g, hg, scale).astype(BF16)
        o_slc = _attend_full(qr_rows, buf[slot, 2 * kvw + g * HEAD_DIM:2 * kvw + (g + 1) * HEAD_DIM, :],
                             buf[slot, 3 * kvw + g * HEAD_DIM:3 * kvw + (g + 1) * HEAD_DIM, :], slc_cap, hg)
        wk = jnp.concatenate([wst_ref[g * HEAD_DIM:(g + 1) * HEAD_DIM, :],
                              wnew_ref[g * HEAD_DIM:(g + 1) * HEAD_DIM, :]], axis=1)
        wv = jnp.concatenate([wst_ref[kvw + g * HEAD_DIM:kvw + (g + 1) * HEAD_DIM, :],
                              wnew_ref[kvw + g * HEAD_DIM:kvw + (g + 1) * HEAD_DIM, :]], axis=1)
        o_win = _attend_full(qr_rows, wk, wv, win_cap, hg)
        for j in range(hg):
            h = g * hg + j
            lane = IDX_HEADS + 3 * h
            rs = slice(j * rows, (j + 1) * rows)
            o_ref[:, h * HEAD_DIM:(h + 1) * HEAD_DIM] = (gates[:, lane:lane + 1] * o_cmp[rs]
                                                         + gates[:, lane + 1:lane + 2] * o_slc[rs]
                                                         + gates[:, lane + 2:lane + 3] * o_win[rs])


def _decode_dsa_kernel(pt_ref, iq_ref, small_ref, dq_ref, idx_hbm, kv_hbm, inew_ref, kvnew_ref, o_ref,
                       ibuf, kbuf, isem, ksem, s_scr, *, past, t_dec, n_pages, page, topk):
    slot = _gather_step([idx_hbm, kv_hbm], pt_ref, n_pages, page, [ibuf, kbuf], [isem, ksem])
    rows = DEC_ROWS
    hg = DSA_HEADS // DSA_KV_HEADS
    scale = HEAD_DIM ** -0.5
    n_keys = ibuf.shape[2]
    kc = LANES
    nk = n_keys // kc
    ibuf[slot, :, pl.ds(past, LANES)] = inew_ref[...]
    kbuf[slot, :, pl.ds(past, LANES)] = kvnew_ref[...]
    qpos = past + jnp.minimum(lax.broadcasted_iota(jnp.int32, (rows, 1), 0), t_dec - 1)

    ik = ibuf[slot]
    ik_hi = ik.astype(BF16)
    ik_lo = (ik - ik_hi.astype(F32)).astype(BF16)
    k3 = jnp.concatenate([ik_hi, ik_lo, ik_hi], axis=0)
    iq = iq_ref[...]
    w = small_ref[...]
    sc = jnp.zeros((rows, n_keys), F32)
    for h in range(IDX_HEADS):
        d = jnp.dot(_split3_cols(iq[:, h * IDX_DIM:(h + 1) * IDX_DIM]), k3, preferred_element_type=F32)
        sc = sc + jnp.maximum(d, 0.0) * w[:, h:h + 1]
    sc = sc * ((IDX_DIM ** -0.5) * (IDX_HEADS ** -0.5))
    kpos = lax.broadcasted_iota(jnp.int32, (1, n_keys), 1)
    real_row = lax.broadcasted_iota(jnp.int32, (rows, 1), 0) < t_dec
    ok = jnp.logical_and(kpos <= qpos, jnp.logical_or(real_row, kpos == 0))
    s_scr[...] = jnp.where(ok, sc, -POS_INF)
    mn = jnp.min(jnp.where(ok, sc, POS_INF), axis=1, keepdims=True)
    mx = jnp.max(jnp.where(ok, sc, NEG), axis=1, keepdims=True)
    n_valid = jnp.where(real_row, qpos + 1, 1).astype(F32)
    t, cnt = _topk_threshold(s_scr, mn, mx, n_valid, topk, nk, kc)
    _write_topk_cap(s_scr, t, cnt, topk, nk, kc)
    cap = s_scr[...]

    dq = dq_ref[...]
    for g in range(DSA_KV_HEADS):
        q_rows = _stack_heads(dq, g, hg, scale).astype(BF16)
        o = _attend_full(q_rows, kbuf[slot, g * HEAD_DIM:(g + 1) * HEAD_DIM, :],
                         kbuf[slot, (DSA_KV_HEADS + g) * HEAD_DIM:(DSA_KV_HEADS + g + 1) * HEAD_DIM, :], cap, hg)
        for j in range(hg):
            h = g * hg + j
            o_ref[:, h * HEAD_DIM:(h + 1) * HEAD_DIM] = o[j * rows:(j + 1) * rows]


def _pad_rows(a2d, bd, t_dec):
    return jnp.pad(a2d.reshape(bd, t_dec, -1), ((0, 0), (0, DEC_ROWS - t_dec), (0, 0)))


def _new_cols(a_t, bd, t_dec):
    f = a_t.shape[0]
    return jnp.pad(a_t.reshape(f, bd, t_dec).transpose(1, 0, 2), ((0, 0), (0, 0), (0, LANES - t_dec)))


def _feature_major_pages(pool):
    n_pool, page = pool.shape[:2]
    nd = pool.ndim
    return pool.transpose((0,) + tuple(range(2, nd)) + (1,)).reshape(n_pool, -1, page)


def _decode_nsa(fs, cache_nsa, win_state, page_table, cw, past, t_dec):
    bd, n_pages = page_table.shape
    pool = _feature_major_pages(cache_nsa)
    page = pool.shape[2]
    f = pool.shape[1]
    n_keys = past + LANES
    wst = _feature_major_pages(win_state)
    kvw = NSA_KV_HEADS * HEAD_DIM
    per_b = lambda *shape: pl.BlockSpec((None,) + shape, lambda b, pt: (b,) + (0,) * len(shape))
    const = lambda a: pl.BlockSpec(a.shape, lambda b, pt: (0,) * a.ndim, pipeline_mode=pl.Buffered(1))
    grid_spec = pltpu.PrefetchScalarGridSpec(
        num_scalar_prefetch=1,
        grid=(bd,),
        in_specs=[per_b(DEC_ROWS, W_NQ), per_b(DEC_ROWS, W_NQ), per_b(DEC_ROWS, LANES),
                  pl.BlockSpec(memory_space=pl.ANY), per_b(f, LANES),
                  per_b(2 * kvw, wst.shape[2]), per_b(2 * kvw, LANES)] + [const(a) for a in cw],
        out_specs=per_b(DEC_ROWS, W_NQ),
        scratch_shapes=[pltpu.VMEM((2, f, n_keys), F32), pltpu.SemaphoreType.DMA((2,)),
                        pltpu.VMEM((past, kvw), F32), pltpu.VMEM((past, kvw), F32),
                        pltpu.VMEM((past // GROUP_TOKENS + 8, NSA_KV_HEADS * CMP_HID), F32)])
    out = pl.pallas_call(
        functools.partial(_decode_nsa_kernel, past=past, t_dec=t_dec, n_pages=n_pages, page=page),
        grid_spec=grid_spec,
        out_shape=jax.ShapeDtypeStruct((bd, DEC_ROWS, W_NQ), F32),
        compiler_params=pltpu.CompilerParams(dimension_semantics=('arbitrary',), vmem_limit_bytes=VMEM_LIMIT),
        name='decode_nsa',
    )(page_table.reshape(-1), _pad_rows(fs['nsa_q'], bd, t_dec), _pad_rows(fs['nsa_q_rot'], bd, t_dec),
      _pad_rows(fs['small'], bd, t_dec), pool, _new_cols(fs['nsa_t'], bd, t_dec), wst,
      _new_cols(fs['win_t'], bd, t_dec), *cw)
    return out[:, :t_dec].reshape(bd * t_dec, W_NQ)


def _decode_dsa(fs, cache_dsa, cache_idx, page_table, past, t_dec):
    bd, n_pages = page_table.shape
    kpool = _feature_major_pages(cache_dsa)
    ipool = _feature_major_pages(cache_idx)
    page = kpool.shape[2]
    n_keys = past + LANES
    topk = min(DSA_TOPK_MAX, (past + t_dec) // 4)
    per_b = lambda *shape: pl.BlockSpec((None,) + shape, lambda b, pt: (b,) + (0,) * len(shape))
    grid_spec = pltpu.PrefetchScalarGridSpec(
        num_scalar_prefetch=1,
        grid=(bd,),
        in_specs=[per_b(DEC_ROWS, W_IQ), per_b(DEC_ROWS, LANES), per_b(DEC_ROWS, W_DQ),
                  pl.BlockSpec(memory_space=pl.ANY), pl.BlockSpec(memory_space=pl.ANY),
                  per_b(IDX_DIM, LANES), per_b(W_DKV, LANES)],
        out_specs=per_b(DEC_ROWS, W_DQ),
        scratch_shapes=[pltpu.VMEM((2, IDX_DIM, n_keys), F32), pltpu.VMEM((2, W_DKV, n_keys), F32),
                        pltpu.SemaphoreType.DMA((2,)), pltpu.SemaphoreType.DMA((2,)),
                        pltpu.VMEM((DEC_ROWS, n_keys), F32)])
    out = pl.pallas_call(
        functools.partial(_decode_dsa_kernel, past=past, t_dec=t_dec, n_pages=n_pages, page=page, topk=topk),
        grid_spec=grid_spec,
        out_shape=jax.ShapeDtypeStruct((bd, DEC_ROWS, W_DQ), F32),
        compiler_params=pltpu.CompilerParams(dimension_semantics=('arbitrary',), vmem_limit_bytes=VMEM_LIMIT),
        name='decode_dsa',
    )(page_table.reshape(-1), _pad_rows(fs['idx_q'], bd, t_dec), _pad_rows(fs['small'], bd, t_dec),
      _pad_rows(fs['dsa_q'], bd, t_dec), ipool, kpool, _new_cols(fs['idx_t'], bd, t_dec),
      _new_cols(fs['dsa_t'], bd, t_dec))
    return out[:, :t_dec].reshape(bd * t_dec, W_DQ)


def _masked_softmax(s, mask):
    s = jnp.where(mask, s, NEG)
    p = jnp.where(mask, jnp.exp(s - jnp.max(s, axis=-1, keepdims=True)), 0.0)
    return p / jnp.maximum(jnp.sum(p, axis=-1, keepdims=True), 1e-30)


def _nsa_context(nsa_all, cmp_pe, cmp_w1, cmp_w2):
    Bx, L = nsa_all.shape[:2]
    G = NSA_KV_HEADS
    n_cmp = (L - CMP_BLOCK) // CMP_STRIDE + 1
    starts = np.arange(n_cmp, dtype=np.int32) * CMP_STRIDE
    blk_idx = starts[:, None] + np.arange(CMP_BLOCK, dtype=np.int32)[None, :]

    def compress(rows, pe, w1, w2):
        blk = rows[:, blk_idx] + pe[None, None, :, None, :]
        flat = jnp.swapaxes(blk, 2, 3).reshape(Bx, n_cmp, G, CMP_BLOCK * HEAD_DIM)
        hp = jnp.dot(flat, w1, precision=lax.Precision.HIGHEST)
        return jnp.dot(jax.nn.silu(hp), w2, precision=lax.Precision.HIGHEST)

    ck = compress(nsa_all[:, :, 0], cmp_pe[0], cmp_w1[0], cmp_w2[0])
    cv = compress(nsa_all[:, :, 1], cmp_pe[1], cmp_w1[1], cmp_w2[1])
    cmp_end = jnp.asarray(starts + CMP_BLOCK - 1)
    n_slc = -(-L // SEL_BLOCK)
    j = np.arange(n_slc, dtype=np.int32)[None, :]
    overlap = (starts[:, None] < (j + 1) * SEL_BLOCK) & (starts[:, None] + CMP_BLOCK > j * SEL_BLOCK)
    overlap = jnp.asarray(overlap, dtype=F32)
    pad = n_slc * SEL_BLOCK - L
    slc = jnp.pad(nsa_all[:, :, 2:4], ((0, 0), (0, pad), (0, 0), (0, 0), (0, 0)))
    slc = slc.reshape(Bx, n_slc, SEL_BLOCK, 2, G, HEAD_DIM).transpose(3, 0, 4, 1, 2, 5)
    return (ck, cv, cmp_end, overlap, slc[0], slc[1], min(N_SEL, n_slc))


def _nsa_attend(q, q_rot, gate, qpos, ctx, win_k, win_v, win_pos):
    hp = lax.Precision.HIGHEST
    ck, cv, cmp_end, overlap, slc_kb, slc_vb, n_sel = ctx
    Bx, Tq = q.shape[:2]
    G, Hg = NSA_KV_HEADS, NSA_HEADS // NSA_KV_HEADS
    scale = HEAD_DIM ** -0.5
    qg = q.reshape(Bx, Tq, G, Hg, HEAD_DIM)
    qr = q_rot.reshape(Bx, Tq, G, Hg, HEAD_DIM)
    s = jnp.einsum('btghd,bngd->btghn', qg, ck, precision=hp).astype(F32) * scale
    vis = (cmp_end[None, :] <= qpos[:, None])[None, :, None, None, :]
    p_cmp = _masked_softmax(s, vis)
    o_cmp = jnp.einsum('btghn,bngd->btghd', p_cmp, cv)
    imp = jnp.einsum('btgn,nj->btgj', jnp.sum(p_cmp, axis=3), overlap, precision=hp)
    n_slc = slc_kb.shape[2]
    jb = jnp.arange(n_slc, dtype=jnp.int32)[None, :]
    cur = (qpos // SEL_BLOCK)[:, None]
    forced = (jb == 0) | (jb == cur) | (jb == cur - 1)
    imp = jnp.where(forced[None, :, None, :], FORCED_SCORE, imp)
    imp = jnp.where((jb > cur)[None, :, None, :], NEG, imp)
    top_val, top_idx = lax.top_k(imp, n_sel)
    blk_ok = top_val > 0.5 * NEG
    bi = jnp.arange(Bx)[:, None, None, None]
    gi = jnp.arange(G)[None, None, :, None]
    sk = slc_kb[bi, gi, top_idx].reshape(Bx, Tq, G, n_sel * SEL_BLOCK, HEAD_DIM)
    sv = slc_vb[bi, gi, top_idx].reshape(Bx, Tq, G, n_sel * SEL_BLOCK, HEAD_DIM)
    kpos = top_idx[..., None] * SEL_BLOCK + jnp.arange(SEL_BLOCK, dtype=jnp.int32)
    ok = (blk_ok[..., None] & (kpos <= qpos[None, :, None, None, None])).reshape(Bx, Tq, G, 1, n_sel * SEL_BLOCK)
    s = jnp.einsum('btghd,btgmd->btghm', qr, sk).astype(F32) * scale
    o_slc = jnp.einsum('btghm,btgmd->btghd', _masked_softmax(s, ok), sv)
    s = jnp.einsum('btghd,bwgd->btghw', qr, win_k).astype(F32) * scale
    dist = qpos[:, None] - win_pos[None, :]
    wok = ((dist >= 0) & (dist < WINDOW) & (win_pos[None, :] >= 0))[None, :, None, None, :]
    o_win = jnp.einsum('btghw,bwgd->btghd', _masked_softmax(s, wok), win_v)
    g = gate.reshape(Bx, Tq, G, Hg, 3)
    o = g[..., 0:1] * o_cmp + g[..., 1:2] * o_slc + g[..., 2:3] * o_win
    return o.reshape(Bx, Tq, NSA_HEADS * HEAD_DIM)


def _dsa_attend(q, idx_q, idx_w, qpos, k, v, idx_k, topk):
    hp = lax.Precision.HIGHEST
    Bx, Tq = q.shape[:2]
    L = k.shape[1]
    G, Hg = DSA_KV_HEADS, DSA_HEADS // DSA_KV_HEADS
    causal = jnp.arange(L, dtype=jnp.int32)[None, :] <= qpos[:, None]
    dots = jnp.einsum('bthd,bsd->bths', idx_q, idx_k, precision=hp).astype(F32) * (IDX_DIM ** -0.5)
    score = jnp.einsum('bths,bth->bts', jax.nn.relu(dots), idx_w, precision=hp) * (IDX_HEADS ** -0.5)
    score = jnp.where(causal[None], score, NEG)
    _, sel = lax.top_k(score, topk)
    bi = jnp.arange(Bx)[:, None, None]
    sk = k[bi, sel]
    sv = v[bi, sel]
    ok = (sel <= qpos[None, :, None])[:, :, None, None, :]
    qg = q.reshape(Bx, Tq, G, Hg, HEAD_DIM)
    s = jnp.einsum('btghd,btkgd->btghk', qg, sk).astype(F32) * (HEAD_DIM ** -0.5)
    o = jnp.einsum('btghk,btkgd->btghd', _masked_softmax(s, ok), sv)
    return o.reshape(Bx, Tq, DSA_HEADS * HEAD_DIM)


def _rows_from_t(a_t, bx, t, lead):
    return a_t.reshape(lead + (bx, t)).transpose((len(lead), len(lead) + 1) + tuple(range(len(lead))))


def _rows_from_bt(a_bt, lead):
    bx, _, t = a_bt.shape
    n = len(lead)
    return a_bt.reshape((bx,) + lead + (t,)).transpose((0, n + 1) + tuple(range(1, n + 1)))


def kernel(x_prompt, x_sample, cache_nsa_kv, cache_dsa_kv, cache_dsa_idx_k, state_nsa_win_kv, page_table,
           c_prompt, c_sample, norm_g, w_ada, b_ada, w_in, cmp_pe, cmp_w1, cmp_w2,
           w_up_nsa, w_up_dsa, w_out, final_g):
    depth = norm_g.shape[0]
    assert depth == 1
    B, T, D = x_prompt.shape
    Bd, Td, _ = x_sample.shape
    l = 0
    past = page_table.shape[1] * cache_nsa_kv.shape[2]

    c_all = jnp.concatenate([c_prompt, c_sample], axis=0)
    pad = (-c_all.shape[0]) % 8
    c_all = jnp.pad(c_all, ((0, pad), (0, 0)))
    mod = _ada_mod(c_all, w_ada[l], b_ada[l])
    shift, scale, gate = mod[:, 0:D], mod[:, D:2 * D], mod[:, 2 * D:3 * D]

    weights = _prep_inproj_weights(w_in[l], D)

    tm = 256
    pos_p = jnp.arange(T, dtype=jnp.int32)
    fp = _in_projection(x_prompt.reshape(B * T, D), norm_g[l], scale[:B], shift[:B], T, pos_p, weights, tm, B)
    nsa_rows = _rows_from_bt(fp['nsa_t'], (4, NSA_KV_HEADS, HEAD_DIM))
    wb_p = min(WINDOW, T)
    new_win_prompt = _rows_from_bt(fp['win_t'][:, :, T - wb_p:], (2, NSA_KV_HEADS, HEAD_DIM))
    dsa_rows = _rows_from_bt(fp['dsa_t'], (2, DSA_KV_HEADS, HEAD_DIM))
    idx_rows = _rows_from_bt(fp['idx_t'], (IDX_DIM,))
    cw = _prep_compress_weights(cmp_pe[l], cmp_w1[l], cmp_w2[l])
    ck, cv = _compress_prompt(fp['cmp_k'], fp['cmp_v'], cw, B, T)
    tq = 128
    kc = min(512, T)
    o_n = _nsa_prompt(fp, ck, cv, B, T, tq, kc, 128)
    o_d = _dsa_prompt(fp, B, T, tq, kc)
    y_prompt = _mixer_out(x_prompt.reshape(B * T, D), o_n, o_d, fp, gate[:B], T, final_g,
                          w_up_nsa[l], w_up_dsa[l], w_out[l], tm).reshape(B, T, D)

    Rs = Bd * Td
    tms = min(256, Rs)
    pos_s = past + jnp.arange(Td, dtype=jnp.int32)
    pos_rows = jnp.tile(pos_s, tms // Td)
    rep = lambda a: jnp.repeat(a[B:B + Bd], Td, axis=0)
    fs = _in_projection(x_sample.reshape(Rs, D), norm_g[l], rep(scale), rep(shift), 1, pos_rows, weights, tms, 1)
    for name in ('nsa_t', 'win_t', 'dsa_t', 'idx_t'):
        fs[name] = fs[name][0]
    s_nsa_rows = _rows_from_t(fs['nsa_t'], Bd, Td, (4, NSA_KV_HEADS, HEAD_DIM))
    s_win_rows = _rows_from_t(fs['win_t'], Bd, Td, (2, NSA_KV_HEADS, HEAD_DIM))
    s_dsa_rows = _rows_from_t(fs['dsa_t'], Bd, Td, (2, DSA_KV_HEADS, HEAD_DIM))
    s_idx_rows = _rows_from_t(fs['idx_t'], Bd, Td, (IDX_DIM,))

    win_state = state_nsa_win_kv[l]
    so_n = _decode_nsa(fs, cache_nsa_kv[l], win_state, page_table, cw, past, Td)
    so_d = _decode_dsa(fs, cache_dsa_kv[l], cache_dsa_idx_k[l], page_table, past, Td)
    y_sample = _mixer_out(x_sample.reshape(Rs, D), so_n, so_d, fs,
                          rep(gate), 1, final_g, w_up_nsa[l], w_up_dsa[l], w_out[l], tms).reshape(Bd, Td, D)
    new_win_sample = jnp.concatenate([win_state, s_win_rows], axis=1)[:, Td:]

    return (y_prompt, y_sample, nsa_rows[None], dsa_rows[None], idx_rows[None], new_win_prompt[None],
            s_nsa_rows[None], s_dsa_rows[None], s_idx_rows[None], new_win_sample[None])
```

```python
import functools
import math

import numpy as np
import jax
import jax.numpy as jnp
from jax import lax
from jax.experimental import pallas as pl
from jax.experimental.pallas import tpu as pltpu

F32 = jnp.float32
BF16 = jnp.bfloat16

HEAD_DIM = 64
NSA_HEADS = 8
NSA_KV_HEADS = 2
DSA_HEADS = 8
DSA_KV_HEADS = 2
N_NSA_KV = 6
CMP_BLOCK = 32
CMP_STRIDE = 16
SEL_BLOCK = 64
N_SEL = 16
WINDOW = 512
IDX_HEADS = 4
IDX_DIM = 64
DSA_TOPK_MAX = 256
ROPE_THETA = 10000.0
Q_BLOCK = 128
EPS = 1e-6
NEG = -1e30
FORCED_SCORE = 1e4

LANES = 128
VMEM_LIMIT = 56 * 1024 * 1024

W_NQ = NSA_HEADS * HEAD_DIM
W_NKV = N_NSA_KV * NSA_KV_HEADS * HEAD_DIM
W_NG = NSA_HEADS * 3
W_DQ = DSA_HEADS * HEAD_DIM
W_DKV = 2 * DSA_KV_HEADS * HEAD_DIM
W_IQ = IDX_HEADS * IDX_DIM


def _cuts(d_model):
    widths = (W_NQ, W_NKV, W_NG, W_NQ, W_DQ, W_DKV, W_IQ, IDX_DIM, IDX_HEADS, W_DQ, 2 * d_model)
    c = np.concatenate([[0], np.cumsum(widths)])
    names = ('nq', 'nkv', 'ng', 'nz', 'dq', 'dkv', 'iq', 'ik', 'iw', 'dz', 'mg')
    return {n: (int(c[i]), int(c[i + 1])) for i, n in enumerate(names)}


def _split3_rows(w):
    wh = w.astype(BF16)
    wl = (w - wh.astype(F32)).astype(BF16)
    return jnp.concatenate([wh, wl, wh], axis=0)


def _split3_cols(x):
    xh = x.astype(BF16)
    xl = (x - xh.astype(F32)).astype(BF16)
    return jnp.concatenate([xh, xh, xl], axis=1)


def _silu(v):
    return v * (1.0 / (1.0 + jnp.exp(-v)))


def _sigmoid(v):
    return 1.0 / (1.0 + jnp.exp(-v))


def _ada_kernel(c_ref, w_ref, b_ref, o_ref):
    c3 = _split3_cols(_silu(c_ref[...]))
    o_ref[...] = jnp.dot(c3, w_ref[...], preferred_element_type=F32) + b_ref[...]


def _ada_mod(c_all, w_ada, b_ada):
    m, d = c_all.shape
    n = w_ada.shape[1]
    tn = 512
    w3 = _split3_rows(w_ada)
    return pl.pallas_call(
        _ada_kernel,
        grid=(n // tn,),
        in_specs=[pl.BlockSpec((m, d), lambda j: (0, 0)),
                  pl.BlockSpec((3 * d, tn), lambda j: (0, j)),
                  pl.BlockSpec((1, tn), lambda j: (0, j))],
        out_specs=pl.BlockSpec((m, tn), lambda j: (0, j)),
        out_shape=jax.ShapeDtypeStruct((m, n), F32),
        name='ada_mod',
    )(c_all, w3, b_ada.reshape(1, n))


def _rope_tf(v, cos, sin_signed):
    first_half = (lax.broadcasted_iota(jnp.int32, (1, LANES), 1) % HEAD_DIM) < (HEAD_DIM // 2)
    outs = []
    for c in range(v.shape[1] // LANES):
        vc = v[:, c * LANES:(c + 1) * LANES]
        sw = jnp.where(first_half, pltpu.roll(vc, LANES - HEAD_DIM // 2, 1), pltpu.roll(vc, HEAD_DIM // 2, 1))
        outs.append(vc * cos + sw * sin_signed)
    return jnp.concatenate(outs, axis=1) if len(outs) > 1 else outs[0]


def _rope_ft(v, cos_t, sin_t):
    half = HEAD_DIM // 2
    outs = []
    for hd in range(v.shape[0] // HEAD_DIM):
        x1 = v[hd * HEAD_DIM:hd * HEAD_DIM + half]
        x2 = v[hd * HEAD_DIM + half:(hd + 1) * HEAD_DIM]
        outs.append(x1 * cos_t - x2 * sin_t)
        outs.append(x2 * cos_t + x1 * sin_t)
    return jnp.concatenate(outs, axis=0)


def _inproj_kernel(x_ref, g_ref, sc_ref, sh_ref, cos_ref, sin_ref, cost_ref, sint_ref,
                   wp_ref, wa_ref, wt_ref, wti_ref,
                   q_ref, qrot_ref, iq_ref, small_ref, ck_ref,
                   cv_ref, snz_ref, sdz_ref, dq_ref, mg_ref,
                   nsat_ref, wint_ref, dsat_ref, idxt_ref, idx3_ref):
    x = x_ref[...]
    y = x * lax.rsqrt(jnp.mean(x * x, axis=-1, keepdims=True) + EPS) * g_ref[...]
    h = y * (1.0 + sc_ref[...]) + sh_ref[...]
    hh = h.astype(BF16)
    hl = (h - hh.astype(F32)).astype(BF16)
    h3 = jnp.concatenate([hh, hh, hl], axis=1)
    cos = cos_ref[...]
    sin = sin_ref[...]
    cos_t = cost_ref[...]
    sin_t = sint_ref[...]

    p = jnp.dot(h3, wp_ref[...], preferred_element_type=F32)
    q = p[:, 0:W_NQ]
    q_ref[...] = q
    qrot_ref[...] = _rope_tf(q, cos, sin)
    iq_ref[...] = _rope_tf(p[:, W_NQ:W_NQ + W_IQ], cos, sin)
    small_ref[...] = p[:, W_NQ + W_IQ:W_NQ + W_IQ + LANES]
    ck_ref[...] = p[:, W_NQ + W_IQ + LANES:W_NQ + W_IQ + 2 * LANES]

    a = jnp.dot(hh, wa_ref[...], preferred_element_type=F32)
    cv_ref[...] = a[:, 0:LANES]
    o = LANES
    snz_ref[...] = _silu(a[:, o:o + W_NQ])
    o += W_NQ
    sdz_ref[...] = _silu(a[:, o:o + W_DQ])
    o += W_DQ
    dq_ref[...] = _rope_tf(a[:, o:o + W_DQ], cos, sin)
    o += W_DQ
    mg_ref[...] = _sigmoid(a[:, o:])

    nt = (((1,), (1,)), ((), ()))
    t = lax.dot_general(wt_ref[...], hh, nt, preferred_element_type=F32)
    kv = 2 * HEAD_DIM
    nsat_ref[0:2 * kv, :] = t[0:2 * kv]
    nsat_ref[2 * kv:3 * kv, :] = _rope_ft(t[2 * kv:3 * kv], cos_t, sin_t)
    nsat_ref[3 * kv:4 * kv, :] = t[3 * kv:4 * kv]
    wint_ref[0:kv, :] = _rope_ft(t[4 * kv:5 * kv], cos_t, sin_t)
    wint_ref[kv:2 * kv, :] = t[5 * kv:6 * kv]
    dsat_ref[0:kv, :] = _rope_ft(t[6 * kv:7 * kv], cos_t, sin_t)
    dsat_ref[kv:2 * kv, :] = t[7 * kv:8 * kv]
    ti = lax.dot_general(wti_ref[...], h3, nt, preferred_element_type=F32)
    ti = _rope_ft(ti, cos_t, sin_t)
    idxt_ref[...] = ti
    ti_hi = ti.astype(BF16)
    ti_lo = (ti - ti_hi.astype(F32)).astype(BF16)
    idx3_ref[...] = jnp.concatenate([ti_hi, ti_lo, ti_hi], axis=0)


def _prep_inproj_weights(w_in, d_model):
    c = _cuts(d_model)
    sl = lambda name: w_in[:, c[name][0]:c[name][1]]
    nkv = sl('nkv')
    kv = NSA_KV_HEADS * HEAD_DIM
    small = jnp.concatenate([sl('iw'), sl('ng'), jnp.zeros((d_model, LANES - IDX_HEADS - W_NG), F32)], axis=1)
    wp = jnp.concatenate([sl('nq'), sl('iq'), small, nkv[:, 0:kv]], axis=1)
    wa = jnp.concatenate([nkv[:, kv:2 * kv], sl('nz'), sl('dz'), sl('dq'), sl('mg')], axis=1)
    wt = jnp.concatenate([nkv, sl('dkv')], axis=1).T
    wti = sl('ik')
    return _split3_rows(wp), wa.astype(BF16), wt.astype(BF16), _split3_rows(wti).T


def _rope_tables(pos):
    half = HEAD_DIM // 2
    inv = ROPE_THETA ** (-jnp.arange(half, dtype=F32) / half)
    ang = pos.astype(F32)[:, None] * inv[None, :]
    cos = jnp.cos(ang)
    sin = jnp.sin(ang)
    cos_tf = jnp.tile(cos, (1, LANES // half))
    sin_tf = jnp.tile(jnp.concatenate([-sin, sin], axis=1), (1, LANES // HEAD_DIM))
    return cos_tf, sin_tf, cos.T, sin.T


def _in_projection(x2d, norm_g, scale, shift, rows_per_mod, pos, weights, tm, nb):
    r, d = x2d.shape
    tpb = (r // nb) // tm
    wp, wa, wt, wti = weights
    n_tiles = r // tm
    cos_tf, sin_tf, cos_t, sin_t = _rope_tables(pos)
    p_tiles = pos.shape[0] // tm
    if rows_per_mod == 1:
        mod_spec = pl.BlockSpec((tm, d), lambda i: (i, 0))
        sc, sh = scale, shift
    else:
        per = rows_per_mod // tm
        mod_spec = pl.BlockSpec((None, 1, d), lambda i: (i // per, 0, 0))
        sc, sh = scale[:, None, :], shift[:, None, :]
    const = lambda shape: pl.BlockSpec(shape, lambda i: (0,) * len(shape), pipeline_mode=pl.Buffered(1))
    tf = lambda w: pl.BlockSpec((tm, w), lambda i: (i, 0))
    ft = lambda w: pl.BlockSpec((None, w, tm), lambda i: (i // tpb, 0, i % tpb))
    tf_widths = (W_NQ, W_NQ, W_IQ, LANES, LANES, LANES, W_NQ, W_DQ, W_DQ, 2 * d)
    ft_widths = (4 * LANES, 2 * LANES, 2 * LANES, IDX_DIM)
    outs = pl.pallas_call(
        _inproj_kernel,
        grid=(n_tiles,),
        in_specs=[pl.BlockSpec((tm, d), lambda i: (i, 0)),
                  const((1, d)), mod_spec, mod_spec,
                  pl.BlockSpec((tm, LANES), lambda i: (i % p_tiles, 0)),
                  pl.BlockSpec((tm, LANES), lambda i: (i % p_tiles, 0)),
                  pl.BlockSpec((HEAD_DIM // 2, tm), lambda i: (0, i % p_tiles)),
                  pl.BlockSpec((HEAD_DIM // 2, tm), lambda i: (0, i % p_tiles)),
                  const(wp.shape), const(wa.shape), const(wt.shape), const(wti.shape)],
        out_specs=[tf(w) for w in tf_widths] + [ft(w) for w in ft_widths] + [ft(3 * IDX_DIM)],
        out_shape=[jax.ShapeDtypeStruct((r, w), F32) for w in tf_widths]
        + [jax.ShapeDtypeStruct((nb, w, r // nb), F32) for w in ft_widths]
        + [jax.ShapeDtypeStruct((nb, 3 * IDX_DIM, r // nb), BF16)],
        compiler_params=pltpu.CompilerParams(dimension_semantics=('arbitrary',), vmem_limit_bytes=VMEM_LIMIT),
        name='in_projection',
    )(x2d, norm_g.reshape(1, d), sc, sh, cos_tf, sin_tf, cos_t, sin_t, wp, wa, wt, wti)
    names = ('nsa_q', 'nsa_q_rot', 'idx_q', 'small', 'cmp_k', 'cmp_v', 'silu_nz', 'silu_dz', 'dsa_q', 'merge',
             'nsa_t', 'win_t', 'dsa_t', 'idx_t', 'idx3_t')
    return dict(zip(names, outs))


SMALL_ROWS = 32


def _inproj_prompt_kernel(x_ref, g_ref, sc_ref, sh_ref, cos_ref, sin_ref, cost_ref, sint_ref,
                          wp_ref, wa_ref, wtp_ref, wt_ref,
                          q_ref, small_ref, ck_ref, idx3_ref,
                          cv_ref, slck_ref, wink_ref, dsak_ref, snz_ref, sdz_ref, mg_ref,
                          iqt_ref, smallt_ref, qrt_ref, dqt_ref, nsat_ref, wint_ref, dsat_ref, idxt_ref):
    x = x_ref[...]
    y = x * lax.rsqrt(jnp.mean(x * x, axis=-1, keepdims=True) + EPS) * g_ref[...]
    h = y * (1.0 + sc_ref[...]) + sh_ref[...]
    hh = h.astype(BF16)
    hl = (h - hh.astype(F32)).astype(BF16)
    h3 = jnp.concatenate([hh, hh, hl], axis=1)
    cos = cos_ref[...]
    sin = sin_ref[...]
    cos_t = cost_ref[...]
    sin_t = sint_ref[...]
    nt = (((1,), (1,)), ((), ()))
    kv = NSA_KV_HEADS * HEAD_DIM

    p = jnp.dot(h3, wp_ref[...], preferred_element_type=F32)
    q_ref[...] = p[:, 0:W_NQ]
    small_ref[...] = p[:, W_NQ:W_NQ + LANES]
    ck_ref[...] = p[:, W_NQ + LANES:W_NQ + 2 * LANES]
    ik = _rope_tf(p[:, W_NQ + 2 * LANES:W_NQ + 3 * LANES], cos, sin)
    ik_hi = ik.astype(BF16)
    ik_lo = (ik - ik_hi.astype(F32)).astype(BF16)
    first = lax.broadcasted_iota(jnp.int32, (1, LANES), 1) < IDX_DIM
    idx3_ref[...] = jnp.concatenate([jnp.where(first, ik_hi, ik_lo),
                                     jnp.where(first, ik_hi, jnp.zeros_like(ik_hi))], axis=1)

    a = jnp.dot(hh, wa_ref[...], preferred_element_type=F32)
    cv_ref[...] = a[:, 0:kv]
    slck_ref[...] = _rope_tf(a[:, kv:2 * kv], cos, sin).astype(BF16)
    wink_ref[...] = _rope_tf(a[:, 2 * kv:3 * kv], cos, sin).astype(BF16)
    dsak_ref[...] = _rope_tf(a[:, 3 * kv:4 * kv], cos, sin).astype(BF16)
    o = 4 * kv
    snz_ref[...] = _silu(a[:, o:o + W_NQ])
    o += W_NQ
    sdz_ref[...] = _silu(a[:, o:o + W_DQ])
    o += W_DQ
    mg_ref[...] = _sigmoid(a[:, o:])

    tp = lax.dot_general(wtp_ref[...], h3, nt, preferred_element_type=F32)
    iqt_ref[...] = _rope_ft(tp[0:W_IQ], cos_t, sin_t)
    smallt_ref[...] = tp[W_IQ:W_IQ + SMALL_ROWS]

    t = lax.dot_general(wt_ref[...], hh, nt, preferred_element_type=F32)
    qrt_ref[...] = _rope_ft(t[0:W_NQ], cos_t, sin_t)
    o = W_NQ
    dqt_ref[...] = _rope_ft(t[o:o + W_DQ], cos_t, sin_t)
    o += W_DQ
    nsat_ref[0:2 * kv, :] = t[o:o + 2 * kv]
    nsat_ref[2 * kv:3 * kv, :] = _rope_ft(t[o + 2 * kv:o + 3 * kv], cos_t, sin_t)
    nsat_ref[3 * kv:4 * kv, :] = t[o + 3 * kv:o + 4 * kv]
    wint_ref[0:kv, :] = _rope_ft(t[o + 4 * kv:o + 5 * kv], cos_t, sin_t)
    wint_ref[kv:2 * kv, :] = t[o + 5 * kv:o + 6 * kv]
    dsat_ref[0:kv, :] = _rope_ft(t[o + 6 * kv:o + 7 * kv], cos_t, sin_t)
    dsat_ref[kv:2 * kv, :] = t[o + 7 * kv:o + 8 * kv]
    idxt_ref[...] = _rope_ft(t[o + 8 * kv:o + 8 * kv + IDX_DIM], cos_t, sin_t)


def _prep_inproj_weights_prompt(w_in, d_model):
    c = _cuts(d_model)
    sl = lambda name: w_in[:, c[name][0]:c[name][1]]
    nkv = sl('nkv')
    kv = NSA_KV_HEADS * HEAD_DIM
    zeros = lambda n: jnp.zeros((d_model, n), F32)
    small = jnp.concatenate([sl('iw'), sl('ng')], axis=1)
    wp = jnp.concatenate([sl('nq'), small, zeros(LANES - small.shape[1]), nkv[:, 0:kv], sl('ik'), sl('ik')], axis=1)
    wa = jnp.concatenate([nkv[:, kv:2 * kv], nkv[:, 2 * kv:3 * kv], nkv[:, 4 * kv:5 * kv], sl('dkv')[:, 0:kv],
                          sl('nz'), sl('dz'), sl('mg')], axis=1)
    wtp = jnp.concatenate([sl('iq'), small, zeros(SMALL_ROWS - small.shape[1])], axis=1)
    wt = jnp.concatenate([sl('nq'), sl('dq'), nkv, sl('dkv'), sl('ik')], axis=1)
    return _split3_rows(wp), wa.astype(BF16), _split3_rows(wtp).T, wt.T.astype(BF16)


def _in_projection_prompt(x2d, norm_g, scale, shift, t_len, weights, tm):
    r, d = x2d.shape
    nb = r // t_len
    tpb = t_len // tm
    wp, wa, wtp, wt = weights
    cos_tf, sin_tf, cos_t, sin_t = _rope_tables(jnp.arange(t_len, dtype=jnp.int32))
    mod_spec = pl.BlockSpec((None, 1, d), lambda i: (i // tpb, 0, 0))
    const = lambda shape: pl.BlockSpec(shape, lambda i: (0,) * len(shape), pipeline_mode=pl.Buffered(1))
    tf = lambda w: pl.BlockSpec((tm, w), lambda i: (i, 0))
    ft = lambda w: pl.BlockSpec((None, w, tm), lambda i: (i // tpb, 0, i % tpb))
    kv = NSA_KV_HEADS * HEAD_DIM
    tf_outs = (('nsa_q', W_NQ, F32), ('small', LANES, F32), ('cmp_k', kv, F32), ('idx3', 2 * LANES, BF16),
               ('cmp_v', kv, F32), ('slc_k', kv, BF16), ('win_k', kv, BF16), ('dsa_k', kv, BF16),
               ('silu_nz', W_NQ, F32), ('silu_dz', W_DQ, F32), ('merge', 2 * d, F32))
    ft_outs = (('idx_q_t', W_IQ), ('small_t', SMALL_ROWS), ('nsa_q_rot_t', W_NQ), ('dsa_q_t', W_DQ),
               ('nsa_t', 4 * kv), ('win_t', 2 * kv), ('dsa_t', W_DKV), ('idx_t', IDX_DIM))
    outs = pl.pallas_call(
        _inproj_prompt_kernel,
        grid=(r // tm,),
        in_specs=[pl.BlockSpec((tm, d), lambda i: (i, 0)),
                  const((1, d)), mod_spec, mod_spec,
                  pl.BlockSpec((tm, LANES), lambda i: (i % tpb, 0)),
                  pl.BlockSpec((tm, LANES), lambda i: (i % tpb, 0)),
                  pl.BlockSpec((HEAD_DIM // 2, tm), lambda i: (0, i % tpb)),
                  pl.BlockSpec((HEAD_DIM // 2, tm), lambda i: (0, i % tpb)),
                  const(wp.shape), const(wa.shape), const(wtp.shape), const(wt.shape)],
        out_specs=[tf(w) for _, w, _ in tf_outs] + [ft(w) for _, w in ft_outs],
        out_shape=[jax.ShapeDtypeStruct((r, w), dt) for _, w, dt in tf_outs]
        + [jax.ShapeDtypeStruct((nb, w, t_len), F32) for _, w in ft_outs],
        compiler_params=pltpu.CompilerParams(dimension_semantics=('arbitrary',), vmem_limit_bytes=VMEM_LIMIT),
        name='in_projection_prompt',
    )(x2d, norm_g.reshape(1, d), scale[:, None, :], shift[:, None, :], cos_tf, sin_tf, cos_t, sin_t,
      wp, wa, wtp, wt)
    return dict(zip([n for n, _, _ in tf_outs] + [n for n, _ in ft_outs], outs))


def _mixer_out_kernel(x_ref, on_ref, od_ref, snz_ref, sdz_ref, mg_ref, ga_ref, fg_ref,
                      wun_ref, wud_ref, wo_ref, y_ref):
    d = x_ref.shape[1]
    u_n = jnp.dot((on_ref[...] * snz_ref[...]).astype(BF16), wun_ref[...], preferred_element_type=F32)
    u_d = jnp.dot((od_ref[...] * sdz_ref[...]).astype(BF16), wud_ref[...], preferred_element_type=F32)
    mg = mg_ref[...]
    mix = mg[:, 0:d] * u_n + mg[:, d:2 * d] * u_d
    z = x_ref[...] + ga_ref[...] * jnp.dot(mix.astype(BF16), wo_ref[...], preferred_element_type=F32)
    y_ref[...] = z * lax.rsqrt(jnp.mean(z * z, axis=-1, keepdims=True) + EPS) * fg_ref[...]


def _mixer_out(x2d, o_n, o_d, f, gate, rows_per_mod, final_g, w_up_nsa, w_up_dsa, w_out, tm):
    r, d = x2d.shape
    if rows_per_mod == 1:
        mod_spec = pl.BlockSpec((tm, d), lambda i: (i, 0))
        ga = gate
    else:
        per = rows_per_mod // tm
        mod_spec = pl.BlockSpec((None, 1, d), lambda i: (i // per, 0, 0))
        ga = gate[:, None, :]
    const = lambda shape: pl.BlockSpec(shape, lambda i: (0,) * len(shape), pipeline_mode=pl.Buffered(1))
    tf = lambda w: pl.BlockSpec((tm, w), lambda i: (i, 0))
    return pl.pallas_call(
        _mixer_out_kernel,
        grid=(r // tm,),
        in_specs=[tf(d), tf(W_NQ), tf(W_DQ), tf(W_NQ), tf(W_DQ), tf(2 * d), mod_spec, const((1, d)),
                  const(w_up_nsa.shape), const(w_up_dsa.shape), const(w_out.shape)],
        out_specs=tf(d),
        out_shape=jax.ShapeDtypeStruct((r, d), F32),
        compiler_params=pltpu.CompilerParams(dimension_semantics=('arbitrary',), vmem_limit_bytes=VMEM_LIMIT),
        name='mixer_out',
    )(x2d, o_n, o_d, f['silu_nz'], f['silu_dz'], f['merge'], ga, final_g.reshape(1, d),
      w_up_nsa.astype(BF16), w_up_dsa.astype(BF16), w_out.astype(BF16))


POS_INF = float('inf')


def _count_ge(s_scr, x, nk, kc):
    rows = s_scr.shape[0]
    xb = jnp.broadcast_to(x, (rows, LANES))

    def body(c, acc):
        k0 = pl.multiple_of(c * kc, kc)
        for j in range(kc // LANES):
            sj = s_scr[:, pl.ds(k0 + j * LANES, LANES)]
            acc = acc + jnp.where(sj >= xb, 1.0, 0.0)
        return acc

    acc = lax.fori_loop(0, nk, body, jnp.zeros((rows, LANES), F32))
    return jnp.sum(acc, axis=1, keepdims=True)


_MAG_BITS = 0x7FFFFFFF


def _ordered_key(f):
    b = lax.bitcast_convert_type(f, jnp.int32)
    return jnp.where(b < 0, b ^ _MAG_BITS, b)


def _from_ordered_key(key):
    return lax.bitcast_convert_type(jnp.where(key < 0, key ^ _MAG_BITS, key), F32)


def _topk_threshold(s_scr, row_min, row_max, n_valid, k, nk, kc):
    kf = float(k)

    def cond(st):
        _, _, _, _, done, it = st
        return jnp.logical_and(it < 40, jnp.min(done) < 0.5)

    def body(st):
        lo, hi, cnt_lo, x, _, it = st
        c = _count_ge(s_scr, _from_ordered_key(x), nk, kc)
        up = c >= kf
        lo = jnp.where(up, x, lo)
        cnt_lo = jnp.where(up, c, cnt_lo)
        hi = jnp.where(up, hi, x)
        mid = (lo >> 1) + (hi >> 1) + (lo & hi & 1)
        done = jnp.where(jnp.logical_or(cnt_lo <= kf, mid == lo), 1.0, 0.0)
        return lo, hi, cnt_lo, mid, done, it + 1

    done0 = jnp.where(n_valid <= kf, 1.0, 0.0)
    hi0 = _ordered_key(row_max)
    st = (_ordered_key(row_min), hi0, n_valid, hi0, done0, jnp.int32(0))
    lo, _, cnt_lo, _, _, _ = lax.while_loop(cond, body, st)
    return jnp.where(n_valid <= kf, NEG, _from_ordered_key(lo)), cnt_lo


def _write_topk_cap(s_scr, t, cnt_ge, k, nk, kc):
    rows = s_scr.shape[0]
    kf = float(k)
    tb = jnp.broadcast_to(t, (rows, LANES))
    has_tie = jnp.max(cnt_ge) > kf

    @pl.when(jnp.logical_not(has_tie))
    def _():
        def body(c, _):
            k0 = pl.multiple_of(c * kc, kc)
            for j in range(kc // LANES):
                sl = pl.ds(k0 + j * LANES, LANES)
                s_scr[:, sl] = jnp.where(s_scr[:, sl] >= tb, POS_INF, NEG)
            return 0
        lax.fori_loop(0, nk, body, 0)

    @pl.when(has_tie)
    def _():
        def cnt_body(c, acc):
            k0 = pl.multiple_of(c * kc, kc)
            for j in range(kc // LANES):
                acc = acc + jnp.where(s_scr[:, pl.ds(k0 + j * LANES, LANES)] > tb, 1.0, 0.0)
            return acc
        cnt_gt = jnp.sum(lax.fori_loop(0, nk, cnt_body, jnp.zeros((rows, LANES), F32)), axis=1, keepdims=True)
        need = kf - cnt_gt
        r_i = lax.broadcasted_iota(jnp.int32, (LANES, LANES), 0)
        c_i = lax.broadcasted_iota(jnp.int32, (LANES, LANES), 1)
        upper = jnp.where(r_i < c_i, 1.0, 0.0).astype(BF16)

        def body(c, run):
            k0 = pl.multiple_of(c * kc, kc)
            for j in range(kc // LANES):
                sl = pl.ds(k0 + j * LANES, LANES)
                sj = s_scr[:, sl]
                eq = jnp.where(sj == tb, 1.0, 0.0)
                before = run + jnp.dot(eq.astype(BF16), upper, preferred_element_type=F32)
                keep = jnp.logical_or(sj > tb, jnp.logical_and(sj == tb, before < need))
                s_scr[:, sl] = jnp.where(keep, POS_INF, NEG)
                run = run + jnp.sum(eq, axis=1, keepdims=True)
            return run
        lax.fori_loop(0, nk, body, jnp.zeros((rows, 1), F32))


def _flash_group(q_rows, kt_ref, krow0, vt_ref, vrow0, cap_fn, m_scr, l_scr, acc_scr, c0, c1, kc, n_rep):
    m_scr[...] = jnp.full(m_scr.shape, NEG, F32)
    l_scr[...] = jnp.zeros(l_scr.shape, F32)
    acc_scr[...] = jnp.zeros(acc_scr.shape, F32)
    nt = (((1,), (1,)), ((), ()))

    def body(c, _):
        k0 = pl.multiple_of(c * kc, kc)
        kt = kt_ref[pl.ds(krow0, HEAD_DIM), pl.ds(k0, kc)].astype(BF16)
        vt = vt_ref[pl.ds(vrow0, HEAD_DIM), pl.ds(k0, kc)].astype(BF16)
        s = jnp.dot(q_rows, kt, preferred_element_type=F32)
        cap = cap_fn(k0)
        s = jnp.minimum(s, jnp.concatenate([cap] * n_rep, axis=0))
        m_old = m_scr[...]
        m_new = jnp.maximum(m_old, jnp.max(s, axis=1, keepdims=True))
        p = jnp.exp(s - m_new)
        alpha = jnp.exp(m_old - m_new)
        l_scr[...] = alpha * l_scr[...] + jnp.sum(p, axis=1, keepdims=True)
        acc_scr[...] = alpha * acc_scr[...] + lax.dot_general(p.astype(BF16), vt, nt, preferred_element_type=F32)
        m_scr[...] = m_new
        return 0

    lax.fori_loop(c0, c1, body, 0)
    return acc_scr[...] / jnp.maximum(l_scr[...], 1e-30)


ACC_ROWS = 64


def _col_reduce(x, op):
    n, w = x.shape
    if n > ACC_ROWS and n % ACC_ROWS == 0:
        x = op(x.reshape(n // ACC_ROWS, ACC_ROWS, w), axis=0)
    return op(x, axis=0, keepdims=True)


def _count_ge_t(st_scr, x, nk, kc):
    tq = st_scr.shape[1]

    def body(c, acc):
        k0 = pl.multiple_of(c * kc, kc)
        hit = jnp.where(st_scr[pl.ds(k0, kc), :] >= x, 1.0, 0.0)
        return acc + jnp.sum(hit.reshape(kc // ACC_ROWS, ACC_ROWS, tq), axis=0)

    acc = lax.fori_loop(0, nk, body, jnp.zeros((ACC_ROWS, tq), F32))
    return jnp.sum(acc, axis=0, keepdims=True)


def _topk_threshold_t(st_scr, col_min, col_max, n_valid, k, nk, kc):
    kf = float(k)

    def probe(lo, hi, cnt_lo, cnt_hi, x):
        c = _count_ge_t(st_scr, _from_ordered_key(x), nk, kc)
        up = c >= kf
        return (jnp.where(up, x, lo), jnp.where(up, hi, x), jnp.where(up, c, cnt_lo), jnp.where(up, cnt_hi, c))

    def midpoint(lo, hi):
        return (lo >> 1) + (hi >> 1) + (lo & hi & 1)

    def cond(st):
        _, _, _, _, done, it = st
        return jnp.logical_and(it < 40, jnp.min(done) < 0.5)

    def body(st):
        lo, hi, cnt_lo, cnt_hi, _, it = st
        f_lo = _from_ordered_key(lo)
        f_hi = _from_ordered_key(hi)
        frac = (kf - 0.5 - cnt_hi) / jnp.maximum(cnt_lo - cnt_hi, 1.0)
        guess = _ordered_key(f_hi - (f_hi - f_lo) * frac)
        inside = jnp.logical_and(guess > lo, guess < hi)
        lo, hi, cnt_lo, cnt_hi = probe(lo, hi, cnt_lo, cnt_hi, jnp.where(inside, guess, midpoint(lo, hi)))
        lo, hi, cnt_lo, cnt_hi = probe(lo, hi, cnt_lo, cnt_hi, midpoint(lo, hi))
        done = jnp.where(jnp.logical_or(cnt_lo <= kf, midpoint(lo, hi) == lo), 1.0, 0.0)
        return lo, hi, cnt_lo, cnt_hi, done, it + 1

    lo0 = _ordered_key(col_min)
    hi0 = _ordered_key(col_max)
    lo, hi, cnt_lo, cnt_hi = probe(lo0, hi0, n_valid, jnp.zeros_like(n_valid), hi0)
    done0 = jnp.where(jnp.logical_or(n_valid <= kf, midpoint(lo, hi) == lo), 1.0, 0.0)
    lo, _, cnt_lo, _, _, _ = lax.while_loop(cond, body, (lo, hi, cnt_lo, cnt_hi, done0, jnp.int32(0)))
    return jnp.where(n_valid <= kf, NEG, _from_ordered_key(lo)), cnt_lo


def _write_topk_cap_t(st_scr, t, cnt_ge, k, nk, kc):
    kf = float(k)
    has_tie = jnp.max(cnt_ge) > kf

    @pl.when(jnp.logical_not(has_tie))
    def _():
        def body(c, _):
            sl = pl.ds(pl.multiple_of(c * kc, kc), kc)
            st_scr[sl, :] = jnp.where(st_scr[sl, :] >= t, POS_INF, NEG)
            return 0
        lax.fori_loop(0, nk, body, 0)

    @pl.when(has_tie)
    def _():
        def cnt_body(c, acc):
            sl = pl.ds(pl.multiple_of(c * kc, kc), kc)
            return acc + jnp.sum(jnp.where(st_scr[sl, :] > t, 1.0, 0.0), axis=0, keepdims=True)
        cnt_gt = lax.fori_loop(0, nk, cnt_body, jnp.zeros_like(t))
        need = kf - cnt_gt
        r_i = lax.broadcasted_iota(jnp.int32, (kc, kc), 0)
        c_i = lax.broadcasted_iota(jnp.int32, (kc, kc), 1)
        lower = jnp.where(c_i < r_i, 1.0, 0.0).astype(BF16)

        def body(c, run):
            sl = pl.ds(pl.multiple_of(c * kc, kc), kc)
            s = st_scr[sl, :]
            eq = jnp.where(s == t, 1.0, 0.0)
            before = run + jnp.dot(lower, eq.astype(BF16), preferred_element_type=F32)
            keep = jnp.logical_or(s > t, jnp.logical_and(s == t, before < need))
            st_scr[sl, :] = jnp.where(keep, POS_INF, NEG)
            return run + jnp.sum(eq, axis=0, keepdims=True)
        lax.fori_loop(0, nk, body, jnp.zeros_like(t))


def _q_pad(qt, g, hg, scale):
    qg = jnp.concatenate([qt[(g * hg + j) * HEAD_DIM:(g * hg + j + 1) * HEAD_DIM, :] for j in range(hg)],
                         axis=1) * scale
    z = jnp.zeros_like(qg)
    parts = [z] * NSA_KV_HEADS
    parts[g] = qg
    return jnp.concatenate(parts, axis=0).astype(BF16)


def _flash_t(q_pads, k_ref, vt_ref, vrow0s, cap_fns, c0, c1, kc, n_rep):
    n = q_pads[0].shape[1]
    n_g = len(q_pads)

    def body(c, carry):
        k0 = pl.multiple_of(c * kc, kc)
        k_chunk = k_ref[pl.ds(k0, kc), :]
        out = []
        for g in range(n_g):
            m, l, acc = carry[g]
            s = jnp.dot(k_chunk, q_pads[g], preferred_element_type=F32)
            s = jnp.minimum(s, jnp.concatenate([cap_fns[g](k0)] * n_rep, axis=1))
            m_new = jnp.maximum(m, _col_reduce(s, jnp.max))
            p = jnp.exp(s - m_new)
            alpha = jnp.exp(m - m_new)
            l = alpha * l + _col_reduce(p, jnp.sum)
            vt = vt_ref[pl.ds(vrow0s[g], HEAD_DIM), pl.ds(k0, kc)].astype(BF16)
            acc = alpha * acc + jnp.dot(vt, p.astype(BF16), preferred_element_type=F32)
            out.append((m_new, l, acc))
        return tuple(out)

    init = tuple((jnp.full((1, n), NEG, F32), jnp.zeros((1, n), F32), jnp.zeros((HEAD_DIM, n), F32))
                 for _ in range(n_g))
    fin = lax.fori_loop(c0, c1, body, init)
    return [acc / jnp.maximum(l, 1e-30) for _, l, acc in fin]


def _store_heads(o_ref, o_t, g, hg, tq):
    for j in range(hg):
        h = g * hg + j
        o_ref[:, h * HEAD_DIM:(h + 1) * HEAD_DIM] = o_t[:, j * tq:(j + 1) * tq].T


def _dsa_prompt_kernel(iqt_ref, smallt_ref, dqt_ref, idx3_ref, k_ref, kvt_ref, o_ref, st_scr,
                       *, tq, kc, topk):
    qs = pl.program_id(1) * tq
    nk = (qs + tq + kc - 1) // kc
    qpos = qs + lax.broadcasted_iota(jnp.int32, (1, tq), 1)

    iqt = iqt_ref[...]
    q3 = []
    for h in range(IDX_HEADS):
        qh = iqt[h * IDX_DIM:(h + 1) * IDX_DIM, :]
        q_hi = qh.astype(BF16)
        q_lo = (qh - q_hi.astype(F32)).astype(BF16)
        q3.append(jnp.concatenate([q_hi, q_hi, q_lo, jnp.zeros_like(q_hi)], axis=0))
    wt = smallt_ref[...]
    out_scale = (IDX_DIM ** -0.5) * (IDX_HEADS ** -0.5)

    def score_body(c, carry):
        mn, mx = carry
        k0 = pl.multiple_of(c * kc, kc)
        k3 = idx3_ref[pl.ds(k0, kc), :]
        sc = jnp.zeros((kc, tq), F32)
        for h in range(IDX_HEADS):
            d = jnp.dot(k3, q3[h], preferred_element_type=F32)
            sc = sc + jnp.maximum(d, 0.0) * wt[h:h + 1, :]
        sc = sc * out_scale
        ok = (k0 + lax.broadcasted_iota(jnp.int32, (kc, 1), 0)) <= qpos
        st_scr[pl.ds(k0, kc), :] = jnp.where(ok, sc, -POS_INF)
        mn = jnp.minimum(mn, _col_reduce(jnp.where(ok, sc, POS_INF), jnp.min))
        mx = jnp.maximum(mx, _col_reduce(jnp.where(ok, sc, NEG), jnp.max))
        return mn, mx

    mn, mx = lax.fori_loop(0, nk, score_body,
                           (jnp.full((1, tq), POS_INF, F32), jnp.full((1, tq), NEG, F32)))
    t, cnt = _topk_threshold_t(st_scr, mn, mx, (qpos + 1).astype(F32), topk, nk, kc)
    _write_topk_cap_t(st_scr, t, cnt, topk, nk, kc)

    hg = DSA_HEADS // DSA_KV_HEADS
    dqt = dqt_ref[...]
    groups = range(DSA_KV_HEADS)
    cap_fn = lambda k0: st_scr[pl.ds(k0, kc), :]
    o_ts = _flash_t([_q_pad(dqt, g, hg, HEAD_DIM ** -0.5) for g in groups], k_ref, kvt_ref,
                    [(DSA_KV_HEADS + g) * HEAD_DIM for g in groups], [cap_fn] * DSA_KV_HEADS, 0, nk, kc, hg)
    for g in groups:
        _store_heads(o_ref, o_ts[g], g, hg, tq)


def _dsa_prompt(f, bx, t_len, tq, kc):
    topk = min(DSA_TOPK_MAX, t_len // 4)
    nq = t_len // tq
    hg = DSA_HEADS // DSA_KV_HEADS
    row = lambda w: pl.BlockSpec((tq, w), lambda b, i: (b * nq + i, 0))
    qt = lambda rows: pl.BlockSpec((None, rows, tq), lambda b, i: (b, 0, i))
    keys = lambda w: pl.BlockSpec((t_len, w), lambda b, i: (b, 0))
    per_b = lambda rows: pl.BlockSpec((None, rows, t_len), lambda b, i: (b, 0, 0))
    return pl.pallas_call(
        functools.partial(_dsa_prompt_kernel, tq=tq, kc=kc, topk=topk),
        grid=(bx, nq),
        in_specs=[qt(W_IQ), qt(SMALL_ROWS), qt(W_DQ), keys(2 * LANES), keys(DSA_KV_HEADS * HEAD_DIM),
                  per_b(W_DKV)],
        out_specs=row(W_DQ),
        out_shape=jax.ShapeDtypeStruct((bx * t_len, W_DQ), F32),
        scratch_shapes=[pltpu.VMEM((t_len, tq), F32)],
        compiler_params=pltpu.CompilerParams(dimension_semantics=('arbitrary', 'arbitrary'),
                                             vmem_limit_bytes=VMEM_LIMIT),
        name='dsa_prompt',
    )(f['idx_q_t'], f['small_t'], f['dsa_q_t'], f['idx3'], f['dsa_k'], f['dsa_t'])


GROUP_TOKENS = CMP_STRIDE
N_HALVES = CMP_BLOCK // CMP_STRIDE
CMP_HID = 2 * HEAD_DIM


def _prep_compress_weights(cmp_pe, cmp_w1, cmp_w2):
    g_n = NSA_KV_HEADS
    eye = jnp.eye(g_n, dtype=F32)
    out = []
    for s in range(2):
        w1 = cmp_w1[s].reshape(N_HALVES, GROUP_TOKENS, HEAD_DIM, CMP_HID)
        w1e = jnp.einsum('ardj,gh->arghdj', w1, eye)
        w1e = w1e.transpose(0, 1, 2, 4, 3, 5).reshape(N_HALVES, GROUP_TOKENS * g_n * HEAD_DIM, g_n * CMP_HID)
        pe = cmp_pe[s].reshape(N_HALVES, GROUP_TOKENS, 1, HEAD_DIM)
        pe = jnp.broadcast_to(pe, (N_HALVES, GROUP_TOKENS, g_n, HEAD_DIM)).reshape(N_HALVES, 1, -1)
        w2e = jnp.einsum('jd,gh->gjhd', cmp_w2[s], eye).reshape(g_n * CMP_HID, g_n * HEAD_DIM)
        out.append((w1e, pe, w2e))
    (w1k, pek, w2k), (w1v, pev, w2v) = out
    w1k3 = jnp.stack([_split3_rows(w1k[a]) for a in range(N_HALVES)])
    return (w1k3, pek, _split3_rows(w2k), w1v.astype(BF16), pev, w2v.astype(BF16))


def _compress_slot(x_ref, m, pe_ref, w1_ref, w2_ref, shift_scr, precise):
    xr = jnp.concatenate([x_ref[pl.ds(r, m, stride=GROUP_TOKENS), :] for r in range(GROUP_TOKENS)], axis=1)
    halves = []
    for a in range(N_HALVES):
        xa = xr + pe_ref[a]
        lhs = _split3_cols(xa) if precise else xa.astype(BF16)
        halves.append(jnp.dot(lhs, w1_ref[a], preferred_element_type=F32))
    shift_scr[0:m, :] = halves[1]
    shift_scr[m:m + 8, :] = jnp.zeros((8, shift_scr.shape[1]), F32)
    hid = _silu(halves[0] + shift_scr[pl.ds(1, m), :])
    lhs = _split3_cols(hid) if precise else hid.astype(BF16)
    return jnp.dot(lhs, w2_ref[...], preferred_element_type=F32)


def _compress_kernel(xk_ref, xv_ref, w1k_ref, pek_ref, w2k_ref, w1v_ref, pev_ref, w2v_ref,
                     ck_ref, cv_ref, shift_scr, *, m):
    n_pad = ck_ref.shape[0]
    if n_pad > m:
        ck_ref[m:n_pad, :] = jnp.zeros((n_pad - m, ck_ref.shape[1]), F32)
        cv_ref[m:n_pad, :] = jnp.zeros((n_pad - m, cv_ref.shape[1]), F32)
    ck_ref[0:m, :] = _compress_slot(xk_ref, m, pek_ref, w1k_ref, w2k_ref, shift_scr, True)
    cv_ref[0:m, :] = _compress_slot(xv_ref, m, pev_ref, w1v_ref, w2v_ref, shift_scr, False)


def _compress_prompt(cmp_k, cmp_v, cw, bx, t_len):
    m = t_len // GROUP_TOKENS
    n_pad = -(-m // LANES) * LANES
    kvw = NSA_KV_HEADS * HEAD_DIM
    const = lambda a: pl.BlockSpec(a.shape, lambda b: (0,) * a.ndim, pipeline_mode=pl.Buffered(1))
    x_spec = pl.BlockSpec((t_len, kvw), lambda b: (b, 0))
    o_spec = pl.BlockSpec((None, n_pad, kvw), lambda b: (b, 0, 0))
    return pl.pallas_call(
        functools.partial(_compress_kernel, m=m),
        grid=(bx,),
        in_specs=[x_spec, x_spec] + [const(a) for a in cw],
        out_specs=[o_spec, o_spec],
        out_shape=[jax.ShapeDtypeStruct((bx, n_pad, kvw), F32)] * 2,
        scratch_shapes=[pltpu.VMEM((m + 8, NSA_KV_HEADS * CMP_HID), F32)],
        compiler_params=pltpu.CompilerParams(dimension_semantics=('arbitrary',), vmem_limit_bytes=VMEM_LIMIT),
        name='compress_prompt',
    )(cmp_k, cmp_v, *cw)


SEL_SHIFT = 6


def _stack_heads(x, g, hg, scale):
    return jnp.concatenate(
        [x[:, (g * hg + j) * HEAD_DIM:(g * hg + j + 1) * HEAD_DIM] for j in range(hg)], axis=0) * scale


def _cmp_branch(q_rows, ck_g, cv_g, qpos_rep, hg, tq):
    n_pad = ck_g.shape[0]
    nt = (((1,), (1,)), ((), ()))
    ck_hi = ck_g.astype(BF16)
    ck_lo = (ck_g - ck_hi.astype(F32)).astype(BF16)
    ck3 = jnp.concatenate([ck_hi, ck_lo, ck_hi], axis=1)
    s = lax.dot_general(_split3_cols(q_rows), ck3, nt, preferred_element_type=F32)
    cmp_end = lax.broadcasted_iota(jnp.int32, (1, n_pad), 1) * CMP_STRIDE + (CMP_BLOCK - 1)
    vis = cmp_end <= qpos_rep
    s = jnp.where(vis, s, NEG)
    p = jnp.where(vis, jnp.exp(s - jnp.max(s, axis=1, keepdims=True)), 0.0)
    p = p / jnp.maximum(jnp.sum(p, axis=1, keepdims=True), 1e-30)
    o_cmp = jnp.dot(p.astype(BF16), cv_g.astype(BF16), preferred_element_type=F32)
    p_sum = p[0:tq]
    for j in range(1, hg):
        p_sum = p_sum + p[j * tq:(j + 1) * tq]
    return o_cmp, p_sum


def _select_blocks(p_sum, qpos, n_slc, n_sel):
    tq, n_pad = p_sum.shape
    n_i = lax.broadcasted_iota(jnp.int32, (n_pad, LANES), 0) * CMP_STRIDE
    j_i = lax.broadcasted_iota(jnp.int32, (n_pad, LANES), 1)
    overlap = jnp.logical_and(n_i < (j_i + 1) * SEL_BLOCK, n_i + CMP_BLOCK > j_i * SEL_BLOCK)
    overlap = jnp.where(jnp.logical_and(overlap, j_i < n_slc), 1.0, 0.0).astype(BF16)
    p_hi = p_sum.astype(BF16)
    r1 = p_sum - p_hi.astype(F32)
    p_mid = r1.astype(BF16)
    p_lo = (r1 - p_mid.astype(F32)).astype(BF16)
    imp = (jnp.dot(p_hi, overlap, preferred_element_type=F32)
           + jnp.dot(p_mid, overlap, preferred_element_type=F32)
           + jnp.dot(p_lo, overlap, preferred_element_type=F32))
    jb = lax.broadcasted_iota(jnp.int32, (1, LANES), 1)
    cur = qpos >> SEL_SHIFT
    forced = jnp.logical_or(jb == 0, jnp.logical_or(jb == cur, jb == cur - 1))
    imp = jnp.where(forced, FORCED_SCORE, imp)
    imp = jnp.where(jb > cur, NEG, imp)
    rank = jnp.zeros((tq, LANES), F32)
    for i in range(n_slc):
        vi = imp[:, i:i + 1]
        ahead = jnp.logical_or(vi > imp, jnp.logical_and(vi == imp, jb > i))
        rank = rank + jnp.where(ahead, 1.0, 0.0)
    return jnp.where(jnp.logical_and(rank < float(n_sel), imp > 0.5 * NEG), 1.0, 0.0)


def _select_blocks_t(p_sum, qpos_t, n_slc, n_sel):
    tq, n_pad = p_sum.shape
    nb = -(-n_slc // 8) * 8
    j_i = lax.broadcasted_iota(jnp.int32, (nb, n_pad), 0)
    n_i = lax.broadcasted_iota(jnp.int32, (nb, n_pad), 1) * CMP_STRIDE
    overlap = jnp.logical_and(n_i < (j_i + 1) * SEL_BLOCK, n_i + CMP_BLOCK > j_i * SEL_BLOCK)
    overlap = jnp.where(jnp.logical_and(overlap, j_i < n_slc), 1.0, 0.0).astype(BF16)
    p_t = p_sum.T
    p_hi = p_t.astype(BF16)
    r1 = p_t - p_hi.astype(F32)
    p_mid = r1.astype(BF16)
    p_lo = (r1 - p_mid.astype(F32)).astype(BF16)
    imp = (jnp.dot(overlap, p_hi, preferred_element_type=F32)
           + jnp.dot(overlap, p_mid, preferred_element_type=F32)
           + jnp.dot(overlap, p_lo, preferred_element_type=F32))
    jb = lax.broadcasted_iota(jnp.int32, (nb, 1), 0)
    cur = qpos_t >> SEL_SHIFT
    forced = jnp.logical_or(jb == 0, jnp.logical_or(jb == cur, jb == cur - 1))
    imp = jnp.where(forced, FORCED_SCORE, imp)
    imp = jnp.where(jnp.logical_or(jb > cur, jb >= n_slc), NEG, imp)
    rank = jnp.zeros((nb, tq), F32)
    for i in range(n_slc):
        vi = imp[i:i + 1, :]
        ahead = jnp.logical_or(vi > imp, jnp.logical_and(vi == imp, jb > i))
        rank = rank + jnp.where(ahead, 1.0, 0.0)
    return jnp.where(jnp.logical_and(rank < float(n_sel), imp > 0.5 * NEG), 1.0, 0.0)


def _nsa_prompt_kernel(q_ref, qrt_ref, smallt_ref, ck_ref, cv_ref, slck_ref, wink_ref, slct_ref, wint_ref,
                       o_ref, capt_scr, *, tq, kc, kcw, n_slc, n_sel):
    qs = pl.program_id(1) * tq
    nk = (qs + tq + kc - 1) // kc
    hg = NSA_HEADS // NSA_KV_HEADS
    kvw = NSA_KV_HEADS * HEAD_DIM
    scale = HEAD_DIM ** -0.5
    qpos = qs + lax.broadcasted_iota(jnp.int32, (tq, 1), 0)
    qpos_t = qs + lax.broadcasted_iota(jnp.int32, (1, tq), 1)
    qpos_rep = jnp.concatenate([qpos] * hg, axis=0)
    q = q_ref[...]
    qrt = qrt_ref[...]
    gates_t = _sigmoid(smallt_ref[...])
    ck = ck_ref[...]
    cv = cv_ref[...]
    nb = -(-n_slc // 8) * 8
    blk_l = lax.broadcasted_iota(jnp.int32, (1, nb), 1)

    def win_cap(k0):
        dist = qpos_t - (k0 + lax.broadcasted_iota(jnp.int32, (kcw, 1), 0))
        return jnp.where(jnp.logical_and(dist >= 0, dist < WINDOW), POS_INF, NEG)

    cw0 = jnp.maximum(qs - WINDOW, 0) // kcw
    cw1 = (qs + tq + kcw - 1) // kcw

    groups = range(NSA_KV_HEADS)
    o_cmps, sel_ts = [], []
    for g in groups:
        o_cmp, p_sum = _cmp_branch(_stack_heads(q, g, hg, scale), ck[:, g * HEAD_DIM:(g + 1) * HEAD_DIM],
                                   cv[:, g * HEAD_DIM:(g + 1) * HEAD_DIM], qpos_rep, hg, tq)
        o_cmps.append(o_cmp)
        sel_ts.append(_select_blocks_t(p_sum, qpos_t, n_slc, n_sel).astype(BF16))

    def cap_body(c, _):
        k0 = pl.multiple_of(c * kc, kc)
        kpos = k0 + lax.broadcasted_iota(jnp.int32, (kc, 1), 0)
        expand = jnp.where((kpos >> SEL_SHIFT) == blk_l, 1.0, 0.0).astype(BF16)
        causal = kpos <= qpos_t
        for g in groups:
            sel_k = jnp.dot(expand, sel_ts[g], preferred_element_type=F32)
            capt_scr[g, pl.ds(k0, kc), :] = jnp.where(jnp.logical_and(sel_k > 0.5, causal), POS_INF, NEG)
        return 0

    lax.fori_loop(0, nk, cap_body, 0)
    q_pads = [_q_pad(qrt, g, hg, scale) for g in groups]
    vrows = [kvw + g * HEAD_DIM for g in groups]
    o_slcs = _flash_t(q_pads, slck_ref, slct_ref, vrows,
                      [functools.partial(lambda g, k0: capt_scr[g, pl.ds(k0, kc), :], g) for g in groups],
                      0, nk, kc, hg)
    o_wins = _flash_t(q_pads, wink_ref, wint_ref, vrows, [win_cap] * NSA_KV_HEADS, cw0, cw1, kcw, hg)
    for g in groups:
        for j in range(hg):
            h = g * hg + j
            r = IDX_HEADS + 3 * h
            cols = slice(j * tq, (j + 1) * tq)
            o_t = (gates_t[r:r + 1, :] * o_cmps[g][cols].T + gates_t[r + 1:r + 2, :] * o_slcs[g][:, cols]
                   + gates_t[r + 2:r + 3, :] * o_wins[g][:, cols])
            o_ref[:, h * HEAD_DIM:(h + 1) * HEAD_DIM] = o_t.T


def _nsa_prompt(f, ck, cv, bx, t_len, tq, kc, kcw):
    nq = t_len // tq
    hg = NSA_HEADS // NSA_KV_HEADS
    n_pad = ck.shape[1]
    n_slc = -(-t_len // SEL_BLOCK)
    kvw = NSA_KV_HEADS * HEAD_DIM
    row = lambda w: pl.BlockSpec((tq, w), lambda b, i: (b * nq + i, 0))
    cmp_spec = pl.BlockSpec((None, n_pad, kvw), lambda b, i: (b, 0, 0))
    keys = pl.BlockSpec((t_len, kvw), lambda b, i: (b, 0))
    return pl.pallas_call(
        functools.partial(_nsa_prompt_kernel, tq=tq, kc=kc, kcw=kcw, n_slc=n_slc, n_sel=min(N_SEL, n_slc)),
        grid=(bx, nq),
        in_specs=[row(W_NQ), pl.BlockSpec((None, W_NQ, tq), lambda b, i: (b, 0, i)),
                  pl.BlockSpec((None, SMALL_ROWS, tq), lambda b, i: (b, 0, i)),
                  cmp_spec, cmp_spec, keys, keys,
                  pl.BlockSpec((None, 2 * kvw, t_len), lambda b, i: (b, 1, 0)),
                  pl.BlockSpec((None, 2 * kvw, t_len), lambda b, i: (b, 0, 0))],
        out_specs=row(W_NQ),
        out_shape=jax.ShapeDtypeStruct((bx * t_len, W_NQ), F32),
        scratch_shapes=[pltpu.VMEM((NSA_KV_HEADS, t_len, tq), F32)],
        compiler_params=pltpu.CompilerParams(dimension_semantics=('arbitrary', 'arbitrary'),
                                             vmem_limit_bytes=VMEM_LIMIT),
        name='nsa_prompt',
    )(f['nsa_q'], f['nsa_q_rot_t'], f['small_t'], ck, cv, f['slc_k'], f['win_k'], f['nsa_t'], f['win_t'])


DEC_ROWS = 8


def _page_copies(cache_hbm, pt_ref, b, n_pages, page, buf, sem, slot):
    return [pltpu.make_async_copy(cache_hbm.at[pt_ref[b * n_pages + j]],
                                  buf.at[slot, :, pl.ds(j * page, page)], sem.at[slot])
            for j in range(n_pages)]


def _gather_step(caches, pt_ref, n_pages, page, bufs, sems):
    b = pl.program_id(0)
    nb = pl.num_programs(0)
    slot = b % 2

    @pl.when(b == 0)
    def _():
        for cache, buf, sem in zip(caches, bufs, sems):
            for cp in _page_copies(cache, pt_ref, 0, n_pages, page, buf, sem, 0):
                cp.start()

    @pl.when(b + 1 < nb)
    def _():
        for cache, buf, sem in zip(caches, bufs, sems):
            for cp in _page_copies(cache, pt_ref, b + 1, n_pages, page, buf, sem, 1 - slot):
                cp.start()

    for cache, buf, sem in zip(caches, bufs, sems):
        for cp in _page_copies(cache, pt_ref, b, n_pages, page, buf, sem, slot):
            cp.wait()
    return slot


def _attend_full(q_rows, kt, vt, cap, n_rep):
    nt = (((1,), (1,)), ((), ()))
    s = jnp.dot(q_rows, kt.astype(BF16), preferred_element_type=F32)
    s = jnp.minimum(s, jnp.concatenate([cap] * n_rep, axis=0))
    p = jnp.exp(s - jnp.max(s, axis=1, keepdims=True))
    den = jnp.maximum(jnp.sum(p, axis=1, keepdims=True), 1e-30)
    return lax.dot_general(p.astype(BF16), vt.astype(BF16), nt, preferred_element_type=F32) / den


def _decode_nsa_kernel(pt_ref, q_ref, qr_ref, small_ref, cache_hbm, new_ref, wst_ref, wnew_ref,
                       w1k_ref, pek_ref, w2k_ref, w1v_ref, pev_ref, w2v_ref, o_ref,
                       buf, sem, xk_scr, xv_scr, shift_scr, *, past, t_dec, n_pages, page):
    slot = _gather_step([cache_hbm], pt_ref, n_pages, page, [buf], [sem])
    rows = DEC_ROWS
    hg = NSA_HEADS // NSA_KV_HEADS
    kvw = NSA_KV_HEADS * HEAD_DIM
    scale = HEAD_DIM ** -0.5
    n_keys = buf.shape[2]
    buf[slot, :, pl.ds(past, LANES)] = new_ref[...]
    qpos = past + jnp.minimum(lax.broadcasted_iota(jnp.int32, (rows, 1), 0), t_dec - 1)
    qpos_rep = jnp.concatenate([qpos] * hg, axis=0)

    for j in range(n_pages):
        xk_scr[j * page:(j + 1) * page, :] = buf[slot, 0:kvw, j * page:(j + 1) * page].T
        xv_scr[j * page:(j + 1) * page, :] = buf[slot, kvw:2 * kvw, j * page:(j + 1) * page].T
    m = past // GROUP_TOKENS
    ck = _compress_slot(xk_scr, m, pek_ref, w1k_ref, w2k_ref, shift_scr, True)
    cv = _compress_slot(xv_scr, m, pev_ref, w1v_ref, w2v_ref, shift_scr, False)

    q = q_ref[...]
    qr = qr_ref[...]
    gates = _sigmoid(small_ref[...])
    n_slc = -(-(past + t_dec) // SEL_BLOCK)
    kpos = lax.broadcasted_iota(jnp.int32, (1, n_keys), 1)
    blk_i = lax.broadcasted_iota(jnp.int32, (LANES, 1), 0)
    expand = jnp.where((kpos >> SEL_SHIFT) == blk_i, 1.0, 0.0).astype(BF16)
    wb = wst_ref.shape[1]
    wpos = past - wb + lax.broadcasted_iota(jnp.int32, (1, wb + LANES), 1)
    wdist = qpos - wpos
    win_cap = jnp.where(jnp.logical_and(wdist >= 0, wdist < WINDOW), POS_INF, NEG)

    for g in range(NSA_KV_HEADS):
        gs = slice(g * HEAD_DIM, (g + 1) * HEAD_DIM)
        o_cmp, p_sum = _cmp_branch(_stack_heads(q, g, hg, scale), ck[:, gs], cv[:, gs], qpos_rep, hg, rows)
        sel = _select_blocks(p_sum, qpos, n_slc, min(N_SEL, n_slc)).astype(BF16)
        sel_k = jnp.dot(sel, expand, preferred_element_type=F32)
        slc_cap = jnp.where(jnp.logical_and(sel_k > 0.5, kpos <= qpos), POS_INF, NEG)
        qr_rows = _stack_heads(qr, g, hg, scale).astype(BF16)
        o_slc = _attend_full(qr_rows, buf[slot, 2 * kvw + g * HEAD_DIM:2 * kvw + (g + 1) * HEAD_DIM, :],
                             buf[slot, 3 * kvw + g * HEAD_DIM:3 * kvw + (g + 1) * HEAD_DIM, :], slc_cap, hg)
        wk = jnp.concatenate([wst_ref[g * HEAD_DIM:(g + 1) * HEAD_DIM, :],
                              wnew_ref[g * HEAD_DIM:(g + 1) * HEAD_DIM, :]], axis=1)
        wv = jnp.concatenate([wst_ref[kvw + g * HEAD_DIM:kvw + (g + 1) * HEAD_DIM, :],
                              wnew_ref[kvw + g * HEAD_DIM:kvw + (g + 1) * HEAD_DIM, :]], axis=1)
        o_win = _attend_full(qr_rows, wk, wv, win_cap, hg)
        for j in range(hg):
            h = g * hg + j
            lane = IDX_HEADS + 3 * h
            rs = slice(j * rows, (j + 1) * rows)
            o_ref[:, h * HEAD_DIM:(h + 1) * HEAD_DIM] = (gates[:, lane:lane + 1] * o_cmp[rs]
                                                         + gates[:, lane + 1:lane + 2] * o_slc[rs]
                                                         + gates[:, lane + 2:lane + 3] * o_win[rs])


def _decode_dsa_kernel(pt_ref, iq_ref, small_ref, dq_ref, idx_hbm, kv_hbm, inew_ref, kvnew_ref, o_ref,
                       ibuf, kbuf, isem, ksem, s_scr, *, past, t_dec, n_pages, page, topk):
    slot = _gather_step([idx_hbm, kv_hbm], pt_ref, n_pages, page, [ibuf, kbuf], [isem, ksem])
    rows = DEC_ROWS
    hg = DSA_HEADS // DSA_KV_HEADS
    scale = HEAD_DIM ** -0.5
    n_keys = ibuf.shape[2]
    kc = LANES
    nk = n_keys // kc
    ibuf[slot, :, pl.ds(past, LANES)] = inew_ref[...]
    kbuf[slot, :, pl.ds(past, LANES)] = kvnew_ref[...]
    qpos = past + jnp.minimum(lax.broadcasted_iota(jnp.int32, (rows, 1), 0), t_dec - 1)

    ik = ibuf[slot]
    ik_hi = ik.astype(BF16)
    ik_lo = (ik - ik_hi.astype(F32)).astype(BF16)
    k3 = jnp.concatenate([ik_hi, ik_lo, ik_hi], axis=0)
    iq = iq_ref[...]
    w = small_ref[...]
    sc = jnp.zeros((rows, n_keys), F32)
    for h in range(IDX_HEADS):
        d = jnp.dot(_split3_cols(iq[:, h * IDX_DIM:(h + 1) * IDX_DIM]), k3, preferred_element_type=F32)
        sc = sc + jnp.maximum(d, 0.0) * w[:, h:h + 1]
    sc = sc * ((IDX_DIM ** -0.5) * (IDX_HEADS ** -0.5))
    kpos = lax.broadcasted_iota(jnp.int32, (1, n_keys), 1)
    real_row = lax.broadcasted_iota(jnp.int32, (rows, 1), 0) < t_dec
    ok = jnp.logical_and(kpos <= qpos, jnp.logical_or(real_row, kpos == 0))
    s_scr[...] = jnp.where(ok, sc, -POS_INF)
    mn = jnp.min(jnp.where(ok, sc, POS_INF), axis=1, keepdims=True)
    mx = jnp.max(jnp.where(ok, sc, NEG), axis=1, keepdims=True)
    n_valid = jnp.where(real_row, qpos + 1, 1).astype(F32)
    t, cnt = _topk_threshold(s_scr, mn, mx, n_valid, topk, nk, kc)
    _write_topk_cap(s_scr, t, cnt, topk, nk, kc)
    cap = s_scr[...]

    dq = dq_ref[...]
    for g in range(DSA_KV_HEADS):
        q_rows = _stack_heads(dq, g, hg, scale).astype(BF16)
        o = _attend_full(q_rows, kbuf[slot, g * HEAD_DIM:(g + 1) * HEAD_DIM, :],
                         kbuf[slot, (DSA_KV_HEADS + g) * HEAD_DIM:(DSA_KV_HEADS + g + 1) * HEAD_DIM, :], cap, hg)
        for j in range(hg):
            h = g * hg + j
            o_ref[:, h * HEAD_DIM:(h + 1) * HEAD_DIM] = o[j * rows:(j + 1) * rows]


def _pad_rows(a2d, bd, t_dec):
    return jnp.pad(a2d.reshape(bd, t_dec, -1), ((0, 0), (0, DEC_ROWS - t_dec), (0, 0)))


def _new_cols(a_t, bd, t_dec):
    f = a_t.shape[0]
    return jnp.pad(a_t.reshape(f, bd, t_dec).transpose(1, 0, 2), ((0, 0), (0, 0), (0, LANES - t_dec)))


def _feature_major_pages(pool):
    n_pool, page = pool.shape[:2]
    nd = pool.ndim
    return pool.transpose((0,) + tuple(range(2, nd)) + (1,)).reshape(n_pool, -1, page)


def _decode_nsa(fs, cache_nsa, win_state, page_table, cw, past, t_dec):
    bd, n_pages = page_table.shape
    pool = _feature_major_pages(cache_nsa)
    page = pool.shape[2]
    f = pool.shape[1]
    n_keys = past + LANES
    wst = _feature_major_pages(win_state)
    kvw = NSA_KV_HEADS * HEAD_DIM
    per_b = lambda *shape: pl.BlockSpec((None,) + shape, lambda b, pt: (b,) + (0,) * len(shape))
    const = lambda a: pl.BlockSpec(a.shape, lambda b, pt: (0,) * a.ndim, pipeline_mode=pl.Buffered(1))
    grid_spec = pltpu.PrefetchScalarGridSpec(
        num_scalar_prefetch=1,
        grid=(bd,),
        in_specs=[per_b(DEC_ROWS, W_NQ), per_b(DEC_ROWS, W_NQ), per_b(DEC_ROWS, LANES),
                  pl.BlockSpec(memory_space=pl.ANY), per_b(f, LANES),
                  per_b(2 * kvw, wst.shape[2]), per_b(2 * kvw, LANES)] + [const(a) for a in cw],
        out_specs=per_b(DEC_ROWS, W_NQ),
        scratch_shapes=[pltpu.VMEM((2, f, n_keys), F32), pltpu.SemaphoreType.DMA((2,)),
                        pltpu.VMEM((past, kvw), F32), pltpu.VMEM((past, kvw), F32),
                        pltpu.VMEM((past // GROUP_TOKENS + 8, NSA_KV_HEADS * CMP_HID), F32)])
    out = pl.pallas_call(
        functools.partial(_decode_nsa_kernel, past=past, t_dec=t_dec, n_pages=n_pages, page=page),
        grid_spec=grid_spec,
        out_shape=jax.ShapeDtypeStruct((bd, DEC_ROWS, W_NQ), F32),
        compiler_params=pltpu.CompilerParams(dimension_semantics=('arbitrary',), vmem_limit_bytes=VMEM_LIMIT),
        name='decode_nsa',
    )(page_table.reshape(-1), _pad_rows(fs['nsa_q'], bd, t_dec), _pad_rows(fs['nsa_q_rot'], bd, t_dec),
      _pad_rows(fs['small'], bd, t_dec), pool, _new_cols(fs['nsa_t'], bd, t_dec), wst,
      _new_cols(fs['win_t'], bd, t_dec), *cw)
    return out[:, :t_dec].reshape(bd * t_dec, W_NQ)


def _decode_dsa(fs, cache_dsa, cache_idx, page_table, past, t_dec):
    bd, n_pages = page_table.shape
    kpool = _feature_major_pages(cache_dsa)
    ipool = _feature_major_pages(cache_idx)
    page = kpool.shape[2]
    n_keys = past + LANES
    topk = min(DSA_TOPK_MAX, (past + t_dec) // 4)
    per_b = lambda *shape: pl.BlockSpec((None,) + shape, lambda b, pt: (b,) + (0,) * len(shape))
    grid_spec = pltpu.PrefetchScalarGridSpec(
        num_scalar_prefetch=1,
        grid=(bd,),
        in_specs=[per_b(DEC_ROWS, W_IQ), per_b(DEC_ROWS, LANES), per_b(DEC_ROWS, W_DQ),
                  pl.BlockSpec(memory_space=pl.ANY), pl.BlockSpec(memory_space=pl.ANY),
                  per_b(IDX_DIM, LANES), per_b(W_DKV, LANES)],
        out_specs=per_b(DEC_ROWS, W_DQ),
        scratch_shapes=[pltpu.VMEM((2, IDX_DIM, n_keys), F32), pltpu.VMEM((2, W_DKV, n_keys), F32),
                        pltpu.SemaphoreType.DMA((2,)), pltpu.SemaphoreType.DMA((2,)),
                        pltpu.VMEM((DEC_ROWS, n_keys), F32)])
    out = pl.pallas_call(
        functools.partial(_decode_dsa_kernel, past=past, t_dec=t_dec, n_pages=n_pages, page=page, topk=topk),
        grid_spec=grid_spec,
        out_shape=jax.ShapeDtypeStruct((bd, DEC_ROWS, W_DQ), F32),
        compiler_params=pltpu.CompilerParams(dimension_semantics=('arbitrary',), vmem_limit_bytes=VMEM_LIMIT),
        name='decode_dsa',
    )(page_table.reshape(-1), _pad_rows(fs['idx_q'], bd, t_dec), _pad_rows(fs['small'], bd, t_dec),
      _pad_rows(fs['dsa_q'], bd, t_dec), ipool, kpool, _new_cols(fs['idx_t'], bd, t_dec),
      _new_cols(fs['dsa_t'], bd, t_dec))
    return out[:, :t_dec].reshape(bd * t_dec, W_DQ)


def _masked_softmax(s, mask):
    s = jnp.where(mask, s, NEG)
    p = jnp.where(mask, jnp.exp(s - jnp.max(s, axis=-1, keepdims=True)), 0.0)
    return p / jnp.maximum(jnp.sum(p, axis=-1, keepdims=True), 1e-30)


def _nsa_context(nsa_all, cmp_pe, cmp_w1, cmp_w2):
    Bx, L = nsa_all.shape[:2]
    G = NSA_KV_HEADS
    n_cmp = (L - CMP_BLOCK) // CMP_STRIDE + 1
    starts = np.arange(n_cmp, dtype=np.int32) * CMP_STRIDE
    blk_idx = starts[:, None] + np.arange(CMP_BLOCK, dtype=np.int32)[None, :]

    def compress(rows, pe, w1, w2):
        blk = rows[:, blk_idx] + pe[None, None, :, None, :]
        flat = jnp.swapaxes(blk, 2, 3).reshape(Bx, n_cmp, G, CMP_BLOCK * HEAD_DIM)
        hp = jnp.dot(flat, w1, precision=lax.Precision.HIGHEST)
        return jnp.dot(jax.nn.silu(hp), w2, precision=lax.Precision.HIGHEST)

    ck = compress(nsa_all[:, :, 0], cmp_pe[0], cmp_w1[0], cmp_w2[0])
    cv = compress(nsa_all[:, :, 1], cmp_pe[1], cmp_w1[1], cmp_w2[1])
    cmp_end = jnp.asarray(starts + CMP_BLOCK - 1)
    n_slc = -(-L // SEL_BLOCK)
    j = np.arange(n_slc, dtype=np.int32)[None, :]
    overlap = (starts[:, None] < (j + 1) * SEL_BLOCK) & (starts[:, None] + CMP_BLOCK > j * SEL_BLOCK)
    overlap = jnp.asarray(overlap, dtype=F32)
    pad = n_slc * SEL_BLOCK - L
    slc = jnp.pad(nsa_all[:, :, 2:4], ((0, 0), (0, pad), (0, 0), (0, 0), (0, 0)))
    slc = slc.reshape(Bx, n_slc, SEL_BLOCK, 2, G, HEAD_DIM).transpose(3, 0, 4, 1, 2, 5)
    return (ck, cv, cmp_end, overlap, slc[0], slc[1], min(N_SEL, n_slc))


def _nsa_attend(q, q_rot, gate, qpos, ctx, win_k, win_v, win_pos):
    hp = lax.Precision.HIGHEST
    ck, cv, cmp_end, overlap, slc_kb, slc_vb, n_sel = ctx
    Bx, Tq = q.shape[:2]
    G, Hg = NSA_KV_HEADS, NSA_HEADS // NSA_KV_HEADS
    scale = HEAD_DIM ** -0.5
    qg = q.reshape(Bx, Tq, G, Hg, HEAD_DIM)
    qr = q_rot.reshape(Bx, Tq, G, Hg, HEAD_DIM)
    s = jnp.einsum('btghd,bngd->btghn', qg, ck, precision=hp).astype(F32) * scale
    vis = (cmp_end[None, :] <= qpos[:, None])[None, :, None, None, :]
    p_cmp = _masked_softmax(s, vis)
    o_cmp = jnp.einsum('btghn,bngd->btghd', p_cmp, cv)
    imp = jnp.einsum('btgn,nj->btgj', jnp.sum(p_cmp, axis=3), overlap, precision=hp)
    n_slc = slc_kb.shape[2]
    jb = jnp.arange(n_slc, dtype=jnp.int32)[None, :]
    cur = (qpos // SEL_BLOCK)[:, None]
    forced = (jb == 0) | (jb == cur) | (jb == cur - 1)
    imp = jnp.where(forced[None, :, None, :], FORCED_SCORE, imp)
    imp = jnp.where((jb > cur)[None, :, None, :], NEG, imp)
    top_val, top_idx = lax.top_k(imp, n_sel)
    blk_ok = top_val > 0.5 * NEG
    bi = jnp.arange(Bx)[:, None, None, None]
    gi = jnp.arange(G)[None, None, :, None]
    sk = slc_kb[bi, gi, top_idx].reshape(Bx, Tq, G, n_sel * SEL_BLOCK, HEAD_DIM)
    sv = slc_vb[bi, gi, top_idx].reshape(Bx, Tq, G, n_sel * SEL_BLOCK, HEAD_DIM)
    kpos = top_idx[..., None] * SEL_BLOCK + jnp.arange(SEL_BLOCK, dtype=jnp.int32)
    ok = (blk_ok[..., None] & (kpos <= qpos[None, :, None, None, None])).reshape(Bx, Tq, G, 1, n_sel * SEL_BLOCK)
    s = jnp.einsum('btghd,btgmd->btghm', qr, sk).astype(F32) * scale
    o_slc = jnp.einsum('btghm,btgmd->btghd', _masked_softmax(s, ok), sv)
    s = jnp.einsum('btghd,bwgd->btghw', qr, win_k).astype(F32) * scale
    dist = qpos[:, None] - win_pos[None, :]
    wok = ((dist >= 0) & (dist < WINDOW) & (win_pos[None, :] >= 0))[None, :, None, None, :]
    o_win = jnp.einsum('btghw,bwgd->btghd', _masked_softmax(s, wok), win_v)
    g = gate.reshape(Bx, Tq, G, Hg, 3)
    o = g[..., 0:1] * o_cmp + g[..., 1:2] * o_slc + g[..., 2:3] * o_win
    return o.reshape(Bx, Tq, NSA_HEADS * HEAD_DIM)


def _dsa_attend(q, idx_q, idx_w, qpos, k, v, idx_k, topk):
    hp = lax.Precision.HIGHEST
    Bx, Tq = q.shape[:2]
    L = k.shape[1]
    G, Hg = DSA_KV_HEADS, DSA_HEADS // DSA_KV_HEADS
    causal = jnp.arange(L, dtype=jnp.int32)[None, :] <= qpos[:, None]
    dots = jnp.einsum('bthd,bsd->bths', idx_q, idx_k, precision=hp).astype(F32) * (IDX_DIM ** -0.5)
    score = jnp.einsum('bths,bth->bts', jax.nn.relu(dots), idx_w, precision=hp) * (IDX_HEADS ** -0.5)
    score = jnp.where(causal[None], score, NEG)
    _, sel = lax.top_k(score, topk)
    bi = jnp.arange(Bx)[:, None, None]
    sk = k[bi, sel]
    sv = v[bi, sel]
    ok = (sel <= qpos[None, :, None])[:, :, None, None, :]
    qg = q.reshape(Bx, Tq, G, Hg, HEAD_DIM)
    s = jnp.einsum('btghd,btkgd->btghk', qg, sk).astype(F32) * (HEAD_DIM ** -0.5)
    o = jnp.einsum('btghk,btkgd->btghd', _masked_softmax(s, ok), sv)
    return o.reshape(Bx, Tq, DSA_HEADS * HEAD_DIM)


def _rows_from_t(a_t, bx, t, lead):
    return a_t.reshape(lead + (bx, t)).transpose((len(lead), len(lead) + 1) + tuple(range(len(lead))))


def _rows_from_bt(a_bt, lead):
    bx, _, t = a_bt.shape
    n = len(lead)
    return a_bt.reshape((bx,) + lead + (t,)).transpose((0, n + 1) + tuple(range(1, n + 1)))


def kernel(x_prompt, x_sample, cache_nsa_kv, cache_dsa_kv, cache_dsa_idx_k, state_nsa_win_kv, page_table,
           c_prompt, c_sample, norm_g, w_ada, b_ada, w_in, cmp_pe, cmp_w1, cmp_w2,
           w_up_nsa, w_up_dsa, w_out, final_g):
    depth = norm_g.shape[0]
    assert depth == 1
    B, T, D = x_prompt.shape
    Bd, Td, _ = x_sample.shape
    l = 0
    past = page_table.shape[1] * cache_nsa_kv.shape[2]

    c_all = jnp.concatenate([c_prompt, c_sample], axis=0)
    pad = (-c_all.shape[0]) % 8
    c_all = jnp.pad(c_all, ((0, pad), (0, 0)))
    mod = _ada_mod(c_all, w_ada[l], b_ada[l])
    shift, scale, gate = mod[:, 0:D], mod[:, D:2 * D], mod[:, 2 * D:3 * D]

    weights = _prep_inproj_weights(w_in[l], D)

    tm = 256
    fp = _in_projection_prompt(x_prompt.reshape(B * T, D), norm_g[l], scale[:B], shift[:B], T,
                               _prep_inproj_weights_prompt(w_in[l], D), tm)
    nsa_rows = _rows_from_bt(fp['nsa_t'], (4, NSA_KV_HEADS, HEAD_DIM))
    wb_p = min(WINDOW, T)
    new_win_prompt = _rows_from_bt(fp['win_t'][:, :, T - wb_p:], (2, NSA_KV_HEADS, HEAD_DIM))
    dsa_rows = _rows_from_bt(fp['dsa_t'], (2, DSA_KV_HEADS, HEAD_DIM))
    idx_rows = _rows_from_bt(fp['idx_t'], (IDX_DIM,))
    cw = _prep_compress_weights(cmp_pe[l], cmp_w1[l], cmp_w2[l])
    ck, cv = _compress_prompt(fp['cmp_k'], fp['cmp_v'], cw, B, T)
    tq = 128
    kc = min(512, T)
    o_n = _nsa_prompt(fp, ck, cv, B, T, tq, kc, 128)
    o_d = _dsa_prompt(fp, B, T, tq, kc)
    y_prompt = _mixer_out(x_prompt.reshape(B * T, D), o_n, o_d, fp, gate[:B], T, final_g,
                          w_up_nsa[l], w_up_dsa[l], w_out[l], tm).reshape(B, T, D)

    Rs = Bd * Td
    tms = min(256, Rs)
    pos_s = past + jnp.arange(Td, dtype=jnp.int32)
    pos_rows = jnp.tile(pos_s, tms // Td)
    rep = lambda a: jnp.repeat(a[B:B + Bd], Td, axis=0)
    fs = _in_projection(x_sample.reshape(Rs, D), norm_g[l], rep(scale), rep(shift), 1, pos_rows, weights, tms, 1)
    for name in ('nsa_t', 'win_t', 'dsa_t', 'idx_t'):
        fs[name] = fs[name][0]
    s_nsa_rows = _rows_from_t(fs['nsa_t'], Bd, Td, (4, NSA_KV_HEADS, HEAD_DIM))
    s_win_rows = _rows_from_t(fs['win_t'], Bd, Td, (2, NSA_KV_HEADS, HEAD_DIM))
    s_dsa_rows = _rows_from_t(fs['dsa_t'], Bd, Td, (2, DSA_KV_HEADS, HEAD_DIM))
    s_idx_rows = _rows_from_t(fs['idx_t'], Bd, Td, (IDX_DIM,))

    win_state = state_nsa_win_kv[l]
    so_n = _decode_nsa(fs, cache_nsa_kv[l], win_state, page_table, cw, past, Td)
    so_d = _decode_dsa(fs, cache_dsa_kv[l], cache_dsa_idx_k[l], page_table, past, Td)
    y_sample = _mixer_out(x_sample.reshape(Rs, D), so_n, so_d, fs,
                          rep(gate), 1, final_g, w_up_nsa[l], w_up_dsa[l], w_out[l], tms).reshape(Bd, Td, D)
    new_win_sample = jnp.concatenate([win_state, s_win_rows], axis=1)[:, Td:]

    return (y_prompt, y_sample, nsa_rows[None], dsa_rows[None], idx_rows[None], new_win_prompt[None],
            s_nsa_rows[None], s_dsa_rows[None], s_idx_rows[None], new_win_sample[None])
```

```python
import functools

import numpy as np
import jax
import jax.numpy as jnp
from jax import lax
from jax.experimental import pallas as pl
from jax.experimental.pallas import tpu as pltpu

F32 = jnp.float32
BF16 = jnp.bfloat16

HEAD_DIM = 64
NSA_HEADS = 8
NSA_KV_HEADS = 2
DSA_HEADS = 8
DSA_KV_HEADS = 2
N_NSA_KV = 6
CMP_BLOCK = 32
CMP_STRIDE = 16
SEL_BLOCK = 64
N_SEL = 16
WINDOW = 512
IDX_HEADS = 4
IDX_DIM = 64
DSA_TOPK_MAX = 256
ROPE_THETA = 10000.0
EPS = 1e-6
NEG = -1e30
FORCED_SCORE = 1e4

LANES = 128
VMEM_LIMIT = 56 * 1024 * 1024

PROJ_ROWS = 256
Q_TILE = LANES
KEY_CHUNK = 512
WIN_CHUNK = 128

W_NQ = NSA_HEADS * HEAD_DIM
W_NKV = N_NSA_KV * NSA_KV_HEADS * HEAD_DIM
W_NG = NSA_HEADS * 3
W_DQ = DSA_HEADS * HEAD_DIM
W_DKV = 2 * DSA_KV_HEADS * HEAD_DIM
W_IQ = IDX_HEADS * IDX_DIM


def _cuts(d_model):
    widths = (W_NQ, W_NKV, W_NG, W_NQ, W_DQ, W_DKV, W_IQ, IDX_DIM, IDX_HEADS, W_DQ, 2 * d_model)
    c = np.concatenate([[0], np.cumsum(widths)])
    names = ('nq', 'nkv', 'ng', 'nz', 'dq', 'dkv', 'iq', 'ik', 'iw', 'dz', 'mg')
    return {n: (int(c[i]), int(c[i + 1])) for i, n in enumerate(names)}


def _split3_rows(w):
    wh = w.astype(BF16)
    wl = (w - wh.astype(F32)).astype(BF16)
    return jnp.concatenate([wh, wl, wh], axis=0)


def _split3_cols(x):
    xh = x.astype(BF16)
    xl = (x - xh.astype(F32)).astype(BF16)
    return jnp.concatenate([xh, xh, xl], axis=1)


def _silu(v):
    return v * (1.0 / (1.0 + jnp.exp(-v)))


def _sigmoid(v):
    return 1.0 / (1.0 + jnp.exp(-v))


def _ada_kernel(c_ref, w_ref, b_ref, o_ref):
    c3 = _split3_cols(_silu(c_ref[...]))
    o_ref[...] = jnp.dot(c3, w_ref[...], preferred_element_type=F32) + b_ref[...]


def _ada_mod(c_all, w_ada, b_ada):
    m, d = c_all.shape
    n = w_ada.shape[1]
    tn = 512
    w3 = _split3_rows(w_ada)
    return pl.pallas_call(
        _ada_kernel,
        grid=(n // tn,),
        in_specs=[pl.BlockSpec((m, d), lambda j: (0, 0)),
                  pl.BlockSpec((3 * d, tn), lambda j: (0, j)),
                  pl.BlockSpec((1, tn), lambda j: (0, j))],
        out_specs=pl.BlockSpec((m, tn), lambda j: (0, j)),
        out_shape=jax.ShapeDtypeStruct((m, n), F32),
        name='ada_mod',
    )(c_all, w3, b_ada.reshape(1, n))


def _rope_tf(v, cos, sin_signed):
    first_half = (lax.broadcasted_iota(jnp.int32, (1, LANES), 1) % HEAD_DIM) < (HEAD_DIM // 2)
    outs = []
    for c in range(v.shape[1] // LANES):
        vc = v[:, c * LANES:(c + 1) * LANES]
        sw = jnp.where(first_half, pltpu.roll(vc, LANES - HEAD_DIM // 2, 1), pltpu.roll(vc, HEAD_DIM // 2, 1))
        outs.append(vc * cos + sw * sin_signed)
    return jnp.concatenate(outs, axis=1) if len(outs) > 1 else outs[0]


def _rope_ft(v, cos_t, sin_t):
    half = HEAD_DIM // 2
    outs = []
    for hd in range(v.shape[0] // HEAD_DIM):
        x1 = v[hd * HEAD_DIM:hd * HEAD_DIM + half]
        x2 = v[hd * HEAD_DIM + half:(hd + 1) * HEAD_DIM]
        outs.append(x1 * cos_t - x2 * sin_t)
        outs.append(x2 * cos_t + x1 * sin_t)
    return jnp.concatenate(outs, axis=0)


def _inproj_kernel(x_ref, g_ref, sc_ref, sh_ref, cos_ref, sin_ref, cost_ref, sint_ref,
                   wp_ref, wa_ref, wt_ref, wti_ref,
                   q_ref, qrot_ref, iq_ref, small_ref, ck_ref,
                   cv_ref, snz_ref, sdz_ref, dq_ref, mg_ref,
                   nsat_ref, wint_ref, dsat_ref, idxt_ref, idx3_ref):
    x = x_ref[...]
    y = x * lax.rsqrt(jnp.mean(x * x, axis=-1, keepdims=True) + EPS) * g_ref[...]
    h = y * (1.0 + sc_ref[...]) + sh_ref[...]
    hh = h.astype(BF16)
    hl = (h - hh.astype(F32)).astype(BF16)
    h3 = jnp.concatenate([hh, hh, hl], axis=1)
    cos = cos_ref[...]
    sin = sin_ref[...]
    cos_t = cost_ref[...]
    sin_t = sint_ref[...]

    p = jnp.dot(h3, wp_ref[...], preferred_element_type=F32)
    q = p[:, 0:W_NQ]
    q_ref[...] = q
    qrot_ref[...] = _rope_tf(q, cos, sin)
    iq_ref[...] = _rope_tf(p[:, W_NQ:W_NQ + W_IQ], cos, sin)
    small_ref[...] = p[:, W_NQ + W_IQ:W_NQ + W_IQ + LANES]
    ck_ref[...] = p[:, W_NQ + W_IQ + LANES:W_NQ + W_IQ + 2 * LANES]

    a = jnp.dot(hh, wa_ref[...], preferred_element_type=F32)
    cv_ref[...] = a[:, 0:LANES]
    o = LANES
    snz_ref[...] = _silu(a[:, o:o + W_NQ])
    o += W_NQ
    sdz_ref[...] = _silu(a[:, o:o + W_DQ])
    o += W_DQ
    dq_ref[...] = _rope_tf(a[:, o:o + W_DQ], cos, sin)
    o += W_DQ
    mg_ref[...] = _sigmoid(a[:, o:])

    nt = (((1,), (1,)), ((), ()))
    t = lax.dot_general(wt_ref[...], hh, nt, preferred_element_type=F32)
    kv = 2 * HEAD_DIM
    nsat_ref[0:2 * kv, :] = t[0:2 * kv]
    nsat_ref[2 * kv:3 * kv, :] = _rope_ft(t[2 * kv:3 * kv], cos_t, sin_t)
    nsat_ref[3 * kv:4 * kv, :] = t[3 * kv:4 * kv]
    wint_ref[0:kv, :] = _rope_ft(t[4 * kv:5 * kv], cos_t, sin_t)
    wint_ref[kv:2 * kv, :] = t[5 * kv:6 * kv]
    dsat_ref[0:kv, :] = _rope_ft(t[6 * kv:7 * kv], cos_t, sin_t)
    dsat_ref[kv:2 * kv, :] = t[7 * kv:8 * kv]
    ti = lax.dot_general(wti_ref[...], h3, nt, preferred_element_type=F32)
    ti = _rope_ft(ti, cos_t, sin_t)
    idxt_ref[...] = ti
    ti_hi = ti.astype(BF16)
    ti_lo = (ti - ti_hi.astype(F32)).astype(BF16)
    idx3_ref[...] = jnp.concatenate([ti_hi, ti_lo, ti_hi], axis=0)


def _prep_inproj_weights(w_in, d_model):
    c = _cuts(d_model)
    sl = lambda name: w_in[:, c[name][0]:c[name][1]]
    nkv = sl('nkv')
    kv = NSA_KV_HEADS * HEAD_DIM
    small = jnp.concatenate([sl('iw'), sl('ng'), jnp.zeros((d_model, LANES - IDX_HEADS - W_NG), F32)], axis=1)
    wp = jnp.concatenate([sl('nq'), sl('iq'), small, nkv[:, 0:kv]], axis=1)
    wa = jnp.concatenate([nkv[:, kv:2 * kv], sl('nz'), sl('dz'), sl('dq'), sl('mg')], axis=1)
    wt = jnp.concatenate([nkv, sl('dkv')], axis=1).T
    wti = sl('ik')
    return _split3_rows(wp), wa.astype(BF16), wt.astype(BF16), _split3_rows(wti).T


def _rope_tables(pos):
    half = HEAD_DIM // 2
    inv = ROPE_THETA ** (-jnp.arange(half, dtype=F32) / half)
    ang = pos.astype(F32)[:, None] * inv[None, :]
    cos = jnp.cos(ang)
    sin = jnp.sin(ang)
    cos_tf = jnp.tile(cos, (1, LANES // half))
    sin_tf = jnp.tile(jnp.concatenate([-sin, sin], axis=1), (1, LANES // HEAD_DIM))
    return cos_tf, sin_tf, cos.T, sin.T


def _in_projection(x2d, norm_g, scale, shift, rows_per_mod, pos, weights, tm, nb):
    r, d = x2d.shape
    tpb = (r // nb) // tm
    wp, wa, wt, wti = weights
    n_tiles = r // tm
    cos_tf, sin_tf, cos_t, sin_t = _rope_tables(pos)
    p_tiles = pos.shape[0] // tm
    if rows_per_mod == 1:
        mod_spec = pl.BlockSpec((tm, d), lambda i: (i, 0))
        sc, sh = scale, shift
    else:
        per = rows_per_mod // tm
        mod_spec = pl.BlockSpec((None, 1, d), lambda i: (i // per, 0, 0))
        sc, sh = scale[:, None, :], shift[:, None, :]
    const = lambda shape: pl.BlockSpec(shape, lambda i: (0,) * len(shape), pipeline_mode=pl.Buffered(1))
    tf = lambda w: pl.BlockSpec((tm, w), lambda i: (i, 0))
    ft = lambda w: pl.BlockSpec((None, w, tm), lambda i: (i // tpb, 0, i % tpb))
    tf_widths = (W_NQ, W_NQ, W_IQ, LANES, LANES, LANES, W_NQ, W_DQ, W_DQ, 2 * d)
    ft_widths = (4 * LANES, 2 * LANES, 2 * LANES, IDX_DIM)
    outs = pl.pallas_call(
        _inproj_kernel,
        grid=(n_tiles,),
        in_specs=[pl.BlockSpec((tm, d), lambda i: (i, 0)),
                  const((1, d)), mod_spec, mod_spec,
                  pl.BlockSpec((tm, LANES), lambda i: (i % p_tiles, 0)),
                  pl.BlockSpec((tm, LANES), lambda i: (i % p_tiles, 0)),
                  pl.BlockSpec((HEAD_DIM // 2, tm), lambda i: (0, i % p_tiles)),
                  pl.BlockSpec((HEAD_DIM // 2, tm), lambda i: (0, i % p_tiles)),
                  const(wp.shape), const(wa.shape), const(wt.shape), const(wti.shape)],
        out_specs=[tf(w) for w in tf_widths] + [ft(w) for w in ft_widths] + [ft(3 * IDX_DIM)],
        out_shape=[jax.ShapeDtypeStruct((r, w), F32) for w in tf_widths]
        + [jax.ShapeDtypeStruct((nb, w, r // nb), F32) for w in ft_widths]
        + [jax.ShapeDtypeStruct((nb, 3 * IDX_DIM, r // nb), BF16)],
        compiler_params=pltpu.CompilerParams(dimension_semantics=('arbitrary',), vmem_limit_bytes=VMEM_LIMIT),
        name='in_projection',
    )(x2d, norm_g.reshape(1, d), sc, sh, cos_tf, sin_tf, cos_t, sin_t, wp, wa, wt, wti)
    names = ('nsa_q', 'nsa_q_rot', 'idx_q', 'small', 'cmp_k', 'cmp_v', 'silu_nz', 'silu_dz', 'dsa_q', 'merge',
             'nsa_t', 'win_t', 'dsa_t', 'idx_t', 'idx3_t')
    return dict(zip(names, outs))


SMALL_ROWS = 32


def _inproj_prompt_kernel(x_ref, g_ref, sc_ref, sh_ref, cos_ref, sin_ref, cost_ref, sint_ref,
                          wp_ref, wa_ref, wtp_ref, wt_ref,
                          q_ref, ck_ref, idx3_ref,
                          cv_ref, slck_ref, wink_ref, dsak_ref, snz_ref, sdz_ref, mg_ref,
                          iqt_ref, smallt_ref, qrt_ref, dqt_ref, nsat_ref, wint_ref, dsat_ref, idxt_ref):
    x = x_ref[...]
    y = x * lax.rsqrt(jnp.mean(x * x, axis=-1, keepdims=True) + EPS) * g_ref[...]
    h = y * (1.0 + sc_ref[...]) + sh_ref[...]
    hh = h.astype(BF16)
    hl = (h - hh.astype(F32)).astype(BF16)
    h3 = jnp.concatenate([hh, hh, hl], axis=1)
    cos = cos_ref[...]
    sin = sin_ref[...]
    cos_t = cost_ref[...]
    sin_t = sint_ref[...]
    nt = (((1,), (1,)), ((), ()))
    kv = NSA_KV_HEADS * HEAD_DIM

    p = jnp.dot(h3, wp_ref[...], preferred_element_type=F32)
    q_ref[...] = p[:, 0:W_NQ]
    ck_ref[...] = p[:, W_NQ:W_NQ + LANES]
    ik = _rope_tf(p[:, W_NQ + LANES:W_NQ + 2 * LANES], cos, sin)
    ik_hi = ik.astype(BF16)
    ik_lo = (ik - ik_hi.astype(F32)).astype(BF16)
    first = lax.broadcasted_iota(jnp.int32, (1, LANES), 1) < IDX_DIM
    idx3_ref[...] = jnp.concatenate([jnp.where(first, ik_hi, ik_lo),
                                     jnp.where(first, ik_hi, jnp.zeros_like(ik_hi))], axis=1)

    a = jnp.dot(hh, wa_ref[...], preferred_element_type=F32)
    cv_ref[...] = a[:, 0:kv]
    slck_ref[...] = _rope_tf(a[:, kv:2 * kv], cos, sin).astype(BF16)
    wink_ref[...] = _rope_tf(a[:, 2 * kv:3 * kv], cos, sin).astype(BF16)
    dsak_ref[...] = _rope_tf(a[:, 3 * kv:4 * kv], cos, sin).astype(BF16)
    o = 4 * kv
    snz_ref[...] = _silu(a[:, o:o + W_NQ])
    o += W_NQ
    sdz_ref[...] = _silu(a[:, o:o + W_DQ])
    o += W_DQ
    mg_ref[...] = _sigmoid(a[:, o:])

    tp = lax.dot_general(wtp_ref[...], h3, nt, preferred_element_type=F32)
    iqt_ref[...] = _rope_ft(tp[0:W_IQ], cos_t, sin_t)
    smallt_ref[...] = tp[W_IQ:W_IQ + SMALL_ROWS]

    t = lax.dot_general(wt_ref[...], hh, nt, preferred_element_type=F32)
    qrt_ref[...] = _rope_ft(t[0:W_NQ], cos_t, sin_t)
    o = W_NQ
    dqt_ref[...] = _rope_ft(t[o:o + W_DQ], cos_t, sin_t)
    o += W_DQ
    nsat_ref[0:2 * kv, :] = t[o:o + 2 * kv]
    nsat_ref[2 * kv:3 * kv, :] = _rope_ft(t[o + 2 * kv:o + 3 * kv], cos_t, sin_t)
    nsat_ref[3 * kv:4 * kv, :] = t[o + 3 * kv:o + 4 * kv]
    wint_ref[0:kv, :] = _rope_ft(t[o + 4 * kv:o + 5 * kv], cos_t, sin_t)
    wint_ref[kv:2 * kv, :] = t[o + 5 * kv:o + 6 * kv]
    dsat_ref[0:kv, :] = _rope_ft(t[o + 6 * kv:o + 7 * kv], cos_t, sin_t)
    dsat_ref[kv:2 * kv, :] = t[o + 7 * kv:o + 8 * kv]
    idxt_ref[...] = _rope_ft(t[o + 8 * kv:o + 8 * kv + IDX_DIM], cos_t, sin_t)


def _prep_inproj_weights_prompt(w_in, d_model):
    c = _cuts(d_model)
    sl = lambda name: w_in[:, c[name][0]:c[name][1]]
    nkv = sl('nkv')
    kv = NSA_KV_HEADS * HEAD_DIM
    zeros = lambda n: jnp.zeros((d_model, n), F32)
    small = jnp.concatenate([sl('iw'), sl('ng')], axis=1)
    wp = jnp.concatenate([sl('nq'), nkv[:, 0:kv], sl('ik'), sl('ik')], axis=1)
    wa = jnp.concatenate([nkv[:, kv:2 * kv], nkv[:, 2 * kv:3 * kv], nkv[:, 4 * kv:5 * kv], sl('dkv')[:, 0:kv],
                          sl('nz'), sl('dz'), sl('mg')], axis=1)
    wtp = jnp.concatenate([sl('iq'), small, zeros(SMALL_ROWS - small.shape[1])], axis=1)
    wt = jnp.concatenate([sl('nq'), sl('dq'), nkv, sl('dkv'), sl('ik')], axis=1)
    return _split3_rows(wp), wa.astype(BF16), _split3_rows(wtp).T, wt.T.astype(BF16)


def _in_projection_prompt(x2d, norm_g, scale, shift, t_len, weights, tm):
    r, d = x2d.shape
    nb = r // t_len
    tpb = t_len // tm
    wp, wa, wtp, wt = weights
    cos_tf, sin_tf, cos_t, sin_t = _rope_tables(jnp.arange(t_len, dtype=jnp.int32))
    mod_spec = pl.BlockSpec((None, 1, d), lambda i: (i // tpb, 0, 0))
    const = lambda shape: pl.BlockSpec(shape, lambda i: (0,) * len(shape), pipeline_mode=pl.Buffered(1))
    tf = lambda w: pl.BlockSpec((tm, w), lambda i: (i, 0))
    ft = lambda w: pl.BlockSpec((None, w, tm), lambda i: (i // tpb, 0, i % tpb))
    kv = NSA_KV_HEADS * HEAD_DIM
    tf_outs = (('nsa_q', W_NQ, F32), ('cmp_k', kv, F32), ('idx3', 2 * LANES, BF16),
               ('cmp_v', kv, F32), ('slc_k', kv, BF16), ('win_k', kv, BF16), ('dsa_k', kv, BF16),
               ('silu_nz', W_NQ, F32), ('silu_dz', W_DQ, F32), ('merge', 2 * d, F32))
    ft_outs = (('idx_q_t', W_IQ), ('small_t', SMALL_ROWS), ('nsa_q_rot_t', W_NQ), ('dsa_q_t', W_DQ),
               ('nsa_t', 4 * kv), ('win_t', 2 * kv), ('dsa_t', W_DKV), ('idx_t', IDX_DIM))
    outs = pl.pallas_call(
        _inproj_prompt_kernel,
        grid=(r // tm,),
        in_specs=[pl.BlockSpec((tm, d), lambda i: (i, 0)),
                  const((1, d)), mod_spec, mod_spec,
                  pl.BlockSpec((tm, LANES), lambda i: (i % tpb, 0)),
                  pl.BlockSpec((tm, LANES), lambda i: (i % tpb, 0)),
                  pl.BlockSpec((HEAD_DIM // 2, tm), lambda i: (0, i % tpb)),
                  pl.BlockSpec((HEAD_DIM // 2, tm), lambda i: (0, i % tpb)),
                  const(wp.shape), const(wa.shape), const(wtp.shape), const(wt.shape)],
        out_specs=[tf(w) for _, w, _ in tf_outs] + [ft(w) for _, w in ft_outs],
        out_shape=[jax.ShapeDtypeStruct((r, w), dt) for _, w, dt in tf_outs]
        + [jax.ShapeDtypeStruct((nb, w, t_len), F32) for _, w in ft_outs],
        compiler_params=pltpu.CompilerParams(dimension_semantics=('arbitrary',), vmem_limit_bytes=VMEM_LIMIT),
        name='in_projection_prompt',
    )(x2d, norm_g.reshape(1, d), scale[:, None, :], shift[:, None, :], cos_tf, sin_tf, cos_t, sin_t,
      wp, wa, wtp, wt)
    return dict(zip([n for n, _, _ in tf_outs] + [n for n, _ in ft_outs], outs))


def _mixer_out_kernel(x_ref, on_ref, od_ref, snz_ref, sdz_ref, mg_ref, ga_ref, fg_ref,
                      wun_ref, wud_ref, wo_ref, y_ref):
    d = x_ref.shape[1]
    u_n = jnp.dot((on_ref[...] * snz_ref[...]).astype(BF16), wun_ref[...], preferred_element_type=F32)
    u_d = jnp.dot((od_ref[...] * sdz_ref[...]).astype(BF16), wud_ref[...], preferred_element_type=F32)
    mg = mg_ref[...]
    mix = mg[:, 0:d] * u_n + mg[:, d:2 * d] * u_d
    z = x_ref[...] + ga_ref[...] * jnp.dot(mix.astype(BF16), wo_ref[...], preferred_element_type=F32)
    y_ref[...] = z * lax.rsqrt(jnp.mean(z * z, axis=-1, keepdims=True) + EPS) * fg_ref[...]


def _mixer_out(x2d, o_n, o_d, f, gate, rows_per_mod, final_g, w_up_nsa, w_up_dsa, w_out, tm):
    r, d = x2d.shape
    if rows_per_mod == 1:
        mod_spec = pl.BlockSpec((tm, d), lambda i: (i, 0))
        ga = gate
    else:
        per = rows_per_mod // tm
        mod_spec = pl.BlockSpec((None, 1, d), lambda i: (i // per, 0, 0))
        ga = gate[:, None, :]
    const = lambda shape: pl.BlockSpec(shape, lambda i: (0,) * len(shape), pipeline_mode=pl.Buffered(1))
    tf = lambda w: pl.BlockSpec((tm, w), lambda i: (i, 0))
    return pl.pallas_call(
        _mixer_out_kernel,
        grid=(r // tm,),
        in_specs=[tf(d), tf(W_NQ), tf(W_DQ), tf(W_NQ), tf(W_DQ), tf(2 * d), mod_spec, const((1, d)),
                  const(w_up_nsa.shape), const(w_up_dsa.shape), const(w_out.shape)],
        out_specs=tf(d),
        out_shape=jax.ShapeDtypeStruct((r, d), F32),
        compiler_params=pltpu.CompilerParams(dimension_semantics=('arbitrary',), vmem_limit_bytes=VMEM_LIMIT),
        name='mixer_out',
    )(x2d, o_n, o_d, f['silu_nz'], f['silu_dz'], f['merge'], ga, final_g.reshape(1, d),
      w_up_nsa.astype(BF16), w_up_dsa.astype(BF16), w_out.astype(BF16))


POS_INF = float('inf')


_MAG_BITS = 0x7FFFFFFF


def _ordered_key(f):
    b = lax.bitcast_convert_type(f, jnp.int32)
    return jnp.where(b < 0, b ^ _MAG_BITS, b)


def _from_ordered_key(key):
    return lax.bitcast_convert_type(jnp.where(key < 0, key ^ _MAG_BITS, key), F32)


VALUE_MID_ROUNDS = 12


def _bisect_topk(count_ge, count_zero, v_min, v_max, n_valid, k):
    kf = float(k)

    def probe(lo, hi, cnt_lo, cnt_hi, x):
        c = count_ge(_from_ordered_key(x))
        up = c >= kf
        return (jnp.where(up, x, lo), jnp.where(up, hi, x), jnp.where(up, c, cnt_lo), jnp.where(up, cnt_hi, c))

    def key_mid(lo, hi):
        return (lo >> 1) + (hi >> 1) + (lo & hi & 1)

    def next_probe(lo, hi, it):
        guess = _ordered_key(0.5 * _from_ordered_key(lo) + 0.5 * _from_ordered_key(hi))
        use = jnp.logical_and(it < VALUE_MID_ROUNDS, jnp.logical_and(guess > lo, guess < hi))
        return jnp.where(use, guess, key_mid(lo, hi))

    def cond(st):
        _, _, _, _, done, it = st
        return jnp.logical_and(it < 48, jnp.min(done) < 0.5)

    def body(st):
        lo, hi, cnt_lo, cnt_hi, _, it = st
        lo, hi, cnt_lo, cnt_hi = probe(lo, hi, cnt_lo, cnt_hi, next_probe(lo, hi, it))
        lo, hi, cnt_lo, cnt_hi = probe(lo, hi, cnt_lo, cnt_hi, next_probe(lo, hi, it))
        done = jnp.where(jnp.logical_or(cnt_lo <= kf, key_mid(lo, hi) == lo), 1.0, 0.0)
        return lo, hi, cnt_lo, cnt_hi, done, it + 1

    hi0 = _ordered_key(v_max)
    lo, hi, cnt_lo, cnt_hi = probe(_ordered_key(v_min), hi0, n_valid, jnp.zeros_like(n_valid), hi0)
    is_open = key_mid(lo, hi) != lo
    c_ge0, c_gt0 = count_zero()
    zero_tie = jnp.logical_and(is_open, jnp.logical_and(c_gt0 < kf, c_ge0 >= kf))
    raise_lo = jnp.logical_and(is_open, jnp.logical_and(c_gt0 >= kf, lo < 0))
    lower_hi = jnp.logical_and(is_open, jnp.logical_and(c_ge0 < kf, hi > 0))
    to_zero_lo = jnp.logical_or(zero_tie, raise_lo)
    lo = jnp.where(to_zero_lo, 0, lo)
    cnt_lo = jnp.where(to_zero_lo, c_ge0, cnt_lo)
    hi = jnp.where(jnp.logical_or(zero_tie, lower_hi), 0, hi)
    cnt_hi = jnp.where(lower_hi, c_ge0, cnt_hi)
    done0 = jnp.where(jnp.logical_or(n_valid <= kf, jnp.logical_or(cnt_lo <= kf, key_mid(lo, hi) == lo)),
                      1.0, 0.0)
    lo, _, cnt_lo, _, _, _ = lax.while_loop(cond, body, (lo, hi, cnt_lo, cnt_hi, done0, jnp.int32(0)))
    return jnp.where(n_valid <= kf, NEG, _from_ordered_key(lo)), cnt_lo


def _topk_cap_rows(s, row_min, row_max, n_valid, k):
    rows, n = s.shape
    kf = float(k)
    count_ge = lambda x: jnp.sum(jnp.where(s >= x, 1.0, 0.0), axis=1, keepdims=True)
    count_zero = lambda: (jnp.sum(jnp.where(s >= 0.0, 1.0, 0.0), axis=1, keepdims=True),
                          jnp.sum(jnp.where(s > 0.0, 1.0, 0.0), axis=1, keepdims=True))
    t, _ = _bisect_topk(count_ge, count_zero, row_min, row_max, n_valid, k)
    eq = jnp.where(s == t, 1.0, 0.0)
    need = kf - jnp.sum(jnp.where(s > t, 1.0, 0.0), axis=1, keepdims=True)
    n_grp = n // LANES
    r_i = lax.broadcasted_iota(jnp.int32, (LANES, LANES), 0)
    c_i = lax.broadcasted_iota(jnp.int32, (LANES, LANES), 1)
    upper = jnp.where(r_i < c_i, 1.0, 0.0).astype(BF16)
    stacked = jnp.concatenate([eq[:, j * LANES:(j + 1) * LANES] for j in range(n_grp)], axis=0)
    within = jnp.dot(stacked.astype(BF16), upper, preferred_element_type=F32)
    run = jnp.zeros((rows, 1), F32)
    caps = []
    for j in range(n_grp):
        sl = slice(j * LANES, (j + 1) * LANES)
        before = run + within[j * rows:(j + 1) * rows]
        sj = s[:, sl]
        keep = jnp.logical_or(sj > t, jnp.logical_and(sj == t, before < need))
        caps.append(jnp.where(keep, POS_INF, NEG))
        run = run + jnp.sum(eq[:, sl], axis=1, keepdims=True)
    return jnp.concatenate(caps, axis=1)


ACC_ROWS = 64


def _col_reduce(x, op):
    n, w = x.shape
    if n > ACC_ROWS and n % ACC_ROWS == 0:
        x = op(x.reshape(n // ACC_ROWS, ACC_ROWS, w), axis=0)
    return op(x, axis=0, keepdims=True)


def _count_ge_t(st_scr, x, nk, kc):
    tq = st_scr.shape[1]

    def body(c, acc):
        k0 = pl.multiple_of(c * kc, kc)
        hit = jnp.where(st_scr[pl.ds(k0, kc), :] >= x, 1.0, 0.0)
        return acc + jnp.sum(hit.reshape(kc // ACC_ROWS, ACC_ROWS, tq), axis=0)

    acc = lax.fori_loop(0, nk, body, jnp.zeros((ACC_ROWS, tq), F32))
    return jnp.sum(acc, axis=0, keepdims=True)


def _topk_threshold_t(st_scr, col_min, col_max, n_valid, k, nk, kc):
    tq = st_scr.shape[1]

    def count_zero():
        def body(c, carry):
            ge, gt = carry
            s = st_scr[pl.ds(pl.multiple_of(c * kc, kc), kc), :]
            ge = ge + jnp.sum(jnp.where(s >= 0.0, 1.0, 0.0).reshape(kc // ACC_ROWS, ACC_ROWS, tq), axis=0)
            gt = gt + jnp.sum(jnp.where(s > 0.0, 1.0, 0.0).reshape(kc // ACC_ROWS, ACC_ROWS, tq), axis=0)
            return ge, gt
        z = jnp.zeros((ACC_ROWS, tq), F32)
        ge, gt = lax.fori_loop(0, nk, body, (z, z))
        return jnp.sum(ge, axis=0, keepdims=True), jnp.sum(gt, axis=0, keepdims=True)

    return _bisect_topk(lambda x: _count_ge_t(st_scr, x, nk, kc), count_zero, col_min, col_max, n_valid, k)


def _write_topk_cap_t(st_scr, t, cnt_ge, k, nk, kc):
    kf = float(k)
    has_tie = jnp.max(cnt_ge) > kf

    @pl.when(jnp.logical_not(has_tie))
    def _():
        def body(c, _):
            sl = pl.ds(pl.multiple_of(c * kc, kc), kc)
            st_scr[sl, :] = jnp.where(st_scr[sl, :] >= t, POS_INF, NEG)
            return 0
        lax.fori_loop(0, nk, body, 0)

    @pl.when(has_tie)
    def _():
        def cnt_body(c, acc):
            sl = pl.ds(pl.multiple_of(c * kc, kc), kc)
            return acc + jnp.sum(jnp.where(st_scr[sl, :] > t, 1.0, 0.0), axis=0, keepdims=True)
        cnt_gt = lax.fori_loop(0, nk, cnt_body, jnp.zeros_like(t))
        need = kf - cnt_gt
        r_i = lax.broadcasted_iota(jnp.int32, (kc, kc), 0)
        c_i = lax.broadcasted_iota(jnp.int32, (kc, kc), 1)
        lower = jnp.where(c_i < r_i, 1.0, 0.0).astype(BF16)

        def body(c, run):
            sl = pl.ds(pl.multiple_of(c * kc, kc), kc)
            s = st_scr[sl, :]
            eq = jnp.where(s == t, 1.0, 0.0)
            before = run + jnp.dot(lower, eq.astype(BF16), preferred_element_type=F32)
            keep = jnp.logical_or(s > t, jnp.logical_and(s == t, before < need))
            st_scr[sl, :] = jnp.where(keep, POS_INF, NEG)
            return run + jnp.sum(eq, axis=0, keepdims=True)
        lax.fori_loop(0, nk, body, jnp.zeros_like(t))


def _q_pad(qt, g, hg, scale):
    qg = jnp.concatenate([qt[(g * hg + j) * HEAD_DIM:(g * hg + j + 1) * HEAD_DIM, :] for j in range(hg)],
                         axis=1) * scale
    z = jnp.zeros_like(qg)
    parts = [z] * NSA_KV_HEADS
    parts[g] = qg
    return jnp.concatenate(parts, axis=0).astype(BF16)


def _flash_t(q_pads, k_ref, vt_ref, vrow0s, cap_fns, c0, c1, kc, n_rep):
    n = q_pads[0].shape[1]
    n_g = len(q_pads)

    def body(c, carry):
        k0 = pl.multiple_of(c * kc, kc)
        k_chunk = k_ref[pl.ds(k0, kc), :]
        out = []
        for g in range(n_g):
            m, l, acc = carry[g]
            s = jnp.dot(k_chunk, q_pads[g], preferred_element_type=F32)
            s = jnp.minimum(s, jnp.concatenate([cap_fns[g](k0)] * n_rep, axis=1))
            m_new = jnp.maximum(m, _col_reduce(s, jnp.max))
            p = jnp.exp(s - m_new)
            alpha = jnp.exp(m - m_new)
            l = alpha * l + _col_reduce(p, jnp.sum)
            vt = vt_ref[pl.ds(vrow0s[g], HEAD_DIM), pl.ds(k0, kc)].astype(BF16)
            acc = alpha * acc + jnp.dot(vt, p.astype(BF16), preferred_element_type=F32)
            out.append((m_new, l, acc))
        return tuple(out)

    init = tuple((jnp.full((1, n), NEG, F32), jnp.zeros((1, n), F32), jnp.zeros((HEAD_DIM, n), F32))
                 for _ in range(n_g))
    fin = lax.fori_loop(c0, c1, body, init)
    return [acc / jnp.maximum(l, 1e-30) for _, l, acc in fin]


def _store_heads(o_ref, o_t, g, hg, tq):
    for j in range(hg):
        h = g * hg + j
        o_ref[:, h * HEAD_DIM:(h + 1) * HEAD_DIM] = o_t[:, j * tq:(j + 1) * tq].T


def _dsa_prompt_kernel(iqt_ref, smallt_ref, dqt_ref, idx3_ref, k_ref, kvt_ref, o_ref, st_scr,
                       *, tq, kc, topk):
    qs = pl.program_id(1) * tq
    nk = (qs + tq + kc - 1) // kc
    qpos = qs + lax.broadcasted_iota(jnp.int32, (1, tq), 1)

    iqt = iqt_ref[...]
    q3 = []
    for h in range(IDX_HEADS):
        qh = iqt[h * IDX_DIM:(h + 1) * IDX_DIM, :]
        q_hi = qh.astype(BF16)
        q_lo = (qh - q_hi.astype(F32)).astype(BF16)
        q3.append(jnp.concatenate([q_hi, q_hi, q_lo, jnp.zeros_like(q_hi)], axis=0))
    wt = smallt_ref[...]
    out_scale = (IDX_DIM ** -0.5) * (IDX_HEADS ** -0.5)

    def score_body(c, carry):
        mn, mx = carry
        k0 = pl.multiple_of(c * kc, kc)
        k3 = idx3_ref[pl.ds(k0, kc), :]
        sc = jnp.zeros((kc, tq), F32)
        for h in range(IDX_HEADS):
            d = jnp.dot(k3, q3[h], preferred_element_type=F32)
            sc = sc + jnp.maximum(d, 0.0) * wt[h:h + 1, :]
        sc = sc * out_scale
        ok = (k0 + lax.broadcasted_iota(jnp.int32, (kc, 1), 0)) <= qpos
        st_scr[pl.ds(k0, kc), :] = jnp.where(ok, sc, -POS_INF)
        mn = jnp.minimum(mn, _col_reduce(jnp.where(ok, sc, POS_INF), jnp.min))
        mx = jnp.maximum(mx, _col_reduce(jnp.where(ok, sc, NEG), jnp.max))
        return mn, mx

    mn, mx = lax.fori_loop(0, nk, score_body,
                           (jnp.full((1, tq), POS_INF, F32), jnp.full((1, tq), NEG, F32)))
    t, cnt = _topk_threshold_t(st_scr, mn, mx, (qpos + 1).astype(F32), topk, nk, kc)
    _write_topk_cap_t(st_scr, t, cnt, topk, nk, kc)

    hg = DSA_HEADS // DSA_KV_HEADS
    dqt = dqt_ref[...]
    groups = range(DSA_KV_HEADS)
    cap_fn = lambda k0: st_scr[pl.ds(k0, kc), :]
    o_ts = _flash_t([_q_pad(dqt, g, hg, HEAD_DIM ** -0.5) for g in groups], k_ref, kvt_ref,
                    [(DSA_KV_HEADS + g) * HEAD_DIM for g in groups], [cap_fn] * DSA_KV_HEADS, 0, nk, kc, hg)
    for g in groups:
        _store_heads(o_ref, o_ts[g], g, hg, tq)


def _dsa_prompt(f, bx, t_len, tq, kc):
    topk = min(DSA_TOPK_MAX, t_len // 4)
    nq = t_len // tq
    hg = DSA_HEADS // DSA_KV_HEADS
    row = lambda w: pl.BlockSpec((tq, w), lambda b, i: (b * nq + i, 0))
    qt = lambda rows: pl.BlockSpec((None, rows, tq), lambda b, i: (b, 0, i))
    keys = lambda w: pl.BlockSpec((t_len, w), lambda b, i: (b, 0))
    per_b = lambda rows: pl.BlockSpec((None, rows, t_len), lambda b, i: (b, 0, 0))
    return pl.pallas_call(
        functools.partial(_dsa_prompt_kernel, tq=tq, kc=kc, topk=topk),
        grid=(bx, nq),
        in_specs=[qt(W_IQ), qt(SMALL_ROWS), qt(W_DQ), keys(2 * LANES), keys(DSA_KV_HEADS * HEAD_DIM),
                  per_b(W_DKV)],
        out_specs=row(W_DQ),
        out_shape=jax.ShapeDtypeStruct((bx * t_len, W_DQ), F32),
        scratch_shapes=[pltpu.VMEM((t_len, tq), F32)],
        compiler_params=pltpu.CompilerParams(dimension_semantics=('arbitrary', 'arbitrary'),
                                             vmem_limit_bytes=VMEM_LIMIT),
        name='dsa_prompt',
    )(f['idx_q_t'], f['small_t'], f['dsa_q_t'], f['idx3'], f['dsa_k'], f['dsa_t'])


GROUP_TOKENS = CMP_STRIDE
N_HALVES = CMP_BLOCK // CMP_STRIDE
CMP_HID = 2 * HEAD_DIM


def _prep_compress_weights(cmp_pe, cmp_w1, cmp_w2):
    g_n = NSA_KV_HEADS
    eye = jnp.eye(g_n, dtype=F32)
    out = []
    for s in range(2):
        w1 = cmp_w1[s].reshape(N_HALVES, GROUP_TOKENS, HEAD_DIM, CMP_HID)
        w1e = jnp.einsum('ardj,gh->arghdj', w1, eye)
        w1e = w1e.transpose(0, 1, 2, 4, 3, 5).reshape(N_HALVES, GROUP_TOKENS * g_n * HEAD_DIM, g_n * CMP_HID)
        pe = cmp_pe[s].reshape(N_HALVES, GROUP_TOKENS, 1, HEAD_DIM)
        pe = jnp.broadcast_to(pe, (N_HALVES, GROUP_TOKENS, g_n, HEAD_DIM)).reshape(N_HALVES, 1, -1)
        w2e = jnp.einsum('jd,gh->gjhd', cmp_w2[s], eye).reshape(g_n * CMP_HID, g_n * HEAD_DIM)
        out.append((w1e, pe, w2e))
    (w1k, pek, w2k), (w1v, pev, w2v) = out
    w1k3 = jnp.stack([_split3_rows(w1k[a]) for a in range(N_HALVES)])
    return (w1k3, pek, _split3_rows(w2k), w1v.astype(BF16), pev, w2v.astype(BF16))


def _compress_slot(x_ref, m, pe_ref, w1_ref, w2_ref, shift_scr, precise):
    xr = jnp.concatenate([x_ref[pl.ds(r, m, stride=GROUP_TOKENS), :] for r in range(GROUP_TOKENS)], axis=1)
    halves = []
    for a in range(N_HALVES):
        xa = xr + pe_ref[a]
        lhs = _split3_cols(xa) if precise else xa.astype(BF16)
        halves.append(jnp.dot(lhs, w1_ref[a], preferred_element_type=F32))
    shift_scr[0:m, :] = halves[1]
    shift_scr[m:m + 8, :] = jnp.zeros((8, shift_scr.shape[1]), F32)
    hid = _silu(halves[0] + shift_scr[pl.ds(1, m), :])
    lhs = _split3_cols(hid) if precise else hid.astype(BF16)
    return jnp.dot(lhs, w2_ref[...], preferred_element_type=F32)


def _compress_kernel(xk_ref, xv_ref, w1k_ref, pek_ref, w2k_ref, w1v_ref, pev_ref, w2v_ref,
                     ck_ref, cv_ref, shift_scr, *, m):
    n_pad = ck_ref.shape[0]
    if n_pad > m:
        ck_ref[m:n_pad, :] = jnp.zeros((n_pad - m, ck_ref.shape[1]), F32)
        cv_ref[m:n_pad, :] = jnp.zeros((n_pad - m, cv_ref.shape[1]), F32)
    ck_ref[0:m, :] = _compress_slot(xk_ref, m, pek_ref, w1k_ref, w2k_ref, shift_scr, True)
    cv_ref[0:m, :] = _compress_slot(xv_ref, m, pev_ref, w1v_ref, w2v_ref, shift_scr, False)


def _compress_prompt(cmp_k, cmp_v, cw, bx, t_len):
    m = t_len // GROUP_TOKENS
    n_pad = -(-m // LANES) * LANES
    kvw = NSA_KV_HEADS * HEAD_DIM
    const = lambda a: pl.BlockSpec(a.shape, lambda b: (0,) * a.ndim, pipeline_mode=pl.Buffered(1))
    x_spec = pl.BlockSpec((t_len, kvw), lambda b: (b, 0))
    o_spec = pl.BlockSpec((None, n_pad, kvw), lambda b: (b, 0, 0))
    return pl.pallas_call(
        functools.partial(_compress_kernel, m=m),
        grid=(bx,),
        in_specs=[x_spec, x_spec] + [const(a) for a in cw],
        out_specs=[o_spec, o_spec],
        out_shape=[jax.ShapeDtypeStruct((bx, n_pad, kvw), F32)] * 2,
        scratch_shapes=[pltpu.VMEM((m + 8, NSA_KV_HEADS * CMP_HID), F32)],
        compiler_params=pltpu.CompilerParams(dimension_semantics=('arbitrary',), vmem_limit_bytes=VMEM_LIMIT),
        name='compress_prompt',
    )(cmp_k, cmp_v, *cw)


SEL_SHIFT = 6


def _stack_heads(x, g, hg, scale):
    return jnp.concatenate(
        [x[:, (g * hg + j) * HEAD_DIM:(g * hg + j + 1) * HEAD_DIM] for j in range(hg)], axis=0) * scale


def _cmp_branch(q_rows, ck_g, cv_g, qpos_rep, hg, tq):
    n_pad = ck_g.shape[0]
    nt = (((1,), (1,)), ((), ()))
    ck_hi = ck_g.astype(BF16)
    ck_lo = (ck_g - ck_hi.astype(F32)).astype(BF16)
    ck3 = jnp.concatenate([ck_hi, ck_lo, ck_hi], axis=1)
    s = lax.dot_general(_split3_cols(q_rows), ck3, nt, preferred_element_type=F32)
    cmp_end = lax.broadcasted_iota(jnp.int32, (1, n_pad), 1) * CMP_STRIDE + (CMP_BLOCK - 1)
    vis = cmp_end <= qpos_rep
    s = jnp.where(vis, s, NEG)
    p = jnp.where(vis, jnp.exp(s - jnp.max(s, axis=1, keepdims=True)), 0.0)
    p = p / jnp.maximum(jnp.sum(p, axis=1, keepdims=True), 1e-30)
    o_cmp = jnp.dot(p.astype(BF16), cv_g.astype(BF16), preferred_element_type=F32)
    p_sum = p[0:tq]
    for j in range(1, hg):
        p_sum = p_sum + p[j * tq:(j + 1) * tq]
    return o_cmp, p_sum


def _select_blocks(p_sum, qpos, n_slc, n_sel):
    tq, n_pad = p_sum.shape
    n_i = lax.broadcasted_iota(jnp.int32, (n_pad, LANES), 0) * CMP_STRIDE
    j_i = lax.broadcasted_iota(jnp.int32, (n_pad, LANES), 1)
    overlap = jnp.logical_and(n_i < (j_i + 1) * SEL_BLOCK, n_i + CMP_BLOCK > j_i * SEL_BLOCK)
    overlap = jnp.where(jnp.logical_and(overlap, j_i < n_slc), 1.0, 0.0).astype(BF16)
    p_hi = p_sum.astype(BF16)
    r1 = p_sum - p_hi.astype(F32)
    p_mid = r1.astype(BF16)
    p_lo = (r1 - p_mid.astype(F32)).astype(BF16)
    imp = (jnp.dot(p_hi, overlap, preferred_element_type=F32)
           + jnp.dot(p_mid, overlap, preferred_element_type=F32)
           + jnp.dot(p_lo, overlap, preferred_element_type=F32))
    jb = lax.broadcasted_iota(jnp.int32, (1, LANES), 1)
    cur = qpos >> SEL_SHIFT
    forced = jnp.logical_or(jb == 0, jnp.logical_or(jb == cur, jb == cur - 1))
    imp = jnp.where(forced, FORCED_SCORE, imp)
    imp = jnp.where(jb > cur, NEG, imp)
    rank = jnp.zeros((tq, LANES), F32)
    for i in range(n_slc):
        vi = imp[:, i:i + 1]
        ahead = jnp.logical_or(vi > imp, jnp.logical_and(vi == imp, jb > i))
        rank = rank + jnp.where(ahead, 1.0, 0.0)
    return jnp.where(jnp.logical_and(rank < float(n_sel), imp > 0.5 * NEG), 1.0, 0.0)


def _select_blocks_t(p_sum, qpos_t, n_slc, n_sel):
    tq, n_pad = p_sum.shape
    nb = -(-n_slc // 8) * 8
    j_i = lax.broadcasted_iota(jnp.int32, (nb, n_pad), 0)
    n_i = lax.broadcasted_iota(jnp.int32, (nb, n_pad), 1) * CMP_STRIDE
    overlap = jnp.logical_and(n_i < (j_i + 1) * SEL_BLOCK, n_i + CMP_BLOCK > j_i * SEL_BLOCK)
    overlap = jnp.where(jnp.logical_and(overlap, j_i < n_slc), 1.0, 0.0).astype(BF16)
    p_t = p_sum.T
    p_hi = p_t.astype(BF16)
    r1 = p_t - p_hi.astype(F32)
    p_mid = r1.astype(BF16)
    p_lo = (r1 - p_mid.astype(F32)).astype(BF16)
    imp = (jnp.dot(overlap, p_hi, preferred_element_type=F32)
           + jnp.dot(overlap, p_mid, preferred_element_type=F32)
           + jnp.dot(overlap, p_lo, preferred_element_type=F32))
    jb = lax.broadcasted_iota(jnp.int32, (nb, 1), 0)
    cur = qpos_t >> SEL_SHIFT
    forced = jnp.logical_or(jb == 0, jnp.logical_or(jb == cur, jb == cur - 1))
    imp = jnp.where(forced, FORCED_SCORE, imp)
    imp = jnp.where(jnp.logical_or(jb > cur, jb >= n_slc), NEG, imp)
    rank = jnp.zeros((nb, tq), F32)
    for i in range(n_slc):
        vi = imp[i:i + 1, :]
        ahead = jnp.logical_or(vi > imp, jnp.logical_and(vi == imp, jb > i))
        rank = rank + jnp.where(ahead, 1.0, 0.0)
    return jnp.where(jnp.logical_and(rank < float(n_sel), imp > 0.5 * NEG), 1.0, 0.0)


def _nsa_prompt_kernel(q_ref, qrt_ref, smallt_ref, ck_ref, cv_ref, slck_ref, wink_ref, slct_ref, wint_ref,
                       o_ref, capt_scr, *, tq, kc, kcw, n_slc, n_sel):
    qs = pl.program_id(1) * tq
    nk = (qs + tq + kc - 1) // kc
    hg = NSA_HEADS // NSA_KV_HEADS
    kvw = NSA_KV_HEADS * HEAD_DIM
    scale = HEAD_DIM ** -0.5
    qpos = qs + lax.broadcasted_iota(jnp.int32, (tq, 1), 0)
    qpos_t = qs + lax.broadcasted_iota(jnp.int32, (1, tq), 1)
    qpos_rep = jnp.concatenate([qpos] * hg, axis=0)
    q = q_ref[...]
    qrt = qrt_ref[...]
    gates_t = _sigmoid(smallt_ref[...])
    ck = ck_ref[...]
    cv = cv_ref[...]
    nb = -(-n_slc // 8) * 8
    blk_l = lax.broadcasted_iota(jnp.int32, (1, nb), 1)

    def win_cap(k0):
        dist = qpos_t - (k0 + lax.broadcasted_iota(jnp.int32, (kcw, 1), 0))
        return jnp.where(jnp.logical_and(dist >= 0, dist < WINDOW), POS_INF, NEG)

    cw0 = jnp.maximum(qs - WINDOW, 0) // kcw
    cw1 = (qs + tq + kcw - 1) // kcw

    groups = range(NSA_KV_HEADS)
    o_cmps, sel_ts = [], []
    for g in groups:
        o_cmp, p_sum = _cmp_branch(_stack_heads(q, g, hg, scale), ck[:, g * HEAD_DIM:(g + 1) * HEAD_DIM],
                                   cv[:, g * HEAD_DIM:(g + 1) * HEAD_DIM], qpos_rep, hg, tq)
        o_cmps.append(o_cmp)
        sel_ts.append(_select_blocks_t(p_sum, qpos_t, n_slc, n_sel).astype(BF16))

    def cap_body(c, _):
        k0 = pl.multiple_of(c * kc, kc)
        kpos = k0 + lax.broadcasted_iota(jnp.int32, (kc, 1), 0)
        expand = jnp.where((kpos >> SEL_SHIFT) == blk_l, 1.0, 0.0).astype(BF16)
        causal = kpos <= qpos_t
        for g in groups:
            sel_k = jnp.dot(expand, sel_ts[g], preferred_element_type=F32)
            capt_scr[g, pl.ds(k0, kc), :] = jnp.where(jnp.logical_and(sel_k > 0.5, causal), POS_INF, NEG)
        return 0

    lax.fori_loop(0, nk, cap_body, 0)
    q_pads = [_q_pad(qrt, g, hg, scale) for g in groups]
    vrows = [kvw + g * HEAD_DIM for g in groups]
    o_slcs = _flash_t(q_pads, slck_ref, slct_ref, vrows,
                      [functools.partial(lambda g, k0: capt_scr[g, pl.ds(k0, kc), :], g) for g in groups],
                      0, nk, kc, hg)
    o_wins = _flash_t(q_pads, wink_ref, wint_ref, vrows, [win_cap] * NSA_KV_HEADS, cw0, cw1, kcw, hg)
    for g in groups:
        for j in range(hg):
            h = g * hg + j
            r = IDX_HEADS + 3 * h
            cols = slice(j * tq, (j + 1) * tq)
            o_t = (gates_t[r:r + 1, :] * o_cmps[g][cols].T + gates_t[r + 1:r + 2, :] * o_slcs[g][:, cols]
                   + gates_t[r + 2:r + 3, :] * o_wins[g][:, cols])
            o_ref[:, h * HEAD_DIM:(h + 1) * HEAD_DIM] = o_t.T


def _nsa_prompt(f, ck, cv, bx, t_len, tq, kc, kcw):
    nq = t_len // tq
    hg = NSA_HEADS // NSA_KV_HEADS
    n_pad = ck.shape[1]
    n_slc = -(-t_len // SEL_BLOCK)
    kvw = NSA_KV_HEADS * HEAD_DIM
    row = lambda w: pl.BlockSpec((tq, w), lambda b, i: (b * nq + i, 0))
    cmp_spec = pl.BlockSpec((None, n_pad, kvw), lambda b, i: (b, 0, 0))
    keys = pl.BlockSpec((t_len, kvw), lambda b, i: (b, 0))
    return pl.pallas_call(
        functools.partial(_nsa_prompt_kernel, tq=tq, kc=kc, kcw=kcw, n_slc=n_slc, n_sel=min(N_SEL, n_slc)),
        grid=(bx, nq),
        in_specs=[row(W_NQ), pl.BlockSpec((None, W_NQ, tq), lambda b, i: (b, 0, i)),
                  pl.BlockSpec((None, SMALL_ROWS, tq), lambda b, i: (b, 0, i)),
                  cmp_spec, cmp_spec, keys, keys,
                  pl.BlockSpec((None, 2 * kvw, t_len), lambda b, i: (b, 1, 0)),
                  pl.BlockSpec((None, 2 * kvw, t_len), lambda b, i: (b, 0, 0))],
        out_specs=row(W_NQ),
        out_shape=jax.ShapeDtypeStruct((bx * t_len, W_NQ), F32),
        scratch_shapes=[pltpu.VMEM((NSA_KV_HEADS, t_len, tq), F32)],
        compiler_params=pltpu.CompilerParams(dimension_semantics=('arbitrary', 'arbitrary'),
                                             vmem_limit_bytes=VMEM_LIMIT),
        name='nsa_prompt',
    )(f['nsa_q'], f['nsa_q_rot_t'], f['small_t'], ck, cv, f['slc_k'], f['win_k'], f['nsa_t'], f['win_t'])


DEC_ROWS = 8


def _page_copies(cache_hbm, pt_ref, b, n_pages, page, buf, sem, slot):
    return [pltpu.make_async_copy(cache_hbm.at[pt_ref[b * n_pages + j]],
                                  buf.at[slot, :, pl.ds(j * page, page)], sem.at[slot])
            for j in range(n_pages)]


def _gather_step(caches, pt_ref, n_pages, page, bufs, sems):
    b = pl.program_id(0)
    nb = pl.num_programs(0)
    slot = b % 2

    @pl.when(b == 0)
    def _():
        for cache, buf, sem in zip(caches, bufs, sems):
            for cp in _page_copies(cache, pt_ref, 0, n_pages, page, buf, sem, 0):
                cp.start()

    @pl.when(b + 1 < nb)
    def _():
        for cache, buf, sem in zip(caches, bufs, sems):
            for cp in _page_copies(cache, pt_ref, b + 1, n_pages, page, buf, sem, 1 - slot):
                cp.start()

    for cache, buf, sem in zip(caches, bufs, sems):
        for cp in _page_copies(cache, pt_ref, b, n_pages, page, buf, sem, slot):
            cp.wait()
    return slot


def _attend_full(q_rows, kt, vt, cap, n_rep):
    nt = (((1,), (1,)), ((), ()))
    s = jnp.dot(q_rows, kt.astype(BF16), preferred_element_type=F32)
    s = jnp.minimum(s, jnp.concatenate([cap] * n_rep, axis=0))
    p = jnp.exp(s - jnp.max(s, axis=1, keepdims=True))
    den = jnp.maximum(jnp.sum(p, axis=1, keepdims=True), 1e-30)
    return lax.dot_general(p.astype(BF16), vt.astype(BF16), nt, preferred_element_type=F32) / den


def _decode_nsa_kernel(pt_ref, q_ref, qr_ref, small_ref, cache_hbm, new_ref, wst_ref, wnew_ref,
                       w1k_ref, pek_ref, w2k_ref, w1v_ref, pev_ref, w2v_ref, o_ref,
                       buf, sem, xk_scr, xv_scr, shift_scr, *, past, t_dec, n_pages, page):
    slot = _gather_step([cache_hbm], pt_ref, n_pages, page, [buf], [sem])
    rows = DEC_ROWS
    hg = NSA_HEADS // NSA_KV_HEADS
    kvw = NSA_KV_HEADS * HEAD_DIM
    scale = HEAD_DIM ** -0.5
    n_keys = buf.shape[2]
    buf[slot, :, pl.ds(past, LANES)] = new_ref[...]
    qpos = past + jnp.minimum(lax.broadcasted_iota(jnp.int32, (rows, 1), 0), t_dec - 1)
    qpos_rep = jnp.concatenate([qpos] * hg, axis=0)

    for j in range(n_pages):
        xk_scr[j * page:(j + 1) * page, :] = buf[slot, 0:kvw, j * page:(j + 1) * page].T
        xv_scr[j * page:(j + 1) * page, :] = buf[slot, kvw:2 * kvw, j * page:(j + 1) * page].T
    m = past // GROUP_TOKENS
    ck = _compress_slot(xk_scr, m, pek_ref, w1k_ref, w2k_ref, shift_scr, True)
    cv = _compress_slot(xv_scr, m, pev_ref, w1v_ref, w2v_ref, shift_scr, False)

    q = q_ref[...]
    qr = qr_ref[...]
    gates = _sigmoid(small_ref[...])
    n_slc = -(-(past + t_dec) // SEL_BLOCK)
    kpos = lax.broadcasted_iota(jnp.int32, (1, n_keys), 1)
    blk_i = lax.broadcasted_iota(jnp.int32, (LANES, 1), 0)
    expand = jnp.where((kpos >> SEL_SHIFT) == blk_i, 1.0, 0.0).astype(BF16)
    wb = wst_ref.shape[1]
    wpos = past - wb + lax.broadcasted_iota(jnp.int32, (1, wb + LANES), 1)
    wdist = qpos - wpos
    win_cap = jnp.where(jnp.logical_and(wdist >= 0, wdist < WINDOW), POS_INF, NEG)

    for g in range(NSA_KV_HEADS):
        gs = slice(g * HEAD_DIM, (g + 1) * HEAD_DIM)
        o_cmp, p_sum = _cmp_branch(_stack_heads(q, g, hg, scale), ck[:, gs], cv[:, gs], qpos_rep, hg, rows)
        sel = _select_blocks(p_sum, qpos, n_slc, min(N_SEL, n_slc)).astype(BF16)
        sel_k = jnp.dot(sel, expand, preferred_element_type=F32)
        slc_cap = jnp.where(jnp.logical_and(sel_k > 0.5, kpos <= qpos), POS_INF, NEG)
        qr_rows = _stack_heads(qr, g, hg, scale).astype(BF16)
        o_slc = _attend_full(qr_rows, buf[slot, 2 * kvw + g * HEAD_DIM:2 * kvw + (g + 1) * HEAD_DIM, :],
                             buf[slot, 3 * kvw + g * HEAD_DIM:3 * kvw + (g + 1) * HEAD_DIM, :], slc_cap, hg)
        wk = jnp.concatenate([wst_ref[g * HEAD_DIM:(g + 1) * HEAD_DIM, :],
                              wnew_ref[g * HEAD_DIM:(g + 1) * HEAD_DIM, :]], axis=1)
        wv = jnp.concatenate([wst_ref[kvw + g * HEAD_DIM:kvw + (g + 1) * HEAD_DIM, :],
                              wnew_ref[kvw + g * HEAD_DIM:kvw + (g + 1) * HEAD_DIM, :]], axis=1)
        o_win = _attend_full(qr_rows, wk, wv, win_cap, hg)
        for j in range(hg):
            h = g * hg + j
            lane = IDX_HEADS + 3 * h
            rs = slice(j * rows, (j + 1) * rows)
            o_ref[:, h * HEAD_DIM:(h + 1) * HEAD_DIM] = (gates[:, lane:lane + 1] * o_cmp[rs]
                                                         + gates[:, lane + 1:lane + 2] * o_slc[rs]
                                                         + gates[:, lane + 2:lane + 3] * o_win[rs])


def _decode_dsa_kernel(pt_ref, iq_ref, small_ref, dq_ref, idx_hbm, kv_hbm, inew_ref, kvnew_ref, o_ref,
                       ibuf, kbuf, isem, ksem, *, past, t_dec, n_pages, page, topk):
    slot = _gather_step([idx_hbm, kv_hbm], pt_ref, n_pages, page, [ibuf, kbuf], [isem, ksem])
    rows = DEC_ROWS
    hg = DSA_HEADS // DSA_KV_HEADS
    scale = HEAD_DIM ** -0.5
    n_keys = ibuf.shape[2]
    ibuf[slot, :, pl.ds(past, LANES)] = inew_ref[...]
    kbuf[slot, :, pl.ds(past, LANES)] = kvnew_ref[...]
    qpos = past + jnp.minimum(lax.broadcasted_iota(jnp.int32, (rows, 1), 0), t_dec - 1)

    ik = ibuf[slot]
    ik_hi = ik.astype(BF16)
    ik_lo = (ik - ik_hi.astype(F32)).astype(BF16)
    k3 = jnp.concatenate([ik_hi, ik_lo, ik_hi], axis=0)
    iq = iq_ref[...]
    w = small_ref[...]
    sc = jnp.zeros((rows, n_keys), F32)
    for h in range(IDX_HEADS):
        d = jnp.dot(_split3_cols(iq[:, h * IDX_DIM:(h + 1) * IDX_DIM]), k3, preferred_element_type=F32)
        sc = sc + jnp.maximum(d, 0.0) * w[:, h:h + 1]
    sc = sc * ((IDX_DIM ** -0.5) * (IDX_HEADS ** -0.5))
    kpos = lax.broadcasted_iota(jnp.int32, (1, n_keys), 1)
    real_row = lax.broadcasted_iota(jnp.int32, (rows, 1), 0) < t_dec
    ok = jnp.logical_and(kpos <= qpos, jnp.logical_or(real_row, kpos == 0))
    mn = jnp.min(jnp.where(ok, sc, POS_INF), axis=1, keepdims=True)
    mx = jnp.max(jnp.where(ok, sc, NEG), axis=1, keepdims=True)
    n_valid = jnp.where(real_row, qpos + 1, 1).astype(F32)
    cap = _topk_cap_rows(jnp.where(ok, sc, -POS_INF), mn, mx, n_valid, topk)

    dq = dq_ref[...]
    for g in range(DSA_KV_HEADS):
        q_rows = _stack_heads(dq, g, hg, scale).astype(BF16)
        o = _attend_full(q_rows, kbuf[slot, g * HEAD_DIM:(g + 1) * HEAD_DIM, :],
                         kbuf[slot, (DSA_KV_HEADS + g) * HEAD_DIM:(DSA_KV_HEADS + g + 1) * HEAD_DIM, :], cap, hg)
        for j in range(hg):
            h = g * hg + j
            o_ref[:, h * HEAD_DIM:(h + 1) * HEAD_DIM] = o[j * rows:(j + 1) * rows]


def _pad_rows(a2d, bd, t_dec):
    return jnp.pad(a2d.reshape(bd, t_dec, -1), ((0, 0), (0, DEC_ROWS - t_dec), (0, 0)))


def _new_cols(a_t, bd, t_dec):
    f = a_t.shape[0]
    return jnp.pad(a_t.reshape(f, bd, t_dec).transpose(1, 0, 2), ((0, 0), (0, 0), (0, LANES - t_dec)))


def _feature_major_pages(pool):
    n_pool, page = pool.shape[:2]
    nd = pool.ndim
    return pool.transpose((0,) + tuple(range(2, nd)) + (1,)).reshape(n_pool, -1, page)


def _decode_nsa(fs, cache_nsa, win_state, page_table, cw, past, t_dec):
    bd, n_pages = page_table.shape
    pool = _feature_major_pages(cache_nsa)
    page = pool.shape[2]
    f = pool.shape[1]
    n_keys = past + LANES
    wst = _feature_major_pages(win_state)
    kvw = NSA_KV_HEADS * HEAD_DIM
    per_b = lambda *shape: pl.BlockSpec((None,) + shape, lambda b, pt: (b,) + (0,) * len(shape))
    const = lambda a: pl.BlockSpec(a.shape, lambda b, pt: (0,) * a.ndim, pipeline_mode=pl.Buffered(1))
    grid_spec = pltpu.PrefetchScalarGridSpec(
        num_scalar_prefetch=1,
        grid=(bd,),
        in_specs=[per_b(DEC_ROWS, W_NQ), per_b(DEC_ROWS, W_NQ), per_b(DEC_ROWS, LANES),
                  pl.BlockSpec(memory_space=pl.ANY), per_b(f, LANES),
                  per_b(2 * kvw, wst.shape[2]), per_b(2 * kvw, LANES)] + [const(a) for a in cw],
        out_specs=per_b(DEC_ROWS, W_NQ),
        scratch_shapes=[pltpu.VMEM((2, f, n_keys), F32), pltpu.SemaphoreType.DMA((2,)),
                        pltpu.VMEM((past, kvw), F32), pltpu.VMEM((past, kvw), F32),
                        pltpu.VMEM((past // GROUP_TOKENS + 8, NSA_KV_HEADS * CMP_HID), F32)])
    out = pl.pallas_call(
        functools.partial(_decode_nsa_kernel, past=past, t_dec=t_dec, n_pages=n_pages, page=page),
        grid_spec=grid_spec,
        out_shape=jax.ShapeDtypeStruct((bd, DEC_ROWS, W_NQ), F32),
        compiler_params=pltpu.CompilerParams(dimension_semantics=('arbitrary',), vmem_limit_bytes=VMEM_LIMIT),
        name='decode_nsa',
    )(page_table.reshape(-1), _pad_rows(fs['nsa_q'], bd, t_dec), _pad_rows(fs['nsa_q_rot'], bd, t_dec),
      _pad_rows(fs['small'], bd, t_dec), pool, _new_cols(fs['nsa_t'], bd, t_dec), wst,
      _new_cols(fs['win_t'], bd, t_dec), *cw)
    return out[:, :t_dec].reshape(bd * t_dec, W_NQ)


def _decode_dsa(fs, cache_dsa, cache_idx, page_table, past, t_dec):
    bd, n_pages = page_table.shape
    kpool = _feature_major_pages(cache_dsa)
    ipool = _feature_major_pages(cache_idx)
    page = kpool.shape[2]
    n_keys = past + LANES
    topk = min(DSA_TOPK_MAX, (past + t_dec) // 4)
    per_b = lambda *shape: pl.BlockSpec((None,) + shape, lambda b, pt: (b,) + (0,) * len(shape))
    grid_spec = pltpu.PrefetchScalarGridSpec(
        num_scalar_prefetch=1,
        grid=(bd,),
        in_specs=[per_b(DEC_ROWS, W_IQ), per_b(DEC_ROWS, LANES), per_b(DEC_ROWS, W_DQ),
                  pl.BlockSpec(memory_space=pl.ANY), pl.BlockSpec(memory_space=pl.ANY),
                  per_b(IDX_DIM, LANES), per_b(W_DKV, LANES)],
        out_specs=per_b(DEC_ROWS, W_DQ),
        scratch_shapes=[pltpu.VMEM((2, IDX_DIM, n_keys), F32), pltpu.VMEM((2, W_DKV, n_keys), F32),
                        pltpu.SemaphoreType.DMA((2,)), pltpu.SemaphoreType.DMA((2,))])
    out = pl.pallas_call(
        functools.partial(_decode_dsa_kernel, past=past, t_dec=t_dec, n_pages=n_pages, page=page, topk=topk),
        grid_spec=grid_spec,
        out_shape=jax.ShapeDtypeStruct((bd, DEC_ROWS, W_DQ), F32),
        compiler_params=pltpu.CompilerParams(dimension_semantics=('arbitrary',), vmem_limit_bytes=VMEM_LIMIT),
        name='decode_dsa',
    )(page_table.reshape(-1), _pad_rows(fs['idx_q'], bd, t_dec), _pad_rows(fs['small'], bd, t_dec),
      _pad_rows(fs['dsa_q'], bd, t_dec), ipool, kpool, _new_cols(fs['idx_t'], bd, t_dec),
      _new_cols(fs['dsa_t'], bd, t_dec))
    return out[:, :t_dec].reshape(bd * t_dec, W_DQ)


def _rows_from_t(a_t, bx, t, lead):
    return a_t.reshape(lead + (bx, t)).transpose((len(lead), len(lead) + 1) + tuple(range(len(lead))))


def _rows_from_bt(a_bt, lead):
    bx, _, t = a_bt.shape
    n = len(lead)
    return a_bt.reshape((bx,) + lead + (t,)).transpose((0, n + 1) + tuple(range(1, n + 1)))


def kernel(x_prompt, x_sample, cache_nsa_kv, cache_dsa_kv, cache_dsa_idx_k, state_nsa_win_kv, page_table,
           c_prompt, c_sample, norm_g, w_ada, b_ada, w_in, cmp_pe, cmp_w1, cmp_w2,
           w_up_nsa, w_up_dsa, w_out, final_g):
    depth = norm_g.shape[0]
    assert depth == 1
    B, T, D = x_prompt.shape
    Bd, Td, _ = x_sample.shape
    l = 0
    past = page_table.shape[1] * cache_nsa_kv.shape[2]

    c_all = jnp.concatenate([c_prompt, c_sample], axis=0)
    pad = (-c_all.shape[0]) % 8
    c_all = jnp.pad(c_all, ((0, pad), (0, 0)))
    mod = _ada_mod(c_all, w_ada[l], b_ada[l])
    shift, scale, gate = mod[:, 0:D], mod[:, D:2 * D], mod[:, 2 * D:3 * D]

    weights = _prep_inproj_weights(w_in[l], D)

    tm = PROJ_ROWS
    fp = _in_projection_prompt(x_prompt.reshape(B * T, D), norm_g[l], scale[:B], shift[:B], T,
                               _prep_inproj_weights_prompt(w_in[l], D), tm)
    nsa_rows = _rows_from_bt(fp['nsa_t'], (4, NSA_KV_HEADS, HEAD_DIM))
    wb_p = min(WINDOW, T)
    new_win_prompt = _rows_from_bt(fp['win_t'][:, :, T - wb_p:], (2, NSA_KV_HEADS, HEAD_DIM))
    dsa_rows = _rows_from_bt(fp['dsa_t'], (2, DSA_KV_HEADS, HEAD_DIM))
    idx_rows = _rows_from_bt(fp['idx_t'], (IDX_DIM,))
    cw = _prep_compress_weights(cmp_pe[l], cmp_w1[l], cmp_w2[l])
    ck, cv = _compress_prompt(fp['cmp_k'], fp['cmp_v'], cw, B, T)
    kc = min(KEY_CHUNK, T)
    o_n = _nsa_prompt(fp, ck, cv, B, T, Q_TILE, kc, WIN_CHUNK)
    o_d = _dsa_prompt(fp, B, T, Q_TILE, kc)
    y_prompt = _mixer_out(x_prompt.reshape(B * T, D), o_n, o_d, fp, gate[:B], T, final_g,
                          w_up_nsa[l], w_up_dsa[l], w_out[l], tm).reshape(B, T, D)

    Rs = Bd * Td
    tms = min(PROJ_ROWS, Rs)
    pos_s = past + jnp.arange(Td, dtype=jnp.int32)
    pos_rows = jnp.tile(pos_s, tms // Td)
    rep = lambda a: jnp.repeat(a[B:B + Bd], Td, axis=0)
    fs = _in_projection(x_sample.reshape(Rs, D), norm_g[l], rep(scale), rep(shift), 1, pos_rows, weights, tms, 1)
    for name in ('nsa_t', 'win_t', 'dsa_t', 'idx_t'):
        fs[name] = fs[name][0]
    s_nsa_rows = _rows_from_t(fs['nsa_t'], Bd, Td, (4, NSA_KV_HEADS, HEAD_DIM))
    s_win_rows = _rows_from_t(fs['win_t'], Bd, Td, (2, NSA_KV_HEADS, HEAD_DIM))
    s_dsa_rows = _rows_from_t(fs['dsa_t'], Bd, Td, (2, DSA_KV_HEADS, HEAD_DIM))
    s_idx_rows = _rows_from_t(fs['idx_t'], Bd, Td, (IDX_DIM,))

    win_state = state_nsa_win_kv[l]
    so_n = _decode_nsa(fs, cache_nsa_kv[l], win_state, page_table, cw, past, Td)
    so_d = _decode_dsa(fs, cache_dsa_kv[l], cache_dsa_idx_k[l], page_table, past, Td)
    y_sample = _mixer_out(x_sample.reshape(Rs, D), so_n, so_d, fs,
                          rep(gate), 1, final_g, w_up_nsa[l], w_up_dsa[l], w_out[l], tms).reshape(Bd, Td, D)
    new_win_sample = jnp.concatenate([win_state, s_win_rows], axis=1)[:, Td:]

    return (y_prompt, y_sample, nsa_rows[None], dsa_rows[None], idx_rows[None], new_win_prompt[None],
            s_nsa_rows[None], s_dsa_rows[None], s_idx_rows[None], new_win_sample[None])
```

```python
import functools

import numpy as np
import jax
import jax.numpy as jnp
from jax import lax
from jax.experimental import pallas as pl
from jax.experimental.pallas import tpu as pltpu

F32 = jnp.float32
BF16 = jnp.bfloat16

HEAD_DIM = 64
NSA_HEADS = 8
NSA_KV_HEADS = 2
DSA_HEADS = 8
DSA_KV_HEADS = 2
N_NSA_KV = 6
CMP_BLOCK = 32
CMP_STRIDE = 16
SEL_BLOCK = 64
N_SEL = 16
WINDOW = 512
IDX_HEADS = 4
IDX_DIM = 64
DSA_TOPK_MAX = 256
ROPE_THETA = 10000.0
EPS = 1e-6
NEG = -1e30
FORCED_SCORE = 1e4

LANES = 128
VMEM_LIMIT = 56 * 1024 * 1024

PROJ_ROWS = 256
Q_TILE = LANES
KEY_CHUNK = 1024

W_NQ = NSA_HEADS * HEAD_DIM
W_NKV = N_NSA_KV * NSA_KV_HEADS * HEAD_DIM
W_NG = NSA_HEADS * 3
W_DQ = DSA_HEADS * HEAD_DIM
W_DKV = 2 * DSA_KV_HEADS * HEAD_DIM
W_IQ = IDX_HEADS * IDX_DIM


def _cuts(d_model):
    widths = (W_NQ, W_NKV, W_NG, W_NQ, W_DQ, W_DKV, W_IQ, IDX_DIM, IDX_HEADS, W_DQ, 2 * d_model)
    c = np.concatenate([[0], np.cumsum(widths)])
    names = ('nq', 'nkv', 'ng', 'nz', 'dq', 'dkv', 'iq', 'ik', 'iw', 'dz', 'mg')
    return {n: (int(c[i]), int(c[i + 1])) for i, n in enumerate(names)}


def _split3_rows(w):
    wh = w.astype(BF16)
    wl = (w - wh.astype(F32)).astype(BF16)
    return jnp.concatenate([wh, wl, wh], axis=0)


def _split3_cols(x):
    xh = x.astype(BF16)
    xl = (x - xh.astype(F32)).astype(BF16)
    return jnp.concatenate([xh, xh, xl], axis=1)


def _silu(v):
    return v * (1.0 / (1.0 + jnp.exp(-v)))


def _sigmoid(v):
    return 1.0 / (1.0 + jnp.exp(-v))


def _ada_kernel(c_ref, w_ref, b_ref, o_ref):
    c3 = _split3_cols(_silu(c_ref[...]))
    o_ref[...] = jnp.dot(c3, w_ref[...], preferred_element_type=F32) + b_ref[...]


def _ada_mod(c_all, w_ada, b_ada):
    m, d = c_all.shape
    n = w_ada.shape[1]
    tn = 512
    w3 = _split3_rows(w_ada)
    return pl.pallas_call(
        _ada_kernel,
        grid=(n // tn,),
        in_specs=[pl.BlockSpec((m, d), lambda j: (0, 0)),
                  pl.BlockSpec((3 * d, tn), lambda j: (0, j)),
                  pl.BlockSpec((1, tn), lambda j: (0, j))],
        out_specs=pl.BlockSpec((m, tn), lambda j: (0, j)),
        out_shape=jax.ShapeDtypeStruct((m, n), F32),
        name='ada_mod',
    )(c_all, w3, b_ada.reshape(1, n))


def _rope_tf(v, cos, sin_signed):
    first_half = (lax.broadcasted_iota(jnp.int32, (1, LANES), 1) % HEAD_DIM) < (HEAD_DIM // 2)
    outs = []
    for c in range(v.shape[1] // LANES):
        vc = v[:, c * LANES:(c + 1) * LANES]
        sw = jnp.where(first_half, pltpu.roll(vc, LANES - HEAD_DIM // 2, 1), pltpu.roll(vc, HEAD_DIM // 2, 1))
        outs.append(vc * cos + sw * sin_signed)
    return jnp.concatenate(outs, axis=1) if len(outs) > 1 else outs[0]


def _rope_ft(v, cos_t, sin_t):
    half = HEAD_DIM // 2
    outs = []
    for hd in range(v.shape[0] // HEAD_DIM):
        x1 = v[hd * HEAD_DIM:hd * HEAD_DIM + half]
        x2 = v[hd * HEAD_DIM + half:(hd + 1) * HEAD_DIM]
        outs.append(x1 * cos_t - x2 * sin_t)
        outs.append(x2 * cos_t + x1 * sin_t)
    return jnp.concatenate(outs, axis=0)


def _inproj_kernel(x_ref, g_ref, sc_ref, sh_ref, cos_ref, sin_ref, cost_ref, sint_ref,
                   wp_ref, wa_ref, wt_ref, wti_ref,
                   q_ref, qrot_ref, iq_ref, small_ref, ck_ref,
                   cv_ref, snz_ref, sdz_ref, dq_ref, mg_ref,
                   nsat_ref, wint_ref, dsat_ref, idxt_ref, idx3_ref):
    x = x_ref[...]
    y = x * lax.rsqrt(jnp.mean(x * x, axis=-1, keepdims=True) + EPS) * g_ref[...]
    h = y * (1.0 + sc_ref[...]) + sh_ref[...]
    hh = h.astype(BF16)
    hl = (h - hh.astype(F32)).astype(BF16)
    h3 = jnp.concatenate([hh, hh, hl], axis=1)
    cos = cos_ref[...]
    sin = sin_ref[...]
    cos_t = cost_ref[...]
    sin_t = sint_ref[...]

    p = jnp.dot(h3, wp_ref[...], preferred_element_type=F32)
    q = p[:, 0:W_NQ]
    q_ref[...] = q
    qrot_ref[...] = _rope_tf(q, cos, sin)
    iq_ref[...] = _rope_tf(p[:, W_NQ:W_NQ + W_IQ], cos, sin)
    small_ref[...] = p[:, W_NQ + W_IQ:W_NQ + W_IQ + LANES]
    ck_ref[...] = p[:, W_NQ + W_IQ + LANES:W_NQ + W_IQ + 2 * LANES]

    a = jnp.dot(hh, wa_ref[...], preferred_element_type=F32)
    cv_ref[...] = a[:, 0:LANES]
    o = LANES
    snz_ref[...] = _silu(a[:, o:o + W_NQ])
    o += W_NQ
    sdz_ref[...] = _silu(a[:, o:o + W_DQ])
    o += W_DQ
    dq_ref[...] = _rope_tf(a[:, o:o + W_DQ], cos, sin)
    o += W_DQ
    mg_ref[...] = _sigmoid(a[:, o:])

    nt = (((1,), (1,)), ((), ()))
    t = lax.dot_general(wt_ref[...], hh, nt, preferred_element_type=F32)
    kv = 2 * HEAD_DIM
    nsat_ref[0:2 * kv, :] = t[0:2 * kv]
    nsat_ref[2 * kv:3 * kv, :] = _rope_ft(t[2 * kv:3 * kv], cos_t, sin_t)
    nsat_ref[3 * kv:4 * kv, :] = t[3 * kv:4 * kv]
    wint_ref[0:kv, :] = _rope_ft(t[4 * kv:5 * kv], cos_t, sin_t)
    wint_ref[kv:2 * kv, :] = t[5 * kv:6 * kv]
    dsat_ref[0:kv, :] = _rope_ft(t[6 * kv:7 * kv], cos_t, sin_t)
    dsat_ref[kv:2 * kv, :] = t[7 * kv:8 * kv]
    ti = lax.dot_general(wti_ref[...], h3, nt, preferred_element_type=F32)
    ti = _rope_ft(ti, cos_t, sin_t)
    idxt_ref[...] = ti
    ti_hi = ti.astype(BF16)
    ti_lo = (ti - ti_hi.astype(F32)).astype(BF16)
    idx3_ref[...] = jnp.concatenate([ti_hi, ti_lo, ti_hi], axis=0)


def _prep_inproj_weights(w_in, d_model):
    c = _cuts(d_model)
    sl = lambda name: w_in[:, c[name][0]:c[name][1]]
    nkv = sl('nkv')
    kv = NSA_KV_HEADS * HEAD_DIM
    small = jnp.concatenate([sl('iw'), sl('ng'), jnp.zeros((d_model, LANES - IDX_HEADS - W_NG), F32)], axis=1)
    wp = jnp.concatenate([sl('nq'), sl('iq'), small, nkv[:, 0:kv]], axis=1)
    wa = jnp.concatenate([nkv[:, kv:2 * kv], sl('nz'), sl('dz'), sl('dq'), sl('mg')], axis=1)
    wt = jnp.concatenate([nkv, sl('dkv')], axis=1).T
    wti = sl('ik')
    return _split3_rows(wp), wa.astype(BF16), wt.astype(BF16), _split3_rows(wti).T


def _rope_tables(pos):
    half = HEAD_DIM // 2
    inv = ROPE_THETA ** (-jnp.arange(half, dtype=F32) / half)
    ang = pos.astype(F32)[:, None] * inv[None, :]
    cos = jnp.cos(ang)
    sin = jnp.sin(ang)
    cos_tf = jnp.tile(cos, (1, LANES // half))
    sin_tf = jnp.tile(jnp.concatenate([-sin, sin], axis=1), (1, LANES // HEAD_DIM))
    return cos_tf, sin_tf, cos.T, sin.T


def _in_projection(x2d, norm_g, scale, shift, rows_per_mod, pos, weights, tm, nb):
    r, d = x2d.shape
    tpb = (r // nb) // tm
    wp, wa, wt, wti = weights
    n_tiles = r // tm
    cos_tf, sin_tf, cos_t, sin_t = _rope_tables(pos)
    p_tiles = pos.shape[0] // tm
    if rows_per_mod == 1:
        mod_spec = pl.BlockSpec((tm, d), lambda i: (i, 0))
        sc, sh = scale, shift
    else:
        per = rows_per_mod // tm
        mod_spec = pl.BlockSpec((None, 1, d), lambda i: (i // per, 0, 0))
        sc, sh = scale[:, None, :], shift[:, None, :]
    const = lambda shape: pl.BlockSpec(shape, lambda i: (0,) * len(shape), pipeline_mode=pl.Buffered(1))
    tf = lambda w: pl.BlockSpec((tm, w), lambda i: (i, 0))
    ft = lambda w: pl.BlockSpec((None, w, tm), lambda i: (i // tpb, 0, i % tpb))
    tf_widths = (W_NQ, W_NQ, W_IQ, LANES, LANES, LANES, W_NQ, W_DQ, W_DQ, 2 * d)
    ft_widths = (4 * LANES, 2 * LANES, 2 * LANES, IDX_DIM)
    outs = pl.pallas_call(
        _inproj_kernel,
        grid=(n_tiles,),
        in_specs=[pl.BlockSpec((tm, d), lambda i: (i, 0)),
                  const((1, d)), mod_spec, mod_spec,
                  pl.BlockSpec((tm, LANES), lambda i: (i % p_tiles, 0)),
                  pl.BlockSpec((tm, LANES), lambda i: (i % p_tiles, 0)),
                  pl.BlockSpec((HEAD_DIM // 2, tm), lambda i: (0, i % p_tiles)),
                  pl.BlockSpec((HEAD_DIM // 2, tm), lambda i: (0, i % p_tiles)),
                  const(wp.shape), const(wa.shape), const(wt.shape), const(wti.shape)],
        out_specs=[tf(w) for w in tf_widths] + [ft(w) for w in ft_widths] + [ft(3 * IDX_DIM)],
        out_shape=[jax.ShapeDtypeStruct((r, w), F32) for w in tf_widths]
        + [jax.ShapeDtypeStruct((nb, w, r // nb), F32) for w in ft_widths]
        + [jax.ShapeDtypeStruct((nb, 3 * IDX_DIM, r // nb), BF16)],
        compiler_params=pltpu.CompilerParams(dimension_semantics=('arbitrary',), vmem_limit_bytes=VMEM_LIMIT),
        name='in_projection',
    )(x2d, norm_g.reshape(1, d), sc, sh, cos_tf, sin_tf, cos_t, sin_t, wp, wa, wt, wti)
    names = ('nsa_q', 'nsa_q_rot', 'idx_q', 'small', 'cmp_k', 'cmp_v', 'silu_nz', 'silu_dz', 'dsa_q', 'merge',
             'nsa_t', 'win_t', 'dsa_t', 'idx_t', 'idx3_t')
    return dict(zip(names, outs))


SMALL_ROWS = 32


def _inproj_prompt_kernel(x_ref, g_ref, sc_ref, sh_ref, cos_ref, sin_ref, cost_ref, sint_ref,
                          wp_ref, wa_ref, wtp_ref, wt_ref,
                          q_ref, ck_ref, idx3_ref,
                          cv_ref, slck_ref, wink_ref, dsak_ref, snz_ref, sdz_ref, mg_ref,
                          iqt_ref, smallt_ref, qrt_ref, dqt_ref, nsat_ref, wint_ref, dsat_ref, idxt_ref):
    x = x_ref[...]
    y = x * lax.rsqrt(jnp.mean(x * x, axis=-1, keepdims=True) + EPS) * g_ref[...]
    h = y * (1.0 + sc_ref[...]) + sh_ref[...]
    hh = h.astype(BF16)
    hl = (h - hh.astype(F32)).astype(BF16)
    h3 = jnp.concatenate([hh, hh, hl], axis=1)
    cos = cos_ref[...]
    sin = sin_ref[...]
    cos_t = cost_ref[...]
    sin_t = sint_ref[...]
    nt = (((1,), (1,)), ((), ()))
    kv = NSA_KV_HEADS * HEAD_DIM

    p = jnp.dot(h3, wp_ref[...], preferred_element_type=F32)
    q_ref[...] = p[:, 0:W_NQ]
    ck_ref[...] = p[:, W_NQ:W_NQ + LANES]
    ik = _rope_tf(p[:, W_NQ + LANES:W_NQ + 2 * LANES], cos, sin)
    ik_hi = ik.astype(BF16)
    ik_lo = (ik - ik_hi.astype(F32)).astype(BF16)
    first = lax.broadcasted_iota(jnp.int32, (1, LANES), 1) < IDX_DIM
    idx3_ref[...] = jnp.concatenate([jnp.where(first, ik_hi, ik_lo),
                                     jnp.where(first, ik_hi, jnp.zeros_like(ik_hi))], axis=1)

    a = jnp.dot(hh, wa_ref[...], preferred_element_type=F32)
    cv_ref[...] = a[:, 0:kv]
    slck_ref[...] = _rope_tf(a[:, kv:2 * kv], cos, sin).astype(BF16)
    wink_ref[...] = _rope_tf(a[:, 2 * kv:3 * kv], cos, sin).astype(BF16)
    dsak_ref[...] = _rope_tf(a[:, 3 * kv:4 * kv], cos, sin).astype(BF16)
    o = 4 * kv
    snz_ref[...] = _silu(a[:, o:o + W_NQ])
    o += W_NQ
    sdz_ref[...] = _silu(a[:, o:o + W_DQ])
    o += W_DQ
    mg_ref[...] = _sigmoid(a[:, o:])

    tp = lax.dot_general(wtp_ref[...], h3, nt, preferred_element_type=F32)
    iqt_ref[...] = _rope_ft(tp[0:W_IQ], cos_t, sin_t)
    smallt_ref[...] = tp[W_IQ:W_IQ + SMALL_ROWS]

    t = lax.dot_general(wt_ref[...], hh, nt, preferred_element_type=F32)
    qrt_ref[...] = _rope_ft(t[0:W_NQ], cos_t, sin_t)
    o = W_NQ
    dqt_ref[...] = _rope_ft(t[o:o + W_DQ], cos_t, sin_t)
    o += W_DQ
    nsat_ref[0:2 * kv, :] = t[o:o + 2 * kv]
    nsat_ref[2 * kv:3 * kv, :] = _rope_ft(t[o + 2 * kv:o + 3 * kv], cos_t, sin_t)
    nsat_ref[3 * kv:4 * kv, :] = t[o + 3 * kv:o + 4 * kv]
    wint_ref[0:kv, :] = _rope_ft(t[o + 4 * kv:o + 5 * kv], cos_t, sin_t)
    wint_ref[kv:2 * kv, :] = t[o + 5 * kv:o + 6 * kv]
    dsat_ref[0:kv, :] = _rope_ft(t[o + 6 * kv:o + 7 * kv], cos_t, sin_t)
    dsat_ref[kv:2 * kv, :] = t[o + 7 * kv:o + 8 * kv]
    idxt_ref[...] = _rope_ft(t[o + 8 * kv:o + 8 * kv + IDX_DIM], cos_t, sin_t)


def _prep_inproj_weights_prompt(w_in, d_model):
    c = _cuts(d_model)
    sl = lambda name: w_in[:, c[name][0]:c[name][1]]
    nkv = sl('nkv')
    kv = NSA_KV_HEADS * HEAD_DIM
    zeros = lambda n: jnp.zeros((d_model, n), F32)
    small = jnp.concatenate([sl('iw'), sl('ng')], axis=1)
    wp = jnp.concatenate([sl('nq'), nkv[:, 0:kv], sl('ik'), sl('ik')], axis=1)
    wa = jnp.concatenate([nkv[:, kv:2 * kv], nkv[:, 2 * kv:3 * kv], nkv[:, 4 * kv:5 * kv], sl('dkv')[:, 0:kv],
                          sl('nz'), sl('dz'), sl('mg')], axis=1)
    wtp = jnp.concatenate([sl('iq'), small, zeros(SMALL_ROWS - small.shape[1])], axis=1)
    wt = jnp.concatenate([sl('nq'), sl('dq'), nkv, sl('dkv'), sl('ik')], axis=1)
    return _split3_rows(wp), wa.astype(BF16), _split3_rows(wtp).T, wt.T.astype(BF16)


def _in_projection_prompt(x2d, norm_g, scale, shift, t_len, weights, tm):
    r, d = x2d.shape
    nb = r // t_len
    tpb = t_len // tm
    wp, wa, wtp, wt = weights
    cos_tf, sin_tf, cos_t, sin_t = _rope_tables(jnp.arange(t_len, dtype=jnp.int32))
    mod_spec = pl.BlockSpec((None, 1, d), lambda i: (i // tpb, 0, 0))
    const = lambda shape: pl.BlockSpec(shape, lambda i: (0,) * len(shape), pipeline_mode=pl.Buffered(1))
    tf = lambda w: pl.BlockSpec((tm, w), lambda i: (i, 0))
    ft = lambda w: pl.BlockSpec((None, w, tm), lambda i: (i // tpb, 0, i % tpb))
    kv = NSA_KV_HEADS * HEAD_DIM
    tf_outs = (('nsa_q', W_NQ, F32), ('cmp_k', kv, F32), ('idx3', 2 * LANES, BF16),
               ('cmp_v', kv, F32), ('slc_k', kv, BF16), ('win_k', kv, BF16), ('dsa_k', kv, BF16),
               ('silu_nz', W_NQ, F32), ('silu_dz', W_DQ, F32), ('merge', 2 * d, F32))
    ft_outs = (('idx_q_t', W_IQ), ('small_t', SMALL_ROWS), ('nsa_q_rot_t', W_NQ), ('dsa_q_t', W_DQ),
               ('nsa_t', 4 * kv), ('win_t', 2 * kv), ('dsa_t', W_DKV), ('idx_t', IDX_DIM))
    outs = pl.pallas_call(
        _inproj_prompt_kernel,
        grid=(r // tm,),
        in_specs=[pl.BlockSpec((tm, d), lambda i: (i, 0)),
                  const((1, d)), mod_spec, mod_spec,
                  pl.BlockSpec((tm, LANES), lambda i: (i % tpb, 0)),
                  pl.BlockSpec((tm, LANES), lambda i: (i % tpb, 0)),
                  pl.BlockSpec((HEAD_DIM // 2, tm), lambda i: (0, i % tpb)),
                  pl.BlockSpec((HEAD_DIM // 2, tm), lambda i: (0, i % tpb)),
                  const(wp.shape), const(wa.shape), const(wtp.shape), const(wt.shape)],
        out_specs=[tf(w) for _, w, _ in tf_outs] + [ft(w) for _, w in ft_outs],
        out_shape=[jax.ShapeDtypeStruct((r, w), dt) for _, w, dt in tf_outs]
        + [jax.ShapeDtypeStruct((nb, w, t_len), F32) for _, w in ft_outs],
        compiler_params=pltpu.CompilerParams(dimension_semantics=('arbitrary',), vmem_limit_bytes=VMEM_LIMIT),
        name='in_projection_prompt',
    )(x2d, norm_g.reshape(1, d), scale[:, None, :], shift[:, None, :], cos_tf, sin_tf, cos_t, sin_t,
      wp, wa, wtp, wt)
    return dict(zip([n for n, _, _ in tf_outs] + [n for n, _ in ft_outs], outs))


def _mixer_out_kernel(x_ref, on_ref, od_ref, snz_ref, sdz_ref, mg_ref, ga_ref, fg_ref,
                      wun_ref, wud_ref, wo_ref, y_ref):
    d = x_ref.shape[1]
    u_n = jnp.dot((on_ref[...] * snz_ref[...]).astype(BF16), wun_ref[...], preferred_element_type=F32)
    u_d = jnp.dot((od_ref[...] * sdz_ref[...]).astype(BF16), wud_ref[...], preferred_element_type=F32)
    mg = mg_ref[...]
    mix = mg[:, 0:d] * u_n + mg[:, d:2 * d] * u_d
    z = x_ref[...] + ga_ref[...] * jnp.dot(mix.astype(BF16), wo_ref[...], preferred_element_type=F32)
    y_ref[...] = z * lax.rsqrt(jnp.mean(z * z, axis=-1, keepdims=True) + EPS) * fg_ref[...]


def _mixer_out(x2d, o_n, o_d, f, gate, rows_per_mod, final_g, w_up_nsa, w_up_dsa, w_out, tm):
    r, d = x2d.shape
    if rows_per_mod == 1:
        mod_spec = pl.BlockSpec((tm, d), lambda i: (i, 0))
        ga = gate
    else:
        per = rows_per_mod // tm
        mod_spec = pl.BlockSpec((None, 1, d), lambda i: (i // per, 0, 0))
        ga = gate[:, None, :]
    const = lambda shape: pl.BlockSpec(shape, lambda i: (0,) * len(shape), pipeline_mode=pl.Buffered(1))
    tf = lambda w: pl.BlockSpec((tm, w), lambda i: (i, 0))
    return pl.pallas_call(
        _mixer_out_kernel,
        grid=(r // tm,),
        in_specs=[tf(d), tf(W_NQ), tf(W_DQ), tf(W_NQ), tf(W_DQ), tf(2 * d), mod_spec, const((1, d)),
                  const(w_up_nsa.shape), const(w_up_dsa.shape), const(w_out.shape)],
        out_specs=tf(d),
        out_shape=jax.ShapeDtypeStruct((r, d), F32),
        compiler_params=pltpu.CompilerParams(dimension_semantics=('arbitrary',), vmem_limit_bytes=VMEM_LIMIT),
        name='mixer_out',
    )(x2d, o_n, o_d, f['silu_nz'], f['silu_dz'], f['merge'], ga, final_g.reshape(1, d),
      w_up_nsa.astype(BF16), w_up_dsa.astype(BF16), w_out.astype(BF16))


POS_INF = float('inf')


_MAG_BITS = 0x7FFFFFFF


def _ordered_key(f):
    b = lax.bitcast_convert_type(f, jnp.int32)
    return jnp.where(b < 0, b ^ _MAG_BITS, b)


def _from_ordered_key(key):
    return lax.bitcast_convert_type(jnp.where(key < 0, key ^ _MAG_BITS, key), F32)


VALUE_MID_ROUNDS = 12


def _bisect_topk(count_ge, count_zero, v_min, v_max, n_valid, k):
    kf = float(k)

    def probe(lo, hi, cnt_lo, cnt_hi, x):
        c = count_ge(_from_ordered_key(x))
        up = c >= kf
        return (jnp.where(up, x, lo), jnp.where(up, hi, x), jnp.where(up, c, cnt_lo), jnp.where(up, cnt_hi, c))

    def key_mid(lo, hi):
        return (lo >> 1) + (hi >> 1) + (lo & hi & 1)

    def next_probe(lo, hi, it):
        guess = _ordered_key(0.5 * _from_ordered_key(lo) + 0.5 * _from_ordered_key(hi))
        use = jnp.logical_and(it < VALUE_MID_ROUNDS, jnp.logical_and(guess > lo, guess < hi))
        return jnp.where(use, guess, key_mid(lo, hi))

    def cond(st):
        _, _, _, _, done, it = st
        return jnp.logical_and(it < 48, jnp.min(done) < 0.5)

    def body(st):
        lo, hi, cnt_lo, cnt_hi, _, it = st
        lo, hi, cnt_lo, cnt_hi = probe(lo, hi, cnt_lo, cnt_hi, next_probe(lo, hi, it))
        lo, hi, cnt_lo, cnt_hi = probe(lo, hi, cnt_lo, cnt_hi, next_probe(lo, hi, it))
        done = jnp.where(jnp.logical_or(cnt_lo <= kf, key_mid(lo, hi) == lo), 1.0, 0.0)
        return lo, hi, cnt_lo, cnt_hi, done, it + 1

    hi0 = _ordered_key(v_max)
    lo, hi, cnt_lo, cnt_hi = probe(_ordered_key(v_min), hi0, n_valid, jnp.zeros_like(n_valid), hi0)
    is_open = key_mid(lo, hi) != lo
    c_ge0, c_gt0 = count_zero()
    zero_tie = jnp.logical_and(is_open, jnp.logical_and(c_gt0 < kf, c_ge0 >= kf))
    raise_lo = jnp.logical_and(is_open, jnp.logical_and(c_gt0 >= kf, lo < 0))
    lower_hi = jnp.logical_and(is_open, jnp.logical_and(c_ge0 < kf, hi > 0))
    to_zero_lo = jnp.logical_or(zero_tie, raise_lo)
    lo = jnp.where(to_zero_lo, 0, lo)
    cnt_lo = jnp.where(to_zero_lo, c_ge0, cnt_lo)
    hi = jnp.where(jnp.logical_or(zero_tie, lower_hi), 0, hi)
    cnt_hi = jnp.where(lower_hi, c_ge0, cnt_hi)
    done0 = jnp.where(jnp.logical_or(n_valid <= kf, jnp.logical_or(cnt_lo <= kf, key_mid(lo, hi) == lo)),
                      1.0, 0.0)
    lo, _, cnt_lo, _, _, _ = lax.while_loop(cond, body, (lo, hi, cnt_lo, cnt_hi, done0, jnp.int32(0)))
    return jnp.where(n_valid <= kf, NEG, _from_ordered_key(lo)), cnt_lo


def _topk_cap_rows(s, row_min, row_max, n_valid, k):
    rows, n = s.shape
    kf = float(k)
    count_ge = lambda x: jnp.sum(jnp.where(s >= x, 1.0, 0.0), axis=1, keepdims=True)
    count_zero = lambda: (jnp.sum(jnp.where(s >= 0.0, 1.0, 0.0), axis=1, keepdims=True),
                          jnp.sum(jnp.where(s > 0.0, 1.0, 0.0), axis=1, keepdims=True))
    t, _ = _bisect_topk(count_ge, count_zero, row_min, row_max, n_valid, k)
    eq = jnp.where(s == t, 1.0, 0.0)
    need = kf - jnp.sum(jnp.where(s > t, 1.0, 0.0), axis=1, keepdims=True)
    n_grp = n // LANES
    r_i = lax.broadcasted_iota(jnp.int32, (LANES, LANES), 0)
    c_i = lax.broadcasted_iota(jnp.int32, (LANES, LANES), 1)
    upper = jnp.where(r_i < c_i, 1.0, 0.0).astype(BF16)
    stacked = jnp.concatenate([eq[:, j * LANES:(j + 1) * LANES] for j in range(n_grp)], axis=0)
    within = jnp.dot(stacked.astype(BF16), upper, preferred_element_type=F32)
    run = jnp.zeros((rows, 1), F32)
    caps = []
    for j in range(n_grp):
        sl = slice(j * LANES, (j + 1) * LANES)
        before = run + within[j * rows:(j + 1) * rows]
        sj = s[:, sl]
        keep = jnp.logical_or(sj > t, jnp.logical_and(sj == t, before < need))
        caps.append(jnp.where(keep, POS_INF, NEG))
        run = run + jnp.sum(eq[:, sl], axis=1, keepdims=True)
    return jnp.concatenate(caps, axis=1)


ACC_ROWS = 64


def _col_reduce(x, op):
    n, w = x.shape
    if n > ACC_ROWS and n % ACC_ROWS == 0:
        x = op(x.reshape(n // ACC_ROWS, ACC_ROWS, w), axis=0)
    return op(x, axis=0, keepdims=True)


def _count_ge_t(st_scr, x, nk, kc):
    tq = st_scr.shape[1]

    def body(c, acc):
        k0 = pl.multiple_of(c * kc, kc)
        hit = jnp.where(st_scr[pl.ds(k0, kc), :] >= x, 1.0, 0.0)
        return acc + jnp.sum(hit.reshape(kc // ACC_ROWS, ACC_ROWS, tq), axis=0)

    acc = lax.fori_loop(0, nk, body, jnp.zeros((ACC_ROWS, tq), F32))
    return jnp.sum(acc, axis=0, keepdims=True)


def _topk_threshold_t(st_scr, col_min, col_max, n_valid, k, nk, kc):
    tq = st_scr.shape[1]

    def count_zero():
        def body(c, carry):
            ge, gt = carry
            s = st_scr[pl.ds(pl.multiple_of(c * kc, kc), kc), :]
            ge = ge + jnp.sum(jnp.where(s >= 0.0, 1.0, 0.0).reshape(kc // ACC_ROWS, ACC_ROWS, tq), axis=0)
            gt = gt + jnp.sum(jnp.where(s > 0.0, 1.0, 0.0).reshape(kc // ACC_ROWS, ACC_ROWS, tq), axis=0)
            return ge, gt
        z = jnp.zeros((ACC_ROWS, tq), F32)
        ge, gt = lax.fori_loop(0, nk, body, (z, z))
        return jnp.sum(ge, axis=0, keepdims=True), jnp.sum(gt, axis=0, keepdims=True)

    return _bisect_topk(lambda x: _count_ge_t(st_scr, x, nk, kc), count_zero, col_min, col_max, n_valid, k)


def _write_topk_cap_t(st_scr, t, cnt_ge, k, nk, kc):
    kf = float(k)
    has_tie = jnp.max(cnt_ge) > kf

    @pl.when(jnp.logical_not(has_tie))
    def _():
        def body(c, _):
            sl = pl.ds(pl.multiple_of(c * kc, kc), kc)
            st_scr[sl, :] = jnp.where(st_scr[sl, :] >= t, POS_INF, NEG)
            return 0
        lax.fori_loop(0, nk, body, 0)

    @pl.when(has_tie)
    def _():
        def cnt_body(c, acc):
            sl = pl.ds(pl.multiple_of(c * kc, kc), kc)
            return acc + jnp.sum(jnp.where(st_scr[sl, :] > t, 1.0, 0.0), axis=0, keepdims=True)
        cnt_gt = lax.fori_loop(0, nk, cnt_body, jnp.zeros_like(t))
        need = kf - cnt_gt
        r_i = lax.broadcasted_iota(jnp.int32, (kc, kc), 0)
        c_i = lax.broadcasted_iota(jnp.int32, (kc, kc), 1)
        lower = jnp.where(c_i < r_i, 1.0, 0.0).astype(BF16)

        def body(c, run):
            sl = pl.ds(pl.multiple_of(c * kc, kc), kc)
            s = st_scr[sl, :]
            eq = jnp.where(s == t, 1.0, 0.0)
            before = run + jnp.dot(lower, eq.astype(BF16), preferred_element_type=F32)
            keep = jnp.logical_or(s > t, jnp.logical_and(s == t, before < need))
            st_scr[sl, :] = jnp.where(keep, POS_INF, NEG)
            return run + jnp.sum(eq, axis=0, keepdims=True)
        lax.fori_loop(0, nk, body, jnp.zeros_like(t))


LOG2_E = float(np.log2(np.e))


def _q_pad(qt, g, hg, scale):
    qg = jnp.concatenate([qt[(g * hg + j) * HEAD_DIM:(g * hg + j + 1) * HEAD_DIM, :] for j in range(hg)],
                         axis=1) * (scale * LOG2_E)
    z = jnp.zeros_like(qg)
    parts = [z] * NSA_KV_HEADS
    parts[g] = qg
    return jnp.concatenate(parts, axis=0).astype(BF16)


def _flash_t(q_pads, k_ref, vt_ref, vrow0s, cap_fns, k_base, n_chunks, kc, n_rep):
    n = q_pads[0].shape[1]
    n_g = len(q_pads)
    ones = jnp.ones((8, kc), BF16)

    def body(c, carry):
        k0 = pl.multiple_of(k_base + c * kc, LANES)
        k_chunk = k_ref[pl.ds(k0, kc), :]
        out = []
        for g in range(n_g):
            m, acc = carry[g]
            s = jnp.dot(k_chunk, q_pads[g], preferred_element_type=F32)
            s = jnp.minimum(s, jnp.concatenate([cap_fns[g](k0)] * n_rep, axis=1))
            m_new = jnp.maximum(m, _col_reduce(s, jnp.max))
            p = jnp.exp2((s - m_new).astype(BF16))
            vt = jnp.concatenate([vt_ref[pl.ds(vrow0s[g], HEAD_DIM), pl.ds(k0, kc)].astype(BF16), ones], axis=0)
            acc = jnp.exp2(m - m_new) * acc + jnp.dot(vt, p, preferred_element_type=F32)
            out.append((m_new, acc))
        return tuple(out)

    init = tuple((jnp.full((1, n), NEG, F32), jnp.zeros((HEAD_DIM + 8, n), F32)) for _ in range(n_g))
    fin = lax.fori_loop(0, n_chunks, body, init)
    return [acc[0:HEAD_DIM] / jnp.maximum(acc[HEAD_DIM:HEAD_DIM + 1], 1e-30) for _, acc in fin]


def _store_heads(o_ref, o_t, g, hg, tq):
    for j in range(hg):
        h = g * hg + j
        o_ref[:, h * HEAD_DIM:(h + 1) * HEAD_DIM] = o_t[:, j * tq:(j + 1) * tq].T


def _dsa_prompt_kernel(iqt_ref, smallt_ref, dqt_ref, idx3_ref, k_ref, kvt_ref, o_ref, st_scr,
                       *, tq, kc, topk):
    qs = pl.program_id(1) * tq
    nk = (qs + tq + kc - 1) // kc
    qpos = qs + lax.broadcasted_iota(jnp.int32, (1, tq), 1)

    iqt = iqt_ref[...]
    q3 = []
    for h in range(IDX_HEADS):
        qh = iqt[h * IDX_DIM:(h + 1) * IDX_DIM, :]
        q_hi = qh.astype(BF16)
        q_lo = (qh - q_hi.astype(F32)).astype(BF16)
        q3.append(jnp.concatenate([q_hi, q_hi, q_lo, jnp.zeros_like(q_hi)], axis=0))
    wt = smallt_ref[...]
    out_scale = (IDX_DIM ** -0.5) * (IDX_HEADS ** -0.5)

    def score_body(c, carry):
        mn, mx = carry
        k0 = pl.multiple_of(c * kc, kc)
        k3 = idx3_ref[pl.ds(k0, kc), :]
        sc = jnp.zeros((kc, tq), F32)
        for h in range(IDX_HEADS):
            d = jnp.dot(k3, q3[h], preferred_element_type=F32)
            sc = sc + jnp.maximum(d, 0.0) * wt[h:h + 1, :]
        sc = sc * out_scale
        ok = (k0 + lax.broadcasted_iota(jnp.int32, (kc, 1), 0)) <= qpos
        st_scr[pl.ds(k0, kc), :] = jnp.where(ok, sc, -POS_INF)
        mn = jnp.minimum(mn, _col_reduce(jnp.where(ok, sc, POS_INF), jnp.min))
        mx = jnp.maximum(mx, _col_reduce(jnp.where(ok, sc, NEG), jnp.max))
        return mn, mx

    mn, mx = lax.fori_loop(0, nk, score_body,
                           (jnp.full((1, tq), POS_INF, F32), jnp.full((1, tq), NEG, F32)))
    t, cnt = _topk_threshold_t(st_scr, mn, mx, (qpos + 1).astype(F32), topk, nk, kc)
    _write_topk_cap_t(st_scr, t, cnt, topk, nk, kc)

    hg = DSA_HEADS // DSA_KV_HEADS
    dqt = dqt_ref[...]
    groups = range(DSA_KV_HEADS)
    cap_fn = lambda k0: st_scr[pl.ds(k0, kc), :]
    o_ts = _flash_t([_q_pad(dqt, g, hg, HEAD_DIM ** -0.5) for g in groups], k_ref, kvt_ref,
                    [(DSA_KV_HEADS + g) * HEAD_DIM for g in groups], [cap_fn] * DSA_KV_HEADS, 0, nk, kc, hg)
    for g in groups:
        _store_heads(o_ref, o_ts[g], g, hg, tq)


def _dsa_prompt(f, bx, t_len, tq, kc):
    topk = min(DSA_TOPK_MAX, t_len // 4)
    nq = t_len // tq
    hg = DSA_HEADS // DSA_KV_HEADS
    row = lambda w: pl.BlockSpec((tq, w), lambda b, i: (b * nq + i, 0))
    qt = lambda rows: pl.BlockSpec((None, rows, tq), lambda b, i: (b, 0, i))
    keys = lambda w: pl.BlockSpec((t_len, w), lambda b, i: (b, 0))
    per_b = lambda rows: pl.BlockSpec((None, rows, t_len), lambda b, i: (b, 0, 0))
    return pl.pallas_call(
        functools.partial(_dsa_prompt_kernel, tq=tq, kc=kc, topk=topk),
        grid=(bx, nq),
        in_specs=[qt(W_IQ), qt(SMALL_ROWS), qt(W_DQ), keys(2 * LANES), keys(DSA_KV_HEADS * HEAD_DIM),
                  per_b(W_DKV)],
        out_specs=row(W_DQ),
        out_shape=jax.ShapeDtypeStruct((bx * t_len, W_DQ), F32),
        scratch_shapes=[pltpu.VMEM((t_len, tq), F32)],
        compiler_params=pltpu.CompilerParams(dimension_semantics=('arbitrary', 'arbitrary'),
                                             vmem_limit_bytes=VMEM_LIMIT),
        name='dsa_prompt',
    )(f['idx_q_t'], f['small_t'], f['dsa_q_t'], f['idx3'], f['dsa_k'], f['dsa_t'])


GROUP_TOKENS = CMP_STRIDE
N_HALVES = CMP_BLOCK // CMP_STRIDE
CMP_HID = 2 * HEAD_DIM


def _prep_compress_weights(cmp_pe, cmp_w1, cmp_w2):
    g_n = NSA_KV_HEADS
    eye = jnp.eye(g_n, dtype=F32)
    out = []
    for s in range(2):
        w1 = cmp_w1[s].reshape(N_HALVES, GROUP_TOKENS, HEAD_DIM, CMP_HID)
        w1e = jnp.einsum('ardj,gh->arghdj', w1, eye)
        w1e = w1e.transpose(0, 1, 2, 4, 3, 5).reshape(N_HALVES, GROUP_TOKENS * g_n * HEAD_DIM, g_n * CMP_HID)
        pe = cmp_pe[s].reshape(N_HALVES, GROUP_TOKENS, 1, HEAD_DIM)
        pe = jnp.broadcast_to(pe, (N_HALVES, GROUP_TOKENS, g_n, HEAD_DIM)).reshape(N_HALVES, 1, -1)
        w2e = jnp.einsum('jd,gh->gjhd', cmp_w2[s], eye).reshape(g_n * CMP_HID, g_n * HEAD_DIM)
        out.append((w1e, pe, w2e))
    (w1k, pek, w2k), (w1v, pev, w2v) = out
    w1k3 = jnp.stack([_split3_rows(w1k[a]) for a in range(N_HALVES)])
    return (w1k3, pek, _split3_rows(w2k), w1v.astype(BF16), pev, w2v.astype(BF16))


def _compress_slot(x_ref, m, pe_ref, w1_ref, w2_ref, shift_scr, precise):
    xr = jnp.concatenate([x_ref[pl.ds(r, m, stride=GROUP_TOKENS), :] for r in range(GROUP_TOKENS)], axis=1)
    halves = []
    for a in range(N_HALVES):
        xa = xr + pe_ref[a]
        lhs = _split3_cols(xa) if precise else xa.astype(BF16)
        halves.append(jnp.dot(lhs, w1_ref[a], preferred_element_type=F32))
    shift_scr[0:m, :] = halves[1]
    shift_scr[m:m + 8, :] = jnp.zeros((8, shift_scr.shape[1]), F32)
    hid = _silu(halves[0] + shift_scr[pl.ds(1, m), :])
    lhs = _split3_cols(hid) if precise else hid.astype(BF16)
    return jnp.dot(lhs, w2_ref[...], preferred_element_type=F32)


def _compress_kernel(xk_ref, xv_ref, w1k_ref, pek_ref, w2k_ref, w1v_ref, pev_ref, w2v_ref,
                     ck_ref, cv_ref, shift_scr, *, m):
    n_pad = ck_ref.shape[0]
    if n_pad > m:
        ck_ref[m:n_pad, :] = jnp.zeros((n_pad - m, ck_ref.shape[1]), F32)
        cv_ref[m:n_pad, :] = jnp.zeros((n_pad - m, cv_ref.shape[1]), F32)
    ck_ref[0:m, :] = _compress_slot(xk_ref, m, pek_ref, w1k_ref, w2k_ref, shift_scr, True)
    cv_ref[0:m, :] = _compress_slot(xv_ref, m, pev_ref, w1v_ref, w2v_ref, shift_scr, False)


def _compress_prompt(cmp_k, cmp_v, cw, bx, t_len):
    m = t_len // GROUP_TOKENS
    n_pad = -(-m // LANES) * LANES
    kvw = NSA_KV_HEADS * HEAD_DIM
    const = lambda a: pl.BlockSpec(a.shape, lambda b: (0,) * a.ndim, pipeline_mode=pl.Buffered(1))
    x_spec = pl.BlockSpec((t_len, kvw), lambda b: (b, 0))
    o_spec = pl.BlockSpec((None, n_pad, kvw), lambda b: (b, 0, 0))
    return pl.pallas_call(
        functools.partial(_compress_kernel, m=m),
        grid=(bx,),
        in_specs=[x_spec, x_spec] + [const(a) for a in cw],
        out_specs=[o_spec, o_spec],
        out_shape=[jax.ShapeDtypeStruct((bx, n_pad, kvw), F32)] * 2,
        scratch_shapes=[pltpu.VMEM((m + 8, NSA_KV_HEADS * CMP_HID), F32)],
        compiler_params=pltpu.CompilerParams(dimension_semantics=('arbitrary',), vmem_limit_bytes=VMEM_LIMIT),
        name='compress_prompt',
    )(cmp_k, cmp_v, *cw)


SEL_SHIFT = 6


def _stack_heads(x, g, hg, scale):
    return jnp.concatenate(
        [x[:, (g * hg + j) * HEAD_DIM:(g * hg + j + 1) * HEAD_DIM] for j in range(hg)], axis=0) * scale


def _cmp_branch(q_rows, ck_g, cv_g, qpos_rep, hg, tq):
    n_pad = ck_g.shape[0]
    nt = (((1,), (1,)), ((), ()))
    ck_hi = ck_g.astype(BF16)
    ck_lo = (ck_g - ck_hi.astype(F32)).astype(BF16)
    ck3 = jnp.concatenate([ck_hi, ck_lo, ck_hi], axis=1)
    s = lax.dot_general(_split3_cols(q_rows), ck3, nt, preferred_element_type=F32)
    cmp_end = lax.broadcasted_iota(jnp.int32, (1, n_pad), 1) * CMP_STRIDE + (CMP_BLOCK - 1)
    vis = cmp_end <= qpos_rep
    s = jnp.where(vis, s, NEG)
    p = jnp.where(vis, jnp.exp(s - jnp.max(s, axis=1, keepdims=True)), 0.0)
    p = p / jnp.maximum(jnp.sum(p, axis=1, keepdims=True), 1e-30)
    o_cmp = jnp.dot(p.astype(BF16), cv_g.astype(BF16), preferred_element_type=F32)
    p_sum = p[0:tq]
    for j in range(1, hg):
        p_sum = p_sum + p[j * tq:(j + 1) * tq]
    return o_cmp, p_sum


def _select_blocks(p_sum, qpos, n_slc, n_sel):
    tq, n_pad = p_sum.shape
    n_i = lax.broadcasted_iota(jnp.int32, (n_pad, LANES), 0) * CMP_STRIDE
    j_i = lax.broadcasted_iota(jnp.int32, (n_pad, LANES), 1)
    overlap = jnp.logical_and(n_i < (j_i + 1) * SEL_BLOCK, n_i + CMP_BLOCK > j_i * SEL_BLOCK)
    overlap = jnp.where(jnp.logical_and(overlap, j_i < n_slc), 1.0, 0.0).astype(BF16)
    p_hi = p_sum.astype(BF16)
    r1 = p_sum - p_hi.astype(F32)
    p_mid = r1.astype(BF16)
    p_lo = (r1 - p_mid.astype(F32)).astype(BF16)
    imp = (jnp.dot(p_hi, overlap, preferred_element_type=F32)
           + jnp.dot(p_mid, overlap, preferred_element_type=F32)
           + jnp.dot(p_lo, overlap, preferred_element_type=F32))
    jb = lax.broadcasted_iota(jnp.int32, (1, LANES), 1)
    cur = qpos >> SEL_SHIFT
    forced = jnp.logical_or(jb == 0, jnp.logical_or(jb == cur, jb == cur - 1))
    imp = jnp.where(forced, FORCED_SCORE, imp)
    imp = jnp.where(jb > cur, NEG, imp)
    rank = jnp.zeros((tq, LANES), F32)
    for i in range(n_slc):
        vi = imp[:, i:i + 1]
        ahead = jnp.logical_or(vi > imp, jnp.logical_and(vi == imp, jb > i))
        rank = rank + jnp.where(ahead, 1.0, 0.0)
    return jnp.where(jnp.logical_and(rank < float(n_sel), imp > 0.5 * NEG), 1.0, 0.0)


def _select_blocks_t(p_sum, qpos_t, n_slc, n_sel):
    tq, n_pad = p_sum.shape
    nb = -(-n_slc // 8) * 8
    j_i = lax.broadcasted_iota(jnp.int32, (nb, n_pad), 0)
    n_i = lax.broadcasted_iota(jnp.int32, (nb, n_pad), 1) * CMP_STRIDE
    overlap = jnp.logical_and(n_i < (j_i + 1) * SEL_BLOCK, n_i + CMP_BLOCK > j_i * SEL_BLOCK)
    overlap = jnp.where(jnp.logical_and(overlap, j_i < n_slc), 1.0, 0.0).astype(BF16)
    p_t = p_sum.T
    p_hi = p_t.astype(BF16)
    r1 = p_t - p_hi.astype(F32)
    p_mid = r1.astype(BF16)
    p_lo = (r1 - p_mid.astype(F32)).astype(BF16)
    imp = (jnp.dot(overlap, p_hi, preferred_element_type=F32)
           + jnp.dot(overlap, p_mid, preferred_element_type=F32)
           + jnp.dot(overlap, p_lo, preferred_element_type=F32))
    jb = lax.broadcasted_iota(jnp.int32, (nb, 1), 0)
    cur = qpos_t >> SEL_SHIFT
    forced = jnp.logical_or(jb == 0, jnp.logical_or(jb == cur, jb == cur - 1))
    imp = jnp.where(forced, FORCED_SCORE, imp)
    imp = jnp.where(jnp.logical_or(jb > cur, jb >= n_slc), NEG, imp)
    rank = jnp.zeros((nb, tq), F32)
    for i in range(n_slc):
        vi = imp[i:i + 1, :]
        ahead = jnp.logical_or(vi > imp, jnp.logical_and(vi == imp, jb > i))
        rank = rank + jnp.where(ahead, 1.0, 0.0)
    return jnp.where(jnp.logical_and(rank < float(n_sel), imp > 0.5 * NEG), 1.0, 0.0)


def _nsa_prompt_kernel(q_ref, qrt_ref, smallt_ref, ck_ref, cv_ref, slck_ref, wink_ref, slct_ref, wint_ref,
                       o_ref, capt_scr, *, tq, kc, kcw, n_slc, n_sel):
    qs = pl.program_id(1) * tq
    nk = (qs + tq + kc - 1) // kc
    hg = NSA_HEADS // NSA_KV_HEADS
    kvw = NSA_KV_HEADS * HEAD_DIM
    scale = HEAD_DIM ** -0.5
    qpos = qs + lax.broadcasted_iota(jnp.int32, (tq, 1), 0)
    qpos_t = qs + lax.broadcasted_iota(jnp.int32, (1, tq), 1)
    qpos_rep = jnp.concatenate([qpos] * hg, axis=0)
    q = q_ref[...]
    qrt = qrt_ref[...]
    gates_t = _sigmoid(smallt_ref[...])
    ck = ck_ref[...]
    cv = cv_ref[...]
    nb = -(-n_slc // 8) * 8
    blk_l = lax.broadcasted_iota(jnp.int32, (1, nb), 1)

    def win_cap(k0):
        dist = qpos_t - (k0 + lax.broadcasted_iota(jnp.int32, (kcw, 1), 0))
        return jnp.where(jnp.logical_and(dist >= 0, dist < WINDOW), POS_INF, NEG)

    win_base = jnp.maximum(qs + tq - kcw, 0)

    groups = range(NSA_KV_HEADS)
    o_cmps, sel_ts = [], []
    for g in groups:
        o_cmp, p_sum = _cmp_branch(_stack_heads(q, g, hg, scale), ck[:, g * HEAD_DIM:(g + 1) * HEAD_DIM],
                                   cv[:, g * HEAD_DIM:(g + 1) * HEAD_DIM], qpos_rep, hg, tq)
        o_cmps.append(o_cmp)
        sel_ts.append(_select_blocks_t(p_sum, qpos_t, n_slc, n_sel).astype(BF16))

    def cap_body(c, _):
        k0 = pl.multiple_of(c * kc, kc)
        kpos = k0 + lax.broadcasted_iota(jnp.int32, (kc, 1), 0)
        expand = jnp.where((kpos >> SEL_SHIFT) == blk_l, 1.0, 0.0).astype(BF16)
        causal = kpos <= qpos_t
        for g in groups:
            sel_k = jnp.dot(expand, sel_ts[g], preferred_element_type=F32)
            capt_scr[g, pl.ds(k0, kc), :] = jnp.where(jnp.logical_and(sel_k > 0.5, causal), POS_INF, NEG)
        return 0

    lax.fori_loop(0, nk, cap_body, 0)
    q_pads = [_q_pad(qrt, g, hg, scale) for g in groups]
    vrows = [kvw + g * HEAD_DIM for g in groups]
    o_slcs = _flash_t(q_pads, slck_ref, slct_ref, vrows,
                      [functools.partial(lambda g, k0: capt_scr[g, pl.ds(k0, kc), :], g) for g in groups],
                      0, nk, kc, hg)
    o_wins = _flash_t(q_pads, wink_ref, wint_ref, vrows, [win_cap] * NSA_KV_HEADS, win_base, 1, kcw, hg)
    for g in groups:
        for j in range(hg):
            h = g * hg + j
            r = IDX_HEADS + 3 * h
            cols = slice(j * tq, (j + 1) * tq)
            o_t = (gates_t[r:r + 1, :] * o_cmps[g][cols].T + gates_t[r + 1:r + 2, :] * o_slcs[g][:, cols]
                   + gates_t[r + 2:r + 3, :] * o_wins[g][:, cols])
            o_ref[:, h * HEAD_DIM:(h + 1) * HEAD_DIM] = o_t.T


def _nsa_prompt(f, ck, cv, bx, t_len, tq, kc, kcw):
    nq = t_len // tq
    hg = NSA_HEADS // NSA_KV_HEADS
    n_pad = ck.shape[1]
    n_slc = -(-t_len // SEL_BLOCK)
    kvw = NSA_KV_HEADS * HEAD_DIM
    row = lambda w: pl.BlockSpec((tq, w), lambda b, i: (b * nq + i, 0))
    cmp_spec = pl.BlockSpec((None, n_pad, kvw), lambda b, i: (b, 0, 0))
    keys = pl.BlockSpec((t_len, kvw), lambda b, i: (b, 0))
    return pl.pallas_call(
        functools.partial(_nsa_prompt_kernel, tq=tq, kc=kc, kcw=kcw, n_slc=n_slc, n_sel=min(N_SEL, n_slc)),
        grid=(bx, nq),
        in_specs=[row(W_NQ), pl.BlockSpec((None, W_NQ, tq), lambda b, i: (b, 0, i)),
                  pl.BlockSpec((None, SMALL_ROWS, tq), lambda b, i: (b, 0, i)),
                  cmp_spec, cmp_spec, keys, keys,
                  pl.BlockSpec((None, 2 * kvw, t_len), lambda b, i: (b, 1, 0)),
                  pl.BlockSpec((None, 2 * kvw, t_len), lambda b, i: (b, 0, 0))],
        out_specs=row(W_NQ),
        out_shape=jax.ShapeDtypeStruct((bx * t_len, W_NQ), F32),
        scratch_shapes=[pltpu.VMEM((NSA_KV_HEADS, t_len, tq), F32)],
        compiler_params=pltpu.CompilerParams(dimension_semantics=('arbitrary', 'arbitrary'),
                                             vmem_limit_bytes=VMEM_LIMIT),
        name='nsa_prompt',
    )(f['nsa_q'], f['nsa_q_rot_t'], f['small_t'], ck, cv, f['slc_k'], f['win_k'], f['nsa_t'], f['win_t'])


DEC_ROWS = 8


def _page_copies(cache_hbm, pt_ref, b, n_pages, page, buf, sem, slot):
    return [pltpu.make_async_copy(cache_hbm.at[pt_ref[b * n_pages + j]],
                                  buf.at[slot, :, pl.ds(j * page, page)], sem.at[slot])
            for j in range(n_pages)]


def _gather_step(caches, pt_ref, n_pages, page, bufs, sems):
    b = pl.program_id(0)
    nb = pl.num_programs(0)
    slot = b % 2

    @pl.when(b == 0)
    def _():
        for cache, buf, sem in zip(caches, bufs, sems):
            for cp in _page_copies(cache, pt_ref, 0, n_pages, page, buf, sem, 0):
                cp.start()

    @pl.when(b + 1 < nb)
    def _():
        for cache, buf, sem in zip(caches, bufs, sems):
            for cp in _page_copies(cache, pt_ref, b + 1, n_pages, page, buf, sem, 1 - slot):
                cp.start()

    for cache, buf, sem in zip(caches, bufs, sems):
        for cp in _page_copies(cache, pt_ref, b, n_pages, page, buf, sem, slot):
            cp.wait()
    return slot


def _attend_full(q_rows, kt, vt, cap, n_rep):
    nt = (((1,), (1,)), ((), ()))
    s = jnp.dot(q_rows, kt.astype(BF16), preferred_element_type=F32)
    s = jnp.minimum(s, jnp.concatenate([cap] * n_rep, axis=0))
    p = jnp.exp(s - jnp.max(s, axis=1, keepdims=True))
    den = jnp.maximum(jnp.sum(p, axis=1, keepdims=True), 1e-30)
    return lax.dot_general(p.astype(BF16), vt.astype(BF16), nt, preferred_element_type=F32) / den


def _decode_nsa_kernel(pt_ref, q_ref, qr_ref, small_ref, cache_hbm, new_ref, wst_ref, wnew_ref,
                       w1k_ref, pek_ref, w2k_ref, w1v_ref, pev_ref, w2v_ref, o_ref,
                       buf, sem, xk_scr, xv_scr, shift_scr, *, past, t_dec, n_pages, page):
    slot = _gather_step([cache_hbm], pt_ref, n_pages, page, [buf], [sem])
    rows = DEC_ROWS
    hg = NSA_HEADS // NSA_KV_HEADS
    kvw = NSA_KV_HEADS * HEAD_DIM
    scale = HEAD_DIM ** -0.5
    n_keys = buf.shape[2]
    buf[slot, :, pl.ds(past, LANES)] = new_ref[...]
    qpos = past + jnp.minimum(lax.broadcasted_iota(jnp.int32, (rows, 1), 0), t_dec - 1)
    qpos_rep = jnp.concatenate([qpos] * hg, axis=0)

    for j in range(n_pages):
        xk_scr[j * page:(j + 1) * page, :] = buf[slot, 0:kvw, j * page:(j + 1) * page].T
        xv_scr[j * page:(j + 1) * page, :] = buf[slot, kvw:2 * kvw, j * page:(j + 1) * page].T
    m = past // GROUP_TOKENS
    ck = _compress_slot(xk_scr, m, pek_ref, w1k_ref, w2k_ref, shift_scr, True)
    cv = _compress_slot(xv_scr, m, pev_ref, w1v_ref, w2v_ref, shift_scr, False)

    q = q_ref[...]
    qr = qr_ref[...]
    gates = _sigmoid(small_ref[...])
    n_slc = -(-(past + t_dec) // SEL_BLOCK)
    kpos = lax.broadcasted_iota(jnp.int32, (1, n_keys), 1)
    blk_i = lax.broadcasted_iota(jnp.int32, (LANES, 1), 0)
    expand = jnp.where((kpos >> SEL_SHIFT) == blk_i, 1.0, 0.0).astype(BF16)
    wb = wst_ref.shape[1]
    wpos = past - wb + lax.broadcasted_iota(jnp.int32, (1, wb + LANES), 1)
    wdist = qpos - wpos
    win_cap = jnp.where(jnp.logical_and(wdist >= 0, wdist < WINDOW), POS_INF, NEG)

    for g in range(NSA_KV_HEADS):
        gs = slice(g * HEAD_DIM, (g + 1) * HEAD_DIM)
        o_cmp, p_sum = _cmp_branch(_stack_heads(q, g, hg, scale), ck[:, gs], cv[:, gs], qpos_rep, hg, rows)
        sel = _select_blocks(p_sum, qpos, n_slc, min(N_SEL, n_slc)).astype(BF16)
        sel_k = jnp.dot(sel, expand, preferred_element_type=F32)
        slc_cap = jnp.where(jnp.logical_and(sel_k > 0.5, kpos <= qpos), POS_INF, NEG)
        qr_rows = _stack_heads(qr, g, hg, scale).astype(BF16)
        o_slc = _attend_full(qr_rows, buf[slot, 2 * kvw + g * HEAD_DIM:2 * kvw + (g + 1) * HEAD_DIM, :],
                             buf[slot, 3 * kvw + g * HEAD_DIM:3 * kvw + (g + 1) * HEAD_DIM, :], slc_cap, hg)
        wk = jnp.concatenate([wst_ref[g * HEAD_DIM:(g + 1) * HEAD_DIM, :],
                              wnew_ref[g * HEAD_DIM:(g + 1) * HEAD_DIM, :]], axis=1)
        wv = jnp.concatenate([wst_ref[kvw + g * HEAD_DIM:kvw + (g + 1) * HEAD_DIM, :],
                              wnew_ref[kvw + g * HEAD_DIM:kvw + (g + 1) * HEAD_DIM, :]], axis=1)
        o_win = _attend_full(qr_rows, wk, wv, win_cap, hg)
        for j in range(hg):
            h = g * hg + j
            lane = IDX_HEADS + 3 * h
            rs = slice(j * rows, (j + 1) * rows)
            o_ref[:, h * HEAD_DIM:(h + 1) * HEAD_DIM] = (gates[:, lane:lane + 1] * o_cmp[rs]
                                                         + gates[:, lane + 1:lane + 2] * o_slc[rs]
                                                         + gates[:, lane + 2:lane + 3] * o_win[rs])


def _decode_dsa_kernel(pt_ref, iq_ref, small_ref, dq_ref, idx_hbm, kv_hbm, inew_ref, kvnew_ref, o_ref,
                       ibuf, kbuf, isem, ksem, *, past, t_dec, n_pages, page, topk):
    slot = _gather_step([idx_hbm, kv_hbm], pt_ref, n_pages, page, [ibuf, kbuf], [isem, ksem])
    rows = DEC_ROWS
    hg = DSA_HEADS // DSA_KV_HEADS
    scale = HEAD_DIM ** -0.5
    n_keys = ibuf.shape[2]
    ibuf[slot, :, pl.ds(past, LANES)] = inew_ref[...]
    kbuf[slot, :, pl.ds(past, LANES)] = kvnew_ref[...]
    qpos = past + jnp.minimum(lax.broadcasted_iota(jnp.int32, (rows, 1), 0), t_dec - 1)

    ik = ibuf[slot]
    ik_hi = ik.astype(BF16)
    ik_lo = (ik - ik_hi.astype(F32)).astype(BF16)
    k3 = jnp.concatenate([ik_hi, ik_lo, ik_hi], axis=0)
    iq = iq_ref[...]
    w = small_ref[...]
    sc = jnp.zeros((rows, n_keys), F32)
    for h in range(IDX_HEADS):
        d = jnp.dot(_split3_cols(iq[:, h * IDX_DIM:(h + 1) * IDX_DIM]), k3, preferred_element_type=F32)
        sc = sc + jnp.maximum(d, 0.0) * w[:, h:h + 1]
    sc = sc * ((IDX_DIM ** -0.5) * (IDX_HEADS ** -0.5))
    kpos = lax.broadcasted_iota(jnp.int32, (1, n_keys), 1)
    real_row = lax.broadcasted_iota(jnp.int32, (rows, 1), 0) < t_dec
    ok = jnp.logical_and(kpos <= qpos, jnp.logical_or(real_row, kpos == 0))
    mn = jnp.min(jnp.where(ok, sc, POS_INF), axis=1, keepdims=True)
    mx = jnp.max(jnp.where(ok, sc, NEG), axis=1, keepdims=True)
    n_valid = jnp.where(real_row, qpos + 1, 1).astype(F32)
    cap = _topk_cap_rows(jnp.where(ok, sc, -POS_INF), mn, mx, n_valid, topk)

    dq = dq_ref[...]
    for g in range(DSA_KV_HEADS):
        q_rows = _stack_heads(dq, g, hg, scale).astype(BF16)
        o = _attend_full(q_rows, kbuf[slot, g * HEAD_DIM:(g + 1) * HEAD_DIM, :],
                         kbuf[slot, (DSA_KV_HEADS + g) * HEAD_DIM:(DSA_KV_HEADS + g + 1) * HEAD_DIM, :], cap, hg)
        for j in range(hg):
            h = g * hg + j
            o_ref[:, h * HEAD_DIM:(h + 1) * HEAD_DIM] = o[j * rows:(j + 1) * rows]


def _pad_rows(a2d, bd, t_dec):
    return jnp.pad(a2d.reshape(bd, t_dec, -1), ((0, 0), (0, DEC_ROWS - t_dec), (0, 0)))


def _new_cols(a_t, bd, t_dec):
    f = a_t.shape[0]
    return jnp.pad(a_t.reshape(f, bd, t_dec).transpose(1, 0, 2), ((0, 0), (0, 0), (0, LANES - t_dec)))


def _feature_major_pages(pool):
    n_pool, page = pool.shape[:2]
    nd = pool.ndim
    return pool.transpose((0,) + tuple(range(2, nd)) + (1,)).reshape(n_pool, -1, page)


def _decode_nsa(fs, cache_nsa, win_state, page_table, cw, past, t_dec):
    bd, n_pages = page_table.shape
    pool = _feature_major_pages(cache_nsa)
    page = pool.shape[2]
    f = pool.shape[1]
    n_keys = past + LANES
    wst = _feature_major_pages(win_state)
    kvw = NSA_KV_HEADS * HEAD_DIM
    per_b = lambda *shape: pl.BlockSpec((None,) + shape, lambda b, pt: (b,) + (0,) * len(shape))
    const = lambda a: pl.BlockSpec(a.shape, lambda b, pt: (0,) * a.ndim, pipeline_mode=pl.Buffered(1))
    grid_spec = pltpu.PrefetchScalarGridSpec(
        num_scalar_prefetch=1,
        grid=(bd,),
        in_specs=[per_b(DEC_ROWS, W_NQ), per_b(DEC_ROWS, W_NQ), per_b(DEC_ROWS, LANES),
                  pl.BlockSpec(memory_space=pl.ANY), per_b(f, LANES),
                  per_b(2 * kvw, wst.shape[2]), per_b(2 * kvw, LANES)] + [const(a) for a in cw],
        out_specs=per_b(DEC_ROWS, W_NQ),
        scratch_shapes=[pltpu.VMEM((2, f, n_keys), F32), pltpu.SemaphoreType.DMA((2,)),
                        pltpu.VMEM((past, kvw), F32), pltpu.VMEM((past, kvw), F32),
                        pltpu.VMEM((past // GROUP_TOKENS + 8, NSA_KV_HEADS * CMP_HID), F32)])
    out = pl.pallas_call(
        functools.partial(_decode_nsa_kernel, past=past, t_dec=t_dec, n_pages=n_pages, page=page),
        grid_spec=grid_spec,
        out_shape=jax.ShapeDtypeStruct((bd, DEC_ROWS, W_NQ), F32),
        compiler_params=pltpu.CompilerParams(dimension_semantics=('arbitrary',), vmem_limit_bytes=VMEM_LIMIT),
        name='decode_nsa',
    )(page_table.reshape(-1), _pad_rows(fs['nsa_q'], bd, t_dec), _pad_rows(fs['nsa_q_rot'], bd, t_dec),
      _pad_rows(fs['small'], bd, t_dec), pool, _new_cols(fs['nsa_t'], bd, t_dec), wst,
      _new_cols(fs['win_t'], bd, t_dec), *cw)
    return out[:, :t_dec].reshape(bd * t_dec, W_NQ)


def _decode_dsa(fs, cache_dsa, cache_idx, page_table, past, t_dec):
    bd, n_pages = page_table.shape
    kpool = _feature_major_pages(cache_dsa)
    ipool = _feature_major_pages(cache_idx)
    page = kpool.shape[2]
    n_keys = past + LANES
    topk = min(DSA_TOPK_MAX, (past + t_dec) // 4)
    per_b = lambda *shape: pl.BlockSpec((None,) + shape, lambda b, pt: (b,) + (0,) * len(shape))
    grid_spec = pltpu.PrefetchScalarGridSpec(
        num_scalar_prefetch=1,
        grid=(bd,),
        in_specs=[per_b(DEC_ROWS, W_IQ), per_b(DEC_ROWS, LANES), per_b(DEC_ROWS, W_DQ),
                  pl.BlockSpec(memory_space=pl.ANY), pl.BlockSpec(memory_space=pl.ANY),
                  per_b(IDX_DIM, LANES), per_b(W_DKV, LANES)],
        out_specs=per_b(DEC_ROWS, W_DQ),
        scratch_shapes=[pltpu.VMEM((2, IDX_DIM, n_keys), F32), pltpu.VMEM((2, W_DKV, n_keys), F32),
                        pltpu.SemaphoreType.DMA((2,)), pltpu.SemaphoreType.DMA((2,))])
    out = pl.pallas_call(
        functools.partial(_decode_dsa_kernel, past=past, t_dec=t_dec, n_pages=n_pages, page=page, topk=topk),
        grid_spec=grid_spec,
        out_shape=jax.ShapeDtypeStruct((bd, DEC_ROWS, W_DQ), F32),
        compiler_params=pltpu.CompilerParams(dimension_semantics=('arbitrary',), vmem_limit_bytes=VMEM_LIMIT),
        name='decode_dsa',
    )(page_table.reshape(-1), _pad_rows(fs['idx_q'], bd, t_dec), _pad_rows(fs['small'], bd, t_dec),
      _pad_rows(fs['dsa_q'], bd, t_dec), ipool, kpool, _new_cols(fs['idx_t'], bd, t_dec),
      _new_cols(fs['dsa_t'], bd, t_dec))
    return out[:, :t_dec].reshape(bd * t_dec, W_DQ)


def _rows_from_t(a_t, bx, t, lead):
    return a_t.reshape(lead + (bx, t)).transpose((len(lead), len(lead) + 1) + tuple(range(len(lead))))


def _rows_from_bt(a_bt, lead):
    bx, _, t = a_bt.shape
    n = len(lead)
    return a_bt.reshape((bx,) + lead + (t,)).transpose((0, n + 1) + tuple(range(1, n + 1)))


def kernel(x_prompt, x_sample, cache_nsa_kv, cache_dsa_kv, cache_dsa_idx_k, state_nsa_win_kv, page_table,
           c_prompt, c_sample, norm_g, w_ada, b_ada, w_in, cmp_pe, cmp_w1, cmp_w2,
           w_up_nsa, w_up_dsa, w_out, final_g):
    depth = norm_g.shape[0]
    assert depth == 1
    B, T, D = x_prompt.shape
    Bd, Td, _ = x_sample.shape
    l = 0
    past = page_table.shape[1] * cache_nsa_kv.shape[2]

    c_all = jnp.concatenate([c_prompt, c_sample], axis=0)
    pad = (-c_all.shape[0]) % 8
    c_all = jnp.pad(c_all, ((0, pad), (0, 0)))
    mod = _ada_mod(c_all, w_ada[l], b_ada[l])
    shift, scale, gate = mod[:, 0:D], mod[:, D:2 * D], mod[:, 2 * D:3 * D]

    weights = _prep_inproj_weights(w_in[l], D)

    tm = PROJ_ROWS
    fp = _in_projection_prompt(x_prompt.reshape(B * T, D), norm_g[l], scale[:B], shift[:B], T,
                               _prep_inproj_weights_prompt(w_in[l], D), tm)
    nsa_rows = _rows_from_bt(fp['nsa_t'], (4, NSA_KV_HEADS, HEAD_DIM))
    wb_p = min(WINDOW, T)
    new_win_prompt = _rows_from_bt(fp['win_t'][:, :, T - wb_p:], (2, NSA_KV_HEADS, HEAD_DIM))
    dsa_rows = _rows_from_bt(fp['dsa_t'], (2, DSA_KV_HEADS, HEAD_DIM))
    idx_rows = _rows_from_bt(fp['idx_t'], (IDX_DIM,))
    cw = _prep_compress_weights(cmp_pe[l], cmp_w1[l], cmp_w2[l])
    ck, cv = _compress_prompt(fp['cmp_k'], fp['cmp_v'], cw, B, T)
    kc = min(KEY_CHUNK, T)
    o_n = _nsa_prompt(fp, ck, cv, B, T, Q_TILE, kc, min(WINDOW + Q_TILE, T))
    o_d = _dsa_prompt(fp, B, T, Q_TILE, kc)
    y_prompt = _mixer_out(x_prompt.reshape(B * T, D), o_n, o_d, fp, gate[:B], T, final_g,
                          w_up_nsa[l], w_up_dsa[l], w_out[l], tm).reshape(B, T, D)

    Rs = Bd * Td
    tms = min(PROJ_ROWS, Rs)
    pos_s = past + jnp.arange(Td, dtype=jnp.int32)
    pos_rows = jnp.tile(pos_s, tms // Td)
    rep = lambda a: jnp.repeat(a[B:B + Bd], Td, axis=0)
    fs = _in_projection(x_sample.reshape(Rs, D), norm_g[l], rep(scale), rep(shift), 1, pos_rows, weights, tms, 1)
    for name in ('nsa_t', 'win_t', 'dsa_t', 'idx_t'):
        fs[name] = fs[name][0]
    s_nsa_rows = _rows_from_t(fs['nsa_t'], Bd, Td, (4, NSA_KV_HEADS, HEAD_DIM))
    s_win_rows = _rows_from_t(fs['win_t'], Bd, Td, (2, NSA_KV_HEADS, HEAD_DIM))
    s_dsa_rows = _rows_from_t(fs['dsa_t'], Bd, Td, (2, DSA_KV_HEADS, HEAD_DIM))
    s_idx_rows = _rows_from_t(fs['idx_t'], Bd, Td, (IDX_DIM,))

    win_state = state_nsa_win_kv[l]
    so_n = _decode_nsa(fs, cache_nsa_kv[l], win_state, page_table, cw, past, Td)
    so_d = _decode_dsa(fs, cache_dsa_kv[l], cache_dsa_idx_k[l], page_table, past, Td)
    y_sample = _mixer_out(x_sample.reshape(Rs, D), so_n, so_d, fs,
                          rep(gate), 1, final_g, w_up_nsa[l], w_up_dsa[l], w_out[l], tms).reshape(Bd, Td, D)
    new_win_sample = jnp.concatenate([win_state, s_win_rows], axis=1)[:, Td:]

    return (y_prompt, y_sample, nsa_rows[None], dsa_rows[None], idx_rows[None], new_win_prompt[None],
            s_nsa_rows[None], s_dsa_rows[None], s_idx_rows[None], new_win_sample[None])
```

```python
import functools

import numpy as np
import jax
import jax.numpy as jnp
from jax import lax
from jax.experimental import pallas as pl
from jax.experimental.pallas import tpu as pltpu

F32 = jnp.float32
BF16 = jnp.bfloat16

HEAD_DIM = 64
NSA_HEADS = 8
NSA_KV_HEADS = 2
DSA_HEADS = 8
DSA_KV_HEADS = 2
N_NSA_KV = 6
CMP_BLOCK = 32
CMP_STRIDE = 16
SEL_BLOCK = 64
N_SEL = 16
WINDOW = 512
IDX_HEADS = 4
IDX_DIM = 64
DSA_TOPK_MAX = 256
ROPE_THETA = 10000.0
EPS = 1e-6
NEG = -1e30
FORCED_SCORE = 1e4

LANES = 128
VMEM_LIMIT = 56 * 1024 * 1024

PROJ_ROWS = 256
Q_TILE = LANES
KEY_CHUNK = 1024

W_NQ = NSA_HEADS * HEAD_DIM
W_NKV = N_NSA_KV * NSA_KV_HEADS * HEAD_DIM
W_NG = NSA_HEADS * 3
W_DQ = DSA_HEADS * HEAD_DIM
W_DKV = 2 * DSA_KV_HEADS * HEAD_DIM
W_IQ = IDX_HEADS * IDX_DIM


def _cuts(d_model):
    widths = (W_NQ, W_NKV, W_NG, W_NQ, W_DQ, W_DKV, W_IQ, IDX_DIM, IDX_HEADS, W_DQ, 2 * d_model)
    c = np.concatenate([[0], np.cumsum(widths)])
    names = ('nq', 'nkv', 'ng', 'nz', 'dq', 'dkv', 'iq', 'ik', 'iw', 'dz', 'mg')
    return {n: (int(c[i]), int(c[i + 1])) for i, n in enumerate(names)}


def _split3_rows(w):
    wh = w.astype(BF16)
    wl = (w - wh.astype(F32)).astype(BF16)
    return jnp.concatenate([wh, wl, wh], axis=0)


def _split3_cols(x):
    xh = x.astype(BF16)
    xl = (x - xh.astype(F32)).astype(BF16)
    return jnp.concatenate([xh, xh, xl], axis=1)


def _silu(v):
    return v * (1.0 / (1.0 + jnp.exp(-v)))


def _sigmoid(v):
    return 1.0 / (1.0 + jnp.exp(-v))


def _ada_kernel(c_ref, w_ref, b_ref, o_ref):
    c3 = _split3_cols(_silu(c_ref[...]))
    o_ref[...] = jnp.dot(c3, w_ref[...], preferred_element_type=F32) + b_ref[...]


def _ada_mod(c_all, w_ada, b_ada):
    m, d = c_all.shape
    n = w_ada.shape[1]
    tn = 512
    w3 = _split3_rows(w_ada)
    return pl.pallas_call(
        _ada_kernel,
        grid=(n // tn,),
        in_specs=[pl.BlockSpec((m, d), lambda j: (0, 0)),
                  pl.BlockSpec((3 * d, tn), lambda j: (0, j)),
                  pl.BlockSpec((1, tn), lambda j: (0, j))],
        out_specs=pl.BlockSpec((m, tn), lambda j: (0, j)),
        out_shape=jax.ShapeDtypeStruct((m, n), F32),
        name='ada_mod',
    )(c_all, w3, b_ada.reshape(1, n))


def _rope_tf(v, cos, sin_signed):
    first_half = (lax.broadcasted_iota(jnp.int32, (1, LANES), 1) % HEAD_DIM) < (HEAD_DIM // 2)
    outs = []
    for c in range(v.shape[1] // LANES):
        vc = v[:, c * LANES:(c + 1) * LANES]
        sw = jnp.where(first_half, pltpu.roll(vc, LANES - HEAD_DIM // 2, 1), pltpu.roll(vc, HEAD_DIM // 2, 1))
        outs.append(vc * cos + sw * sin_signed)
    return jnp.concatenate(outs, axis=1) if len(outs) > 1 else outs[0]


def _rope_ft(v, cos_t, sin_t):
    half = HEAD_DIM // 2
    outs = []
    for hd in range(v.shape[0] // HEAD_DIM):
        x1 = v[hd * HEAD_DIM:hd * HEAD_DIM + half]
        x2 = v[hd * HEAD_DIM + half:(hd + 1) * HEAD_DIM]
        outs.append(x1 * cos_t - x2 * sin_t)
        outs.append(x2 * cos_t + x1 * sin_t)
    return jnp.concatenate(outs, axis=0)


def _inproj_kernel(x_ref, g_ref, sc_ref, sh_ref, cos_ref, sin_ref, cost_ref, sint_ref,
                   wp_ref, wa_ref, wt_ref, wti_ref,
                   q_ref, qrot_ref, iq_ref, small_ref, ck_ref,
                   cv_ref, snz_ref, sdz_ref, dq_ref, mg_ref,
                   nsat_ref, wint_ref, dsat_ref, idxt_ref, idx3_ref):
    x = x_ref[...]
    y = x * lax.rsqrt(jnp.mean(x * x, axis=-1, keepdims=True) + EPS) * g_ref[...]
    h = y * (1.0 + sc_ref[...]) + sh_ref[...]
    hh = h.astype(BF16)
    hl = (h - hh.astype(F32)).astype(BF16)
    h3 = jnp.concatenate([hh, hh, hl], axis=1)
    cos = cos_ref[...]
    sin = sin_ref[...]
    cos_t = cost_ref[...]
    sin_t = sint_ref[...]

    p = jnp.dot(h3, wp_ref[...], preferred_element_type=F32)
    q = p[:, 0:W_NQ]
    q_ref[...] = q
    qrot_ref[...] = _rope_tf(q, cos, sin)
    iq_ref[...] = _rope_tf(p[:, W_NQ:W_NQ + W_IQ], cos, sin)
    small_ref[...] = p[:, W_NQ + W_IQ:W_NQ + W_IQ + LANES]
    ck_ref[...] = p[:, W_NQ + W_IQ + LANES:W_NQ + W_IQ + 2 * LANES]

    a = jnp.dot(hh, wa_ref[...], preferred_element_type=F32)
    cv_ref[...] = a[:, 0:LANES]
    o = LANES
    snz_ref[...] = _silu(a[:, o:o + W_NQ])
    o += W_NQ
    sdz_ref[...] = _silu(a[:, o:o + W_DQ])
    o += W_DQ
    dq_ref[...] = _rope_tf(a[:, o:o + W_DQ], cos, sin)
    o += W_DQ
    mg_ref[...] = _sigmoid(a[:, o:])

    nt = (((1,), (1,)), ((), ()))
    t = lax.dot_general(wt_ref[...], hh, nt, preferred_element_type=F32)
    kv = 2 * HEAD_DIM
    nsat_ref[0:2 * kv, :] = t[0:2 * kv]
    nsat_ref[2 * kv:3 * kv, :] = _rope_ft(t[2 * kv:3 * kv], cos_t, sin_t)
    nsat_ref[3 * kv:4 * kv, :] = t[3 * kv:4 * kv]
    wint_ref[0:kv, :] = _rope_ft(t[4 * kv:5 * kv], cos_t, sin_t)
    wint_ref[kv:2 * kv, :] = t[5 * kv:6 * kv]
    dsat_ref[0:kv, :] = _rope_ft(t[6 * kv:7 * kv], cos_t, sin_t)
    dsat_ref[kv:2 * kv, :] = t[7 * kv:8 * kv]
    ti = lax.dot_general(wti_ref[...], h3, nt, preferred_element_type=F32)
    ti = _rope_ft(ti, cos_t, sin_t)
    idxt_ref[...] = ti
    ti_hi = ti.astype(BF16)
    ti_lo = (ti - ti_hi.astype(F32)).astype(BF16)
    idx3_ref[...] = jnp.concatenate([ti_hi, ti_lo, ti_hi], axis=0)


def _prep_inproj_weights(w_in, d_model):
    c = _cuts(d_model)
    sl = lambda name: w_in[:, c[name][0]:c[name][1]]
    nkv = sl('nkv')
    kv = NSA_KV_HEADS * HEAD_DIM
    small = jnp.concatenate([sl('iw'), sl('ng'), jnp.zeros((d_model, LANES - IDX_HEADS - W_NG), F32)], axis=1)
    wp = jnp.concatenate([sl('nq'), sl('iq'), small, nkv[:, 0:kv]], axis=1)
    wa = jnp.concatenate([nkv[:, kv:2 * kv], sl('nz'), sl('dz'), sl('dq'), sl('mg')], axis=1)
    wt = jnp.concatenate([nkv, sl('dkv')], axis=1).T
    wti = sl('ik')
    return _split3_rows(wp), wa.astype(BF16), wt.astype(BF16), _split3_rows(wti).T


def _rope_tables(pos):
    half = HEAD_DIM // 2
    inv = ROPE_THETA ** (-jnp.arange(half, dtype=F32) / half)
    ang = pos.astype(F32)[:, None] * inv[None, :]
    cos = jnp.cos(ang)
    sin = jnp.sin(ang)
    cos_tf = jnp.tile(cos, (1, LANES // half))
    sin_tf = jnp.tile(jnp.concatenate([-sin, sin], axis=1), (1, LANES // HEAD_DIM))
    return cos_tf, sin_tf, cos.T, sin.T


def _in_projection(x2d, norm_g, scale, shift, rows_per_mod, pos, weights, tm, nb):
    r, d = x2d.shape
    tpb = (r // nb) // tm
    wp, wa, wt, wti = weights
    n_tiles = r // tm
    cos_tf, sin_tf, cos_t, sin_t = _rope_tables(pos)
    p_tiles = pos.shape[0] // tm
    if rows_per_mod == 1:
        mod_spec = pl.BlockSpec((tm, d), lambda i: (i, 0))
        sc, sh = scale, shift
    else:
        per = rows_per_mod // tm
        mod_spec = pl.BlockSpec((None, 1, d), lambda i: (i // per, 0, 0))
        sc, sh = scale[:, None, :], shift[:, None, :]
    const = lambda shape: pl.BlockSpec(shape, lambda i: (0,) * len(shape), pipeline_mode=pl.Buffered(1))
    tf = lambda w: pl.BlockSpec((tm, w), lambda i: (i, 0))
    ft = lambda w: pl.BlockSpec((None, w, tm), lambda i: (i // tpb, 0, i % tpb))
    tf_widths = (W_NQ, W_NQ, W_IQ, LANES, LANES, LANES, W_NQ, W_DQ, W_DQ, 2 * d)
    ft_widths = (4 * LANES, 2 * LANES, 2 * LANES, IDX_DIM)
    outs = pl.pallas_call(
        _inproj_kernel,
        grid=(n_tiles,),
        in_specs=[pl.BlockSpec((tm, d), lambda i: (i, 0)),
                  const((1, d)), mod_spec, mod_spec,
                  pl.BlockSpec((tm, LANES), lambda i: (i % p_tiles, 0)),
                  pl.BlockSpec((tm, LANES), lambda i: (i % p_tiles, 0)),
                  pl.BlockSpec((HEAD_DIM // 2, tm), lambda i: (0, i % p_tiles)),
                  pl.BlockSpec((HEAD_DIM // 2, tm), lambda i: (0, i % p_tiles)),
                  const(wp.shape), const(wa.shape), const(wt.shape), const(wti.shape)],
        out_specs=[tf(w) for w in tf_widths] + [ft(w) for w in ft_widths] + [ft(3 * IDX_DIM)],
        out_shape=[jax.ShapeDtypeStruct((r, w), F32) for w in tf_widths]
        + [jax.ShapeDtypeStruct((nb, w, r // nb), F32) for w in ft_widths]
        + [jax.ShapeDtypeStruct((nb, 3 * IDX_DIM, r // nb), BF16)],
        compiler_params=pltpu.CompilerParams(dimension_semantics=('arbitrary',), vmem_limit_bytes=VMEM_LIMIT),
        name='in_projection',
    )(x2d, norm_g.reshape(1, d), sc, sh, cos_tf, sin_tf, cos_t, sin_t, wp, wa, wt, wti)
    names = ('nsa_q', 'nsa_q_rot', 'idx_q', 'small', 'cmp_k', 'cmp_v', 'silu_nz', 'silu_dz', 'dsa_q', 'merge',
             'nsa_t', 'win_t', 'dsa_t', 'idx_t', 'idx3_t')
    return dict(zip(names, outs))


SMALL_ROWS = 32


def _inproj_prompt_kernel(x_ref, g_ref, sc_ref, sh_ref, cos_ref, sin_ref, cost_ref, sint_ref,
                          wp_ref, wa_ref, wtp_ref, wt_ref,
                          ck_ref, idx3_ref,
                          cv_ref, slck_ref, wink_ref, dsak_ref, snz_ref, sdz_ref, mg_ref,
                          qt_ref, iqt_ref, smallt_ref, qrt_ref, dqt_ref, nsat_ref, wint_ref, dsat_ref, idxt_ref):
    x = x_ref[...]
    y = x * lax.rsqrt(jnp.mean(x * x, axis=-1, keepdims=True) + EPS) * g_ref[...]
    h = y * (1.0 + sc_ref[...]) + sh_ref[...]
    hh = h.astype(BF16)
    hl = (h - hh.astype(F32)).astype(BF16)
    h3 = jnp.concatenate([hh, hh, hl], axis=1)
    cos = cos_ref[...]
    sin = sin_ref[...]
    cos_t = cost_ref[...]
    sin_t = sint_ref[...]
    nt = (((1,), (1,)), ((), ()))
    kv = NSA_KV_HEADS * HEAD_DIM

    p = jnp.dot(h3, wp_ref[...], preferred_element_type=F32)
    ck_ref[...] = p[:, 0:LANES]
    ik = _rope_tf(p[:, LANES:2 * LANES], cos, sin)
    ik_hi = ik.astype(BF16)
    ik_lo = (ik - ik_hi.astype(F32)).astype(BF16)
    first = lax.broadcasted_iota(jnp.int32, (1, LANES), 1) < IDX_DIM
    idx3_ref[...] = jnp.concatenate([jnp.where(first, ik_hi, ik_lo),
                                     jnp.where(first, ik_hi, jnp.zeros_like(ik_hi))], axis=1)

    a = jnp.dot(hh, wa_ref[...], preferred_element_type=F32)
    cv_ref[...] = a[:, 0:kv]
    slck_ref[...] = _rope_tf(a[:, kv:2 * kv], cos, sin).astype(BF16)
    wink_ref[...] = _rope_tf(a[:, 2 * kv:3 * kv], cos, sin).astype(BF16)
    dsak_ref[...] = _rope_tf(a[:, 3 * kv:4 * kv], cos, sin).astype(BF16)
    o = 4 * kv
    snz_ref[...] = _silu(a[:, o:o + W_NQ])
    o += W_NQ
    sdz_ref[...] = _silu(a[:, o:o + W_DQ])
    o += W_DQ
    mg_ref[...] = _sigmoid(a[:, o:])

    tp = lax.dot_general(wtp_ref[...], h3, nt, preferred_element_type=F32)
    qt_ref[...] = tp[0:W_NQ]
    iqt_ref[...] = _rope_ft(tp[W_NQ:W_NQ + W_IQ], cos_t, sin_t)
    smallt_ref[...] = tp[W_NQ + W_IQ:W_NQ + W_IQ + SMALL_ROWS]

    t = lax.dot_general(wt_ref[...], hh, nt, preferred_element_type=F32)
    qrt_ref[...] = _rope_ft(t[0:W_NQ], cos_t, sin_t)
    o = W_NQ
    dqt_ref[...] = _rope_ft(t[o:o + W_DQ], cos_t, sin_t)
    o += W_DQ
    nsat_ref[0:2 * kv, :] = t[o:o + 2 * kv]
    nsat_ref[2 * kv:3 * kv, :] = _rope_ft(t[o + 2 * kv:o + 3 * kv], cos_t, sin_t)
    nsat_ref[3 * kv:4 * kv, :] = t[o + 3 * kv:o + 4 * kv]
    wint_ref[0:kv, :] = _rope_ft(t[o + 4 * kv:o + 5 * kv], cos_t, sin_t)
    wint_ref[kv:2 * kv, :] = t[o + 5 * kv:o + 6 * kv]
    dsat_ref[0:kv, :] = _rope_ft(t[o + 6 * kv:o + 7 * kv], cos_t, sin_t)
    dsat_ref[kv:2 * kv, :] = t[o + 7 * kv:o + 8 * kv]
    idxt_ref[...] = _rope_ft(t[o + 8 * kv:o + 8 * kv + IDX_DIM], cos_t, sin_t)


def _prep_inproj_weights_prompt(w_in, d_model):
    c = _cuts(d_model)
    sl = lambda name: w_in[:, c[name][0]:c[name][1]]
    nkv = sl('nkv')
    kv = NSA_KV_HEADS * HEAD_DIM
    zeros = lambda n: jnp.zeros((d_model, n), F32)
    small = jnp.concatenate([sl('iw'), sl('ng')], axis=1)
    wp = jnp.concatenate([nkv[:, 0:kv], sl('ik'), sl('ik')], axis=1)
    wa = jnp.concatenate([nkv[:, kv:2 * kv], nkv[:, 2 * kv:3 * kv], nkv[:, 4 * kv:5 * kv], sl('dkv')[:, 0:kv],
                          sl('nz'), sl('dz'), sl('mg')], axis=1)
    wtp = jnp.concatenate([sl('nq'), sl('iq'), small, zeros(SMALL_ROWS - small.shape[1])], axis=1)
    wt = jnp.concatenate([sl('nq'), sl('dq'), nkv, sl('dkv'), sl('ik')], axis=1)
    return _split3_rows(wp), wa.astype(BF16), _split3_rows(wtp).T, wt.T.astype(BF16)


def _in_projection_prompt(x2d, norm_g, scale, shift, t_len, weights, tm):
    r, d = x2d.shape
    nb = r // t_len
    tpb = t_len // tm
    wp, wa, wtp, wt = weights
    cos_tf, sin_tf, cos_t, sin_t = _rope_tables(jnp.arange(t_len, dtype=jnp.int32))
    mod_spec = pl.BlockSpec((None, 1, d), lambda i: (i // tpb, 0, 0))
    const = lambda shape: pl.BlockSpec(shape, lambda i: (0,) * len(shape), pipeline_mode=pl.Buffered(1))
    tf = lambda w: pl.BlockSpec((tm, w), lambda i: (i, 0))
    ft = lambda w: pl.BlockSpec((None, w, tm), lambda i: (i // tpb, 0, i % tpb))
    kv = NSA_KV_HEADS * HEAD_DIM
    tf_outs = (('cmp_k', kv, F32), ('idx3', 2 * LANES, BF16),
               ('cmp_v', kv, F32), ('slc_k', kv, BF16), ('win_k', kv, BF16), ('dsa_k', kv, BF16),
               ('silu_nz', W_NQ, F32), ('silu_dz', W_DQ, F32), ('merge', 2 * d, F32))
    ft_outs = (('nsa_q_t', W_NQ), ('idx_q_t', W_IQ), ('small_t', SMALL_ROWS), ('nsa_q_rot_t', W_NQ), ('dsa_q_t', W_DQ),
               ('nsa_t', 4 * kv), ('win_t', 2 * kv), ('dsa_t', W_DKV), ('idx_t', IDX_DIM))
    outs = pl.pallas_call(
        _inproj_prompt_kernel,
        grid=(r // tm,),
        in_specs=[pl.BlockSpec((tm, d), lambda i: (i, 0)),
                  const((1, d)), mod_spec, mod_spec,
                  pl.BlockSpec((tm, LANES), lambda i: (i % tpb, 0)),
                  pl.BlockSpec((tm, LANES), lambda i: (i % tpb, 0)),
                  pl.BlockSpec((HEAD_DIM // 2, tm), lambda i: (0, i % tpb)),
                  pl.BlockSpec((HEAD_DIM // 2, tm), lambda i: (0, i % tpb)),
                  const(wp.shape), const(wa.shape), const(wtp.shape), const(wt.shape)],
        out_specs=[tf(w) for _, w, _ in tf_outs] + [ft(w) for _, w in ft_outs],
        out_shape=[jax.ShapeDtypeStruct((r, w), dt) for _, w, dt in tf_outs]
        + [jax.ShapeDtypeStruct((nb, w, t_len), F32) for _, w in ft_outs],
        compiler_params=pltpu.CompilerParams(dimension_semantics=('arbitrary',), vmem_limit_bytes=VMEM_LIMIT),
        name='in_projection_prompt',
    )(x2d, norm_g.reshape(1, d), scale[:, None, :], shift[:, None, :], cos_tf, sin_tf, cos_t, sin_t,
      wp, wa, wtp, wt)
    return dict(zip([n for n, _, _ in tf_outs] + [n for n, _ in ft_outs], outs))


def _mixer_out_kernel(x_ref, on_ref, od_ref, snz_ref, sdz_ref, mg_ref, ga_ref, fg_ref,
                      wun_ref, wud_ref, wo_ref, y_ref):
    d = x_ref.shape[1]
    u_n = jnp.dot((on_ref[...] * snz_ref[...]).astype(BF16), wun_ref[...], preferred_element_type=F32)
    u_d = jnp.dot((od_ref[...] * sdz_ref[...]).astype(BF16), wud_ref[...], preferred_element_type=F32)
    mg = mg_ref[...]
    mix = mg[:, 0:d] * u_n + mg[:, d:2 * d] * u_d
    z = x_ref[...] + ga_ref[...] * jnp.dot(mix.astype(BF16), wo_ref[...], preferred_element_type=F32)
    y_ref[...] = z * lax.rsqrt(jnp.mean(z * z, axis=-1, keepdims=True) + EPS) * fg_ref[...]


def _mixer_out(x2d, o_n, o_d, f, gate, rows_per_mod, final_g, w_up_nsa, w_up_dsa, w_out, tm):
    r, d = x2d.shape
    if rows_per_mod == 1:
        mod_spec = pl.BlockSpec((tm, d), lambda i: (i, 0))
        ga = gate
    else:
        per = rows_per_mod // tm
        mod_spec = pl.BlockSpec((None, 1, d), lambda i: (i // per, 0, 0))
        ga = gate[:, None, :]
    const = lambda shape: pl.BlockSpec(shape, lambda i: (0,) * len(shape), pipeline_mode=pl.Buffered(1))
    tf = lambda w: pl.BlockSpec((tm, w), lambda i: (i, 0))
    return pl.pallas_call(
        _mixer_out_kernel,
        grid=(r // tm,),
        in_specs=[tf(d), tf(W_NQ), tf(W_DQ), tf(W_NQ), tf(W_DQ), tf(2 * d), mod_spec, const((1, d)),
                  const(w_up_nsa.shape), const(w_up_dsa.shape), const(w_out.shape)],
        out_specs=tf(d),
        out_shape=jax.ShapeDtypeStruct((r, d), F32),
        compiler_params=pltpu.CompilerParams(dimension_semantics=('arbitrary',), vmem_limit_bytes=VMEM_LIMIT),
        name='mixer_out',
    )(x2d, o_n, o_d, f['silu_nz'], f['silu_dz'], f['merge'], ga, final_g.reshape(1, d),
      w_up_nsa.astype(BF16), w_up_dsa.astype(BF16), w_out.astype(BF16))


POS_INF = float('inf')


_MAG_BITS = 0x7FFFFFFF


def _ordered_key(f):
    b = lax.bitcast_convert_type(f, jnp.int32)
    return jnp.where(b < 0, b ^ _MAG_BITS, b)


def _from_ordered_key(key):
    return lax.bitcast_convert_type(jnp.where(key < 0, key ^ _MAG_BITS, key), F32)


VALUE_MID_ROUNDS = 12


def _bisect_topk(count_ge, count_zero, v_min, v_max, n_valid, k):
    kf = float(k)

    def probe(lo, hi, cnt_lo, cnt_hi, x):
        c = count_ge(_from_ordered_key(x))
        up = c >= kf
        return (jnp.where(up, x, lo), jnp.where(up, hi, x), jnp.where(up, c, cnt_lo), jnp.where(up, cnt_hi, c))

    def key_mid(lo, hi):
        return (lo >> 1) + (hi >> 1) + (lo & hi & 1)

    def next_probe(lo, hi, it):
        guess = _ordered_key(0.5 * _from_ordered_key(lo) + 0.5 * _from_ordered_key(hi))
        use = jnp.logical_and(it < VALUE_MID_ROUNDS, jnp.logical_and(guess > lo, guess < hi))
        return jnp.where(use, guess, key_mid(lo, hi))

    def cond(st):
        _, _, _, _, done, it = st
        return jnp.logical_and(it < 48, jnp.min(done) < 0.5)

    def body(st):
        lo, hi, cnt_lo, cnt_hi, _, it = st
        lo, hi, cnt_lo, cnt_hi = probe(lo, hi, cnt_lo, cnt_hi, next_probe(lo, hi, it))
        lo, hi, cnt_lo, cnt_hi = probe(lo, hi, cnt_lo, cnt_hi, next_probe(lo, hi, it))
        done = jnp.where(jnp.logical_or(cnt_lo <= kf, key_mid(lo, hi) == lo), 1.0, 0.0)
        return lo, hi, cnt_lo, cnt_hi, done, it + 1

    hi0 = _ordered_key(v_max)
    lo, hi, cnt_lo, cnt_hi = probe(_ordered_key(v_min), hi0, n_valid, jnp.zeros_like(n_valid), hi0)
    is_open = key_mid(lo, hi) != lo
    c_ge0, c_gt0 = count_zero()
    zero_tie = jnp.logical_and(is_open, jnp.logical_and(c_gt0 < kf, c_ge0 >= kf))
    raise_lo = jnp.logical_and(is_open, jnp.logical_and(c_gt0 >= kf, lo < 0))
    lower_hi = jnp.logical_and(is_open, jnp.logical_and(c_ge0 < kf, hi > 0))
    to_zero_lo = jnp.logical_or(zero_tie, raise_lo)
    lo = jnp.where(to_zero_lo, 0, lo)
    cnt_lo = jnp.where(to_zero_lo, c_ge0, cnt_lo)
    hi = jnp.where(jnp.logical_or(zero_tie, lower_hi), 0, hi)
    cnt_hi = jnp.where(lower_hi, c_ge0, cnt_hi)
    done0 = jnp.where(jnp.logical_or(n_valid <= kf, jnp.logical_or(cnt_lo <= kf, key_mid(lo, hi) == lo)),
                      1.0, 0.0)
    lo, _, cnt_lo, _, _, _ = lax.while_loop(cond, body, (lo, hi, cnt_lo, cnt_hi, done0, jnp.int32(0)))
    return jnp.where(n_valid <= kf, NEG, _from_ordered_key(lo)), cnt_lo


def _topk_cap_rows(s, row_min, row_max, n_valid, k):
    rows, n = s.shape
    kf = float(k)
    count_ge = lambda x: jnp.sum(jnp.where(s >= x, 1.0, 0.0), axis=1, keepdims=True)
    count_zero = lambda: (jnp.sum(jnp.where(s >= 0.0, 1.0, 0.0), axis=1, keepdims=True),
                          jnp.sum(jnp.where(s > 0.0, 1.0, 0.0), axis=1, keepdims=True))
    t, _ = _bisect_topk(count_ge, count_zero, row_min, row_max, n_valid, k)
    eq = jnp.where(s == t, 1.0, 0.0)
    need = kf - jnp.sum(jnp.where(s > t, 1.0, 0.0), axis=1, keepdims=True)
    n_grp = n // LANES
    r_i = lax.broadcasted_iota(jnp.int32, (LANES, LANES), 0)
    c_i = lax.broadcasted_iota(jnp.int32, (LANES, LANES), 1)
    upper = jnp.where(r_i < c_i, 1.0, 0.0).astype(BF16)
    stacked = jnp.concatenate([eq[:, j * LANES:(j + 1) * LANES] for j in range(n_grp)], axis=0)
    within = jnp.dot(stacked.astype(BF16), upper, preferred_element_type=F32)
    run = jnp.zeros((rows, 1), F32)
    caps = []
    for j in range(n_grp):
        sl = slice(j * LANES, (j + 1) * LANES)
        before = run + within[j * rows:(j + 1) * rows]
        sj = s[:, sl]
        keep = jnp.logical_or(sj > t, jnp.logical_and(sj == t, before < need))
        caps.append(jnp.where(keep, POS_INF, NEG))
        run = run + jnp.sum(eq[:, sl], axis=1, keepdims=True)
    return jnp.concatenate(caps, axis=1)


ACC_ROWS = 64
TIE_BLOCK = 256


def _col_reduce(x, op):
    n, w = x.shape
    if n > ACC_ROWS and n % ACC_ROWS == 0:
        x = op(x.reshape(n // ACC_ROWS, ACC_ROWS, w), axis=0)
    return op(x, axis=0, keepdims=True)


def _count_ge_t(st_scr, x, nk, kc):
    tq = st_scr.shape[1]

    def body(c, acc):
        k0 = pl.multiple_of(c * kc, kc)
        hit = jnp.where(st_scr[pl.ds(k0, kc), :] >= x, 1.0, 0.0)
        return acc + jnp.sum(hit.reshape(kc // ACC_ROWS, ACC_ROWS, tq), axis=0)

    acc = lax.fori_loop(0, nk, body, jnp.zeros((ACC_ROWS, tq), F32))
    return jnp.sum(acc, axis=0, keepdims=True)


def _topk_threshold_t(st_scr, col_min, col_max, n_valid, k, nk, kc):
    tq = st_scr.shape[1]

    def count_zero():
        def body(c, carry):
            ge, gt = carry
            s = st_scr[pl.ds(pl.multiple_of(c * kc, kc), kc), :]
            ge = ge + jnp.sum(jnp.where(s >= 0.0, 1.0, 0.0).reshape(kc // ACC_ROWS, ACC_ROWS, tq), axis=0)
            gt = gt + jnp.sum(jnp.where(s > 0.0, 1.0, 0.0).reshape(kc // ACC_ROWS, ACC_ROWS, tq), axis=0)
            return ge, gt
        z = jnp.zeros((ACC_ROWS, tq), F32)
        ge, gt = lax.fori_loop(0, nk, body, (z, z))
        return jnp.sum(ge, axis=0, keepdims=True), jnp.sum(gt, axis=0, keepdims=True)

    return _bisect_topk(lambda x: _count_ge_t(st_scr, x, nk, kc), count_zero, col_min, col_max, n_valid, k)


def _write_topk_cap_t(st_scr, t, cnt_ge, k, nk, kc):
    kf = float(k)
    tq = st_scr.shape[1]
    blk = min(TIE_BLOCK, kc)
    has_tie = jnp.max(cnt_ge) > kf

    @pl.when(jnp.logical_not(has_tie))
    def _():
        def body(c, _):
            sl = pl.ds(pl.multiple_of(c * kc, kc), kc)
            st_scr[sl, :] = jnp.where(st_scr[sl, :] >= t, POS_INF, NEG)
            return 0
        lax.fori_loop(0, nk, body, 0)

    @pl.when(has_tie)
    def _():
        def cnt_body(c, acc):
            sl = pl.ds(pl.multiple_of(c * kc, kc), kc)
            hit = jnp.where(st_scr[sl, :] > t, 1.0, 0.0)
            return acc + jnp.sum(hit.reshape(kc // ACC_ROWS, ACC_ROWS, tq), axis=0)
        cnt_gt = jnp.sum(lax.fori_loop(0, nk, cnt_body, jnp.zeros((ACC_ROWS, tq), F32)), axis=0, keepdims=True)
        need = kf - cnt_gt
        r_i = lax.broadcasted_iota(jnp.int32, (blk, blk), 0)
        c_i = lax.broadcasted_iota(jnp.int32, (blk, blk), 1)
        lower = jnp.where(c_i < r_i, 1.0, 0.0).astype(BF16)

        def body(c, run):
            k0 = pl.multiple_of(c * kc, kc)
            for b in range(kc // blk):
                sl = pl.ds(k0 + b * blk, blk)
                s = st_scr[sl, :]
                eq = jnp.where(s == t, 1.0, 0.0)
                before = run + jnp.dot(lower, eq.astype(BF16), preferred_element_type=F32)
                keep = jnp.logical_or(s > t, jnp.logical_and(s == t, before < need))
                st_scr[sl, :] = jnp.where(keep, POS_INF, NEG)
                run = run + _col_reduce(eq, jnp.sum)
            return run
        lax.fori_loop(0, nk, body, jnp.zeros_like(t))


LOG2_E = float(np.log2(np.e))


def _q_pad(qt, g, hg, scale):
    qg = jnp.concatenate([qt[(g * hg + j) * HEAD_DIM:(g * hg + j + 1) * HEAD_DIM, :] for j in range(hg)],
                         axis=1) * (scale * LOG2_E)
    z = jnp.zeros_like(qg)
    parts = [z] * NSA_KV_HEADS
    parts[g] = qg
    return jnp.concatenate(parts, axis=0).astype(BF16)


def _flash_t(q_pads, k_ref, vt_ref, vrow0s, cap_fns, k_base, n_chunks, kc, n_rep):
    n = q_pads[0].shape[1]
    n_g = len(q_pads)
    ones = jnp.ones((8, kc), BF16)

    def body(c, carry):
        k0 = pl.multiple_of(k_base + c * kc, LANES)
        k_chunk = k_ref[pl.ds(k0, kc), :]
        out = []
        for g in range(n_g):
            m, acc = carry[g]
            s = jnp.dot(k_chunk, q_pads[g], preferred_element_type=F32)
            s = jnp.minimum(s, jnp.concatenate([cap_fns[g](k0)] * n_rep, axis=1))
            m_new = jnp.maximum(m, _col_reduce(s, jnp.max))
            p = jnp.exp2((s - m_new).astype(BF16))
            vt = jnp.concatenate([vt_ref[pl.ds(vrow0s[g], HEAD_DIM), pl.ds(k0, kc)].astype(BF16), ones], axis=0)
            acc = jnp.exp2(m - m_new) * acc + jnp.dot(vt, p, preferred_element_type=F32)
            out.append((m_new, acc))
        return tuple(out)

    init = tuple((jnp.full((1, n), NEG, F32), jnp.zeros((HEAD_DIM + 8, n), F32)) for _ in range(n_g))
    fin = lax.fori_loop(0, n_chunks, body, init)
    return [acc[0:HEAD_DIM] / jnp.maximum(acc[HEAD_DIM:HEAD_DIM + 1], 1e-30) for _, acc in fin]


def _store_heads(o_ref, o_t, g, hg, tq):
    for j in range(hg):
        h = g * hg + j
        o_ref[:, h * HEAD_DIM:(h + 1) * HEAD_DIM] = o_t[:, j * tq:(j + 1) * tq].T


def _dsa_prompt_kernel(iqt_ref, smallt_ref, dqt_ref, idx3_ref, k_ref, kvt_ref, o_ref, st_scr,
                       *, tq, kc, topk):
    qs = pl.program_id(1) * tq
    nk = (qs + tq + kc - 1) // kc
    qpos = qs + lax.broadcasted_iota(jnp.int32, (1, tq), 1)

    iqt = iqt_ref[...]
    q3 = []
    for h in range(IDX_HEADS):
        qh = iqt[h * IDX_DIM:(h + 1) * IDX_DIM, :]
        q_hi = qh.astype(BF16)
        q_lo = (qh - q_hi.astype(F32)).astype(BF16)
        q3.append(jnp.concatenate([q_hi, q_hi, q_lo, jnp.zeros_like(q_hi)], axis=0))
    wt = smallt_ref[...]
    out_scale = (IDX_DIM ** -0.5) * (IDX_HEADS ** -0.5)

    def score_body(c, carry):
        mn, mx = carry
        k0 = pl.multiple_of(c * kc, kc)
        k3 = idx3_ref[pl.ds(k0, kc), :]
        sc = jnp.zeros((kc, tq), F32)
        for h in range(IDX_HEADS):
            d = jnp.dot(k3, q3[h], preferred_element_type=F32)
            sc = sc + jnp.maximum(d, 0.0) * wt[h:h + 1, :]
        sc = sc * out_scale
        ok = (k0 + lax.broadcasted_iota(jnp.int32, (kc, 1), 0)) <= qpos
        st_scr[pl.ds(k0, kc), :] = jnp.where(ok, sc, -POS_INF)
        mn = jnp.minimum(mn, _col_reduce(jnp.where(ok, sc, POS_INF), jnp.min))
        mx = jnp.maximum(mx, _col_reduce(jnp.where(ok, sc, NEG), jnp.max))
        return mn, mx

    mn, mx = lax.fori_loop(0, nk, score_body,
                           (jnp.full((1, tq), POS_INF, F32), jnp.full((1, tq), NEG, F32)))
    t, cnt = _topk_threshold_t(st_scr, mn, mx, (qpos + 1).astype(F32), topk, nk, kc)
    _write_topk_cap_t(st_scr, t, cnt, topk, nk, kc)

    hg = DSA_HEADS // DSA_KV_HEADS
    dqt = dqt_ref[...]
    groups = range(DSA_KV_HEADS)
    cap_fn = lambda k0: st_scr[pl.ds(k0, kc), :]
    o_ts = _flash_t([_q_pad(dqt, g, hg, HEAD_DIM ** -0.5) for g in groups], k_ref, kvt_ref,
                    [(DSA_KV_HEADS + g) * HEAD_DIM for g in groups], [cap_fn] * DSA_KV_HEADS, 0, nk, kc, hg)
    for g in groups:
        _store_heads(o_ref, o_ts[g], g, hg, tq)


def _dsa_prompt(f, bx, t_len, tq, kc):
    topk = min(DSA_TOPK_MAX, t_len // 4)
    nq = t_len // tq
    hg = DSA_HEADS // DSA_KV_HEADS
    row = lambda w: pl.BlockSpec((tq, w), lambda b, i: (b * nq + i, 0))
    qt = lambda rows: pl.BlockSpec((None, rows, tq), lambda b, i: (b, 0, i))
    keys = lambda w: pl.BlockSpec((t_len, w), lambda b, i: (b, 0))
    per_b = lambda rows: pl.BlockSpec((None, rows, t_len), lambda b, i: (b, 0, 0))
    return pl.pallas_call(
        functools.partial(_dsa_prompt_kernel, tq=tq, kc=kc, topk=topk),
        grid=(bx, nq),
        in_specs=[qt(W_IQ), qt(SMALL_ROWS), qt(W_DQ), keys(2 * LANES), keys(DSA_KV_HEADS * HEAD_DIM),
                  per_b(W_DKV)],
        out_specs=row(W_DQ),
        out_shape=jax.ShapeDtypeStruct((bx * t_len, W_DQ), F32),
        scratch_shapes=[pltpu.VMEM((t_len, tq), F32)],
        compiler_params=pltpu.CompilerParams(dimension_semantics=('arbitrary', 'arbitrary'),
                                             vmem_limit_bytes=VMEM_LIMIT),
        name='dsa_prompt',
    )(f['idx_q_t'], f['small_t'], f['dsa_q_t'], f['idx3'], f['dsa_k'], f['dsa_t'])


GROUP_TOKENS = CMP_STRIDE
N_HALVES = CMP_BLOCK // CMP_STRIDE
CMP_HID = 2 * HEAD_DIM


def _prep_compress_weights(cmp_pe, cmp_w1, cmp_w2):
    g_n = NSA_KV_HEADS
    eye = jnp.eye(g_n, dtype=F32)
    out = []
    for s in range(2):
        w1 = cmp_w1[s].reshape(N_HALVES, GROUP_TOKENS, HEAD_DIM, CMP_HID)
        w1e = jnp.einsum('ardj,gh->arghdj', w1, eye)
        w1e = w1e.transpose(0, 1, 2, 4, 3, 5).reshape(N_HALVES, GROUP_TOKENS * g_n * HEAD_DIM, g_n * CMP_HID)
        pe = cmp_pe[s].reshape(N_HALVES, GROUP_TOKENS, 1, HEAD_DIM)
        pe = jnp.broadcast_to(pe, (N_HALVES, GROUP_TOKENS, g_n, HEAD_DIM)).reshape(N_HALVES, 1, -1)
        w2e = jnp.einsum('jd,gh->gjhd', cmp_w2[s], eye).reshape(g_n * CMP_HID, g_n * HEAD_DIM)
        out.append((w1e, pe, w2e))
    (w1k, pek, w2k), (w1v, pev, w2v) = out
    w1k3 = jnp.stack([_split3_rows(w1k[a]) for a in range(N_HALVES)])
    return (w1k3, pek, _split3_rows(w2k), w1v.astype(BF16), pev, w2v.astype(BF16))


def _compress_slot(x_ref, m, pe_ref, w1_ref, w2_ref, shift_scr, precise):
    xr = jnp.concatenate([x_ref[pl.ds(r, m, stride=GROUP_TOKENS), :] for r in range(GROUP_TOKENS)], axis=1)
    halves = []
    for a in range(N_HALVES):
        xa = xr + pe_ref[a]
        lhs = _split3_cols(xa) if precise else xa.astype(BF16)
        halves.append(jnp.dot(lhs, w1_ref[a], preferred_element_type=F32))
    shift_scr[0:m, :] = halves[1]
    shift_scr[m:m + 8, :] = jnp.zeros((8, shift_scr.shape[1]), F32)
    hid = _silu(halves[0] + shift_scr[pl.ds(1, m), :])
    lhs = _split3_cols(hid) if precise else hid.astype(BF16)
    return jnp.dot(lhs, w2_ref[...], preferred_element_type=F32)


def _compress_kernel(xk_ref, xv_ref, w1k_ref, pek_ref, w2k_ref, w1v_ref, pev_ref, w2v_ref,
                     ck_ref, cv_ref, shift_scr, *, m):
    n_pad = ck_ref.shape[0]
    if n_pad > m:
        ck_ref[m:n_pad, :] = jnp.zeros((n_pad - m, ck_ref.shape[1]), F32)
        cv_ref[m:n_pad, :] = jnp.zeros((n_pad - m, cv_ref.shape[1]), F32)
    ck_ref[0:m, :] = _compress_slot(xk_ref, m, pek_ref, w1k_ref, w2k_ref, shift_scr, True)
    cv_ref[0:m, :] = _compress_slot(xv_ref, m, pev_ref, w1v_ref, w2v_ref, shift_scr, False)


def _compress_prompt(cmp_k, cmp_v, cw, bx, t_len):
    m = t_len // GROUP_TOKENS
    n_pad = -(-m // LANES) * LANES
    kvw = NSA_KV_HEADS * HEAD_DIM
    const = lambda a: pl.BlockSpec(a.shape, lambda b: (0,) * a.ndim, pipeline_mode=pl.Buffered(1))
    x_spec = pl.BlockSpec((t_len, kvw), lambda b: (b, 0))
    o_spec = pl.BlockSpec((None, n_pad, kvw), lambda b: (b, 0, 0))
    return pl.pallas_call(
        functools.partial(_compress_kernel, m=m),
        grid=(bx,),
        in_specs=[x_spec, x_spec] + [const(a) for a in cw],
        out_specs=[o_spec, o_spec],
        out_shape=[jax.ShapeDtypeStruct((bx, n_pad, kvw), F32)] * 2,
        scratch_shapes=[pltpu.VMEM((m + 8, NSA_KV_HEADS * CMP_HID), F32)],
        compiler_params=pltpu.CompilerParams(dimension_semantics=('arbitrary',), vmem_limit_bytes=VMEM_LIMIT),
        name='compress_prompt',
    )(cmp_k, cmp_v, *cw)


SEL_SHIFT = 6


def _stack_heads(x, g, hg, scale):
    return jnp.concatenate(
        [x[:, (g * hg + j) * HEAD_DIM:(g * hg + j + 1) * HEAD_DIM] for j in range(hg)], axis=0) * scale


def _cmp_branch(q_rows, ck_g, cv_g, qpos_rep, hg, tq):
    n_pad = ck_g.shape[0]
    nt = (((1,), (1,)), ((), ()))
    ck_hi = ck_g.astype(BF16)
    ck_lo = (ck_g - ck_hi.astype(F32)).astype(BF16)
    ck3 = jnp.concatenate([ck_hi, ck_lo, ck_hi], axis=1)
    s = lax.dot_general(_split3_cols(q_rows), ck3, nt, preferred_element_type=F32)
    cmp_end = lax.broadcasted_iota(jnp.int32, (1, n_pad), 1) * CMP_STRIDE + (CMP_BLOCK - 1)
    vis = cmp_end <= qpos_rep
    s = jnp.where(vis, s, NEG)
    p = jnp.where(vis, jnp.exp(s - jnp.max(s, axis=1, keepdims=True)), 0.0)
    p = p / jnp.maximum(jnp.sum(p, axis=1, keepdims=True), 1e-30)
    o_cmp = jnp.dot(p.astype(BF16), cv_g.astype(BF16), preferred_element_type=F32)
    p_sum = p[0:tq]
    for j in range(1, hg):
        p_sum = p_sum + p[j * tq:(j + 1) * tq]
    return o_cmp, p_sum


def _select_blocks(p_sum, qpos, n_slc, n_sel):
    tq, n_pad = p_sum.shape
    n_i = lax.broadcasted_iota(jnp.int32, (n_pad, LANES), 0) * CMP_STRIDE
    j_i = lax.broadcasted_iota(jnp.int32, (n_pad, LANES), 1)
    overlap = jnp.logical_and(n_i < (j_i + 1) * SEL_BLOCK, n_i + CMP_BLOCK > j_i * SEL_BLOCK)
    overlap = jnp.where(jnp.logical_and(overlap, j_i < n_slc), 1.0, 0.0).astype(BF16)
    p_hi = p_sum.astype(BF16)
    r1 = p_sum - p_hi.astype(F32)
    p_mid = r1.astype(BF16)
    p_lo = (r1 - p_mid.astype(F32)).astype(BF16)
    imp = (jnp.dot(p_hi, overlap, preferred_element_type=F32)
           + jnp.dot(p_mid, overlap, preferred_element_type=F32)
           + jnp.dot(p_lo, overlap, preferred_element_type=F32))
    jb = lax.broadcasted_iota(jnp.int32, (1, LANES), 1)
    cur = qpos >> SEL_SHIFT
    forced = jnp.logical_or(jb == 0, jnp.logical_or(jb == cur, jb == cur - 1))
    imp = jnp.where(forced, FORCED_SCORE, imp)
    imp = jnp.where(jb > cur, NEG, imp)
    rank = jnp.zeros((tq, LANES), F32)
    for i in range(n_slc):
        vi = imp[:, i:i + 1]
        ahead = jnp.logical_or(vi > imp, jnp.logical_and(vi == imp, jb > i))
        rank = rank + jnp.where(ahead, 1.0, 0.0)
    return jnp.where(jnp.logical_and(rank < float(n_sel), imp > 0.5 * NEG), 1.0, 0.0)


def _cmp_branch_t(qt, ck_g, cvt_g, qpos_t, g, hg, scale):
    n_pad = ck_g.shape[0]
    tq = qt.shape[1]
    q3 = []
    for j in range(hg):
        qh = qt[(g * hg + j) * HEAD_DIM:(g * hg + j + 1) * HEAD_DIM, :] * scale
        q_hi = qh.astype(BF16)
        q_lo = (qh - q_hi.astype(F32)).astype(BF16)
        q3.append(jnp.concatenate([q_hi, q_hi, q_lo], axis=0))
    ck_hi = ck_g.astype(BF16)
    ck_lo = (ck_g - ck_hi.astype(F32)).astype(BF16)
    ck3 = jnp.concatenate([ck_hi, ck_lo, ck_hi], axis=1)
    s = jnp.dot(ck3, jnp.concatenate(q3, axis=1), preferred_element_type=F32)
    cmp_end = lax.broadcasted_iota(jnp.int32, (n_pad, 1), 0) * CMP_STRIDE + (CMP_BLOCK - 1)
    vis = jnp.concatenate([cmp_end <= qpos_t] * hg, axis=1)
    s = jnp.where(vis, s, NEG)
    p = jnp.where(vis, jnp.exp(s - _col_reduce(s, jnp.max)), 0.0)
    p = p / jnp.maximum(_col_reduce(p, jnp.sum), 1e-30)
    o_t = jnp.dot(cvt_g.astype(BF16), p.astype(BF16), preferred_element_type=F32)
    p_sum = p[:, 0:tq]
    for j in range(1, hg):
        p_sum = p_sum + p[:, j * tq:(j + 1) * tq]
    return o_t, p_sum


def _select_blocks_t(p_t, qpos_t, n_slc, n_sel):
    n_pad, tq = p_t.shape
    nb = -(-n_slc // 8) * 8
    j_i = lax.broadcasted_iota(jnp.int32, (nb, n_pad), 0)
    n_i = lax.broadcasted_iota(jnp.int32, (nb, n_pad), 1) * CMP_STRIDE
    overlap = jnp.logical_and(n_i < (j_i + 1) * SEL_BLOCK, n_i + CMP_BLOCK > j_i * SEL_BLOCK)
    overlap = jnp.where(jnp.logical_and(overlap, j_i < n_slc), 1.0, 0.0).astype(BF16)
    p_hi = p_t.astype(BF16)
    r1 = p_t - p_hi.astype(F32)
    p_mid = r1.astype(BF16)
    p_lo = (r1 - p_mid.astype(F32)).astype(BF16)
    imp = (jnp.dot(overlap, p_hi, preferred_element_type=F32)
           + jnp.dot(overlap, p_mid, preferred_element_type=F32)
           + jnp.dot(overlap, p_lo, preferred_element_type=F32))
    jb = lax.broadcasted_iota(jnp.int32, (nb, 1), 0)
    cur = qpos_t >> SEL_SHIFT
    forced = jnp.logical_or(jb == 0, jnp.logical_or(jb == cur, jb == cur - 1))
    imp = jnp.where(forced, FORCED_SCORE, imp)
    imp = jnp.where(jnp.logical_or(jb > cur, jb >= n_slc), NEG, imp)
    rank = jnp.zeros((nb, tq), F32)
    for i in range(n_slc):
        vi = imp[i:i + 1, :]
        ahead = jnp.logical_or(vi > imp, jnp.logical_and(vi == imp, jb > i))
        rank = rank + jnp.where(ahead, 1.0, 0.0)
    return jnp.where(jnp.logical_and(rank < float(n_sel), imp > 0.5 * NEG), 1.0, 0.0)


def _nsa_prompt_kernel(qt_ref, qrt_ref, smallt_ref, ck_ref, cv_ref, slck_ref, wink_ref, slct_ref, wint_ref,
                       o_ref, capt_scr, *, tq, kc, kcw, n_slc, n_sel):
    qs = pl.program_id(1) * tq
    nk = (qs + tq + kc - 1) // kc
    hg = NSA_HEADS // NSA_KV_HEADS
    kvw = NSA_KV_HEADS * HEAD_DIM
    scale = HEAD_DIM ** -0.5
    qpos_t = qs + lax.broadcasted_iota(jnp.int32, (1, tq), 1)
    qt = qt_ref[...]
    qrt = qrt_ref[...]
    gates_t = _sigmoid(smallt_ref[...])
    ck = ck_ref[...]
    cv_t = cv_ref[...].T
    nb = -(-n_slc // 8) * 8
    blk_l = lax.broadcasted_iota(jnp.int32, (1, nb), 1)

    def win_cap(k0):
        dist = qpos_t - (k0 + lax.broadcasted_iota(jnp.int32, (kcw, 1), 0))
        return jnp.where(jnp.logical_and(dist >= 0, dist < WINDOW), POS_INF, NEG)

    win_base = jnp.maximum(qs + tq - kcw, 0)

    groups = range(NSA_KV_HEADS)
    o_cmps, sel_ts = [], []
    for g in groups:
        o_cmp, p_sum = _cmp_branch_t(qt, ck[:, g * HEAD_DIM:(g + 1) * HEAD_DIM],
                                     cv_t[g * HEAD_DIM:(g + 1) * HEAD_DIM, :], qpos_t, g, hg, scale)
        o_cmps.append(o_cmp)
        sel_ts.append(_select_blocks_t(p_sum, qpos_t, n_slc, n_sel).astype(BF16))

    def cap_body(c, _):
        k0 = pl.multiple_of(c * kc, kc)
        kpos = k0 + lax.broadcasted_iota(jnp.int32, (kc, 1), 0)
        expand = jnp.where((kpos >> SEL_SHIFT) == blk_l, 1.0, 0.0).astype(BF16)
        causal = kpos <= qpos_t
        for g in groups:
            sel_k = jnp.dot(expand, sel_ts[g], preferred_element_type=F32)
            capt_scr[g, pl.ds(k0, kc), :] = jnp.where(jnp.logical_and(sel_k > 0.5, causal), POS_INF, NEG)
        return 0

    lax.fori_loop(0, nk, cap_body, 0)
    q_pads = [_q_pad(qrt, g, hg, scale) for g in groups]
    vrows = [kvw + g * HEAD_DIM for g in groups]
    o_slcs = _flash_t(q_pads, slck_ref, slct_ref, vrows,
                      [functools.partial(lambda g, k0: capt_scr[g, pl.ds(k0, kc), :], g) for g in groups],
                      0, nk, kc, hg)
    o_wins = _flash_t(q_pads, wink_ref, wint_ref, vrows, [win_cap] * NSA_KV_HEADS, win_base, 1, kcw, hg)
    for g in groups:
        for j in range(hg):
            h = g * hg + j
            r = IDX_HEADS + 3 * h
            cols = slice(j * tq, (j + 1) * tq)
            o_t = (gates_t[r:r + 1, :] * o_cmps[g][:, cols] + gates_t[r + 1:r + 2, :] * o_slcs[g][:, cols]
                   + gates_t[r + 2:r + 3, :] * o_wins[g][:, cols])
            o_ref[:, h * HEAD_DIM:(h + 1) * HEAD_DIM] = o_t.T


def _nsa_prompt(f, ck, cv, bx, t_len, tq, kc, kcw):
    nq = t_len // tq
    hg = NSA_HEADS // NSA_KV_HEADS
    n_pad = ck.shape[1]
    n_slc = -(-t_len // SEL_BLOCK)
    kvw = NSA_KV_HEADS * HEAD_DIM
    row = lambda w: pl.BlockSpec((tq, w), lambda b, i: (b * nq + i, 0))
    cmp_spec = pl.BlockSpec((None, n_pad, kvw), lambda b, i: (b, 0, 0))
    keys = pl.BlockSpec((t_len, kvw), lambda b, i: (b, 0))
    return pl.pallas_call(
        functools.partial(_nsa_prompt_kernel, tq=tq, kc=kc, kcw=kcw, n_slc=n_slc, n_sel=min(N_SEL, n_slc)),
        grid=(bx, nq),
        in_specs=[pl.BlockSpec((None, W_NQ, tq), lambda b, i: (b, 0, i)),
                  pl.BlockSpec((None, W_NQ, tq), lambda b, i: (b, 0, i)),
                  pl.BlockSpec((None, SMALL_ROWS, tq), lambda b, i: (b, 0, i)),
                  cmp_spec, cmp_spec, keys, keys,
                  pl.BlockSpec((None, 2 * kvw, t_len), lambda b, i: (b, 1, 0)),
                  pl.BlockSpec((None, 2 * kvw, t_len), lambda b, i: (b, 0, 0))],
        out_specs=row(W_NQ),
        out_shape=jax.ShapeDtypeStruct((bx * t_len, W_NQ), F32),
        scratch_shapes=[pltpu.VMEM((NSA_KV_HEADS, t_len, tq), F32)],
        compiler_params=pltpu.CompilerParams(dimension_semantics=('arbitrary', 'arbitrary'),
                                             vmem_limit_bytes=VMEM_LIMIT),
        name='nsa_prompt',
    )(f['nsa_q_t'], f['nsa_q_rot_t'], f['small_t'], ck, cv, f['slc_k'], f['win_k'], f['nsa_t'], f['win_t'])


DEC_ROWS = 8


def _page_copies(cache_hbm, pt_ref, b, n_pages, page, buf, sem, slot):
    return [pltpu.make_async_copy(cache_hbm.at[pt_ref[b * n_pages + j]],
                                  buf.at[slot, :, pl.ds(j * page, page)], sem.at[slot])
            for j in range(n_pages)]


def _gather_step(caches, pt_ref, n_pages, page, bufs, sems):
    b = pl.program_id(0)
    nb = pl.num_programs(0)
    slot = b % 2

    @pl.when(b == 0)
    def _():
        for cache, buf, sem in zip(caches, bufs, sems):
            for cp in _page_copies(cache, pt_ref, 0, n_pages, page, buf, sem, 0):
                cp.start()

    @pl.when(b + 1 < nb)
    def _():
        for cache, buf, sem in zip(caches, bufs, sems):
            for cp in _page_copies(cache, pt_ref, b + 1, n_pages, page, buf, sem, 1 - slot):
                cp.start()

    for cache, buf, sem in zip(caches, bufs, sems):
        for cp in _page_copies(cache, pt_ref, b, n_pages, page, buf, sem, slot):
            cp.wait()
    return slot


def _attend_full(q_rows, kt, vt, cap, n_rep):
    nt = (((1,), (1,)), ((), ()))
    s = jnp.dot(q_rows, kt.astype(BF16), preferred_element_type=F32)
    s = jnp.minimum(s, jnp.concatenate([cap] * n_rep, axis=0))
    p = jnp.exp(s - jnp.max(s, axis=1, keepdims=True))
    den = jnp.maximum(jnp.sum(p, axis=1, keepdims=True), 1e-30)
    return lax.dot_general(p.astype(BF16), vt.astype(BF16), nt, preferred_element_type=F32) / den


def _decode_nsa_kernel(pt_ref, q_ref, qr_ref, small_ref, cache_hbm, new_ref, wst_ref, wnew_ref,
                       w1k_ref, pek_ref, w2k_ref, w1v_ref, pev_ref, w2v_ref, o_ref,
                       buf, sem, xk_scr, xv_scr, shift_scr, *, past, t_dec, n_pages, page):
    slot = _gather_step([cache_hbm], pt_ref, n_pages, page, [buf], [sem])
    rows = DEC_ROWS
    hg = NSA_HEADS // NSA_KV_HEADS
    kvw = NSA_KV_HEADS * HEAD_DIM
    scale = HEAD_DIM ** -0.5
    n_keys = buf.shape[2]
    buf[slot, :, pl.ds(past, LANES)] = new_ref[...]
    qpos = past + jnp.minimum(lax.broadcasted_iota(jnp.int32, (rows, 1), 0), t_dec - 1)
    qpos_rep = jnp.concatenate([qpos] * hg, axis=0)

    for j in range(n_pages):
        xk_scr[j * page:(j + 1) * page, :] = buf[slot, 0:kvw, j * page:(j + 1) * page].T
        xv_scr[j * page:(j + 1) * page, :] = buf[slot, kvw:2 * kvw, j * page:(j + 1) * page].T
    m = past // GROUP_TOKENS
    ck = _compress_slot(xk_scr, m, pek_ref, w1k_ref, w2k_ref, shift_scr, True)
    cv = _compress_slot(xv_scr, m, pev_ref, w1v_ref, w2v_ref, shift_scr, False)

    q = q_ref[...]
    qr = qr_ref[...]
    gates = _sigmoid(small_ref[...])
    n_slc = -(-(past + t_dec) // SEL_BLOCK)
    kpos = lax.broadcasted_iota(jnp.int32, (1, n_keys), 1)
    blk_i = lax.broadcasted_iota(jnp.int32, (LANES, 1), 0)
    expand = jnp.where((kpos >> SEL_SHIFT) == blk_i, 1.0, 0.0).astype(BF16)
    wb = wst_ref.shape[1]
    wpos = past - wb + lax.broadcasted_iota(jnp.int32, (1, wb + LANES), 1)
    wdist = qpos - wpos
    win_cap = jnp.where(jnp.logical_and(wdist >= 0, wdist < WINDOW), POS_INF, NEG)

    for g in range(NSA_KV_HEADS):
        gs = slice(g * HEAD_DIM, (g + 1) * HEAD_DIM)
        o_cmp, p_sum = _cmp_branch(_stack_heads(q, g, hg, scale), ck[:, gs], cv[:, gs], qpos_rep, hg, rows)
        sel = _select_blocks(p_sum, qpos, n_slc, min(N_SEL, n_slc)).astype(BF16)
        sel_k = jnp.dot(sel, expand, preferred_element_type=F32)
        slc_cap = jnp.where(jnp.logical_and(sel_k > 0.5, kpos <= qpos), POS_INF, NEG)
        qr_rows = _stack_heads(qr, g, hg, scale).astype(BF16)
        o_slc = _attend_full(qr_rows, buf[slot, 2 * kvw + g * HEAD_DIM:2 * kvw + (g + 1) * HEAD_DIM, :],
                             buf[slot, 3 * kvw + g * HEAD_DIM:3 * kvw + (g + 1) * HEAD_DIM, :], slc_cap, hg)
        wk = jnp.concatenate([wst_ref[g * HEAD_DIM:(g + 1) * HEAD_DIM, :],
                              wnew_ref[g * HEAD_DIM:(g + 1) * HEAD_DIM, :]], axis=1)
        wv = jnp.concatenate([wst_ref[kvw + g * HEAD_DIM:kvw + (g + 1) * HEAD_DIM, :],
                              wnew_ref[kvw + g * HEAD_DIM:kvw + (g + 1) * HEAD_DIM, :]], axis=1)
        o_win = _attend_full(qr_rows, wk, wv, win_cap, hg)
        for j in range(hg):
            h = g * hg + j
            lane = IDX_HEADS + 3 * h
            rs = slice(j * rows, (j + 1) * rows)
            o_ref[:, h * HEAD_DIM:(h + 1) * HEAD_DIM] = (gates[:, lane:lane + 1] * o_cmp[rs]
                                                         + gates[:, lane + 1:lane + 2] * o_slc[rs]
                                                         + gates[:, lane + 2:lane + 3] * o_win[rs])


def _decode_dsa_kernel(pt_ref, iq_ref, small_ref, dq_ref, idx_hbm, kv_hbm, inew_ref, kvnew_ref, o_ref,
                       ibuf, kbuf, isem, ksem, *, past, t_dec, n_pages, page, topk):
    slot = _gather_step([idx_hbm, kv_hbm], pt_ref, n_pages, page, [ibuf, kbuf], [isem, ksem])
    rows = DEC_ROWS
    hg = DSA_HEADS // DSA_KV_HEADS
    scale = HEAD_DIM ** -0.5
    n_keys = ibuf.shape[2]
    ibuf[slot, :, pl.ds(past, LANES)] = inew_ref[...]
    kbuf[slot, :, pl.ds(past, LANES)] = kvnew_ref[...]
    qpos = past + jnp.minimum(lax.broadcasted_iota(jnp.int32, (rows, 1), 0), t_dec - 1)

    ik = ibuf[slot]
    ik_hi = ik.astype(BF16)
    ik_lo = (ik - ik_hi.astype(F32)).astype(BF16)
    k3 = jnp.concatenate([ik_hi, ik_lo, ik_hi], axis=0)
    iq = iq_ref[...]
    w = small_ref[...]
    sc = jnp.zeros((rows, n_keys), F32)
    for h in range(IDX_HEADS):
        d = jnp.dot(_split3_cols(iq[:, h * IDX_DIM:(h + 1) * IDX_DIM]), k3, preferred_element_type=F32)
        sc = sc + jnp.maximum(d, 0.0) * w[:, h:h + 1]
    sc = sc * ((IDX_DIM ** -0.5) * (IDX_HEADS ** -0.5))
    kpos = lax.broadcasted_iota(jnp.int32, (1, n_keys), 1)
    real_row = lax.broadcasted_iota(jnp.int32, (rows, 1), 0) < t_dec
    ok = jnp.logical_and(kpos <= qpos, jnp.logical_or(real_row, kpos == 0))
    mn = jnp.min(jnp.where(ok, sc, POS_INF), axis=1, keepdims=True)
    mx = jnp.max(jnp.where(ok, sc, NEG), axis=1, keepdims=True)
    n_valid = jnp.where(real_row, qpos + 1, 1).astype(F32)
    cap = _topk_cap_rows(jnp.where(ok, sc, -POS_INF), mn, mx, n_valid, topk)

    dq = dq_ref[...]
    for g in range(DSA_KV_HEADS):
        q_rows = _stack_heads(dq, g, hg, scale).astype(BF16)
        o = _attend_full(q_rows, kbuf[slot, g * HEAD_DIM:(g + 1) * HEAD_DIM, :],
                         kbuf[slot, (DSA_KV_HEADS + g) * HEAD_DIM:(DSA_KV_HEADS + g + 1) * HEAD_DIM, :], cap, hg)
        for j in range(hg):
            h = g * hg + j
            o_ref[:, h * HEAD_DIM:(h + 1) * HEAD_DIM] = o[j * rows:(j + 1) * rows]


def _pad_rows(a2d, bd, t_dec):
    return jnp.pad(a2d.reshape(bd, t_dec, -1), ((0, 0), (0, DEC_ROWS - t_dec), (0, 0)))


def _new_cols(a_t, bd, t_dec):
    f = a_t.shape[0]
    return jnp.pad(a_t.reshape(f, bd, t_dec).transpose(1, 0, 2), ((0, 0), (0, 0), (0, LANES - t_dec)))


def _feature_major_pages(pool):
    n_pool, page = pool.shape[:2]
    nd = pool.ndim
    return pool.transpose((0,) + tuple(range(2, nd)) + (1,)).reshape(n_pool, -1, page)


def _decode_nsa(fs, cache_nsa, win_state, page_table, cw, past, t_dec):
    bd, n_pages = page_table.shape
    pool = _feature_major_pages(cache_nsa)
    page = pool.shape[2]
    f = pool.shape[1]
    n_keys = past + LANES
    wst = _feature_major_pages(win_state)
    kvw = NSA_KV_HEADS * HEAD_DIM
    per_b = lambda *shape: pl.BlockSpec((None,) + shape, lambda b, pt: (b,) + (0,) * len(shape))
    const = lambda a: pl.BlockSpec(a.shape, lambda b, pt: (0,) * a.ndim, pipeline_mode=pl.Buffered(1))
    grid_spec = pltpu.PrefetchScalarGridSpec(
        num_scalar_prefetch=1,
        grid=(bd,),
        in_specs=[per_b(DEC_ROWS, W_NQ), per_b(DEC_ROWS, W_NQ), per_b(DEC_ROWS, LANES),
                  pl.BlockSpec(memory_space=pl.ANY), per_b(f, LANES),
                  per_b(2 * kvw, wst.shape[2]), per_b(2 * kvw, LANES)] + [const(a) for a in cw],
        out_specs=per_b(DEC_ROWS, W_NQ),
        scratch_shapes=[pltpu.VMEM((2, f, n_keys), F32), pltpu.SemaphoreType.DMA((2,)),
                        pltpu.VMEM((past, kvw), F32), pltpu.VMEM((past, kvw), F32),
                        pltpu.VMEM((past // GROUP_TOKENS + 8, NSA_KV_HEADS * CMP_HID), F32)])
    out = pl.pallas_call(
        functools.partial(_decode_nsa_kernel, past=past, t_dec=t_dec, n_pages=n_pages, page=page),
        grid_spec=grid_spec,
        out_shape=jax.ShapeDtypeStruct((bd, DEC_ROWS, W_NQ), F32),
        compiler_params=pltpu.CompilerParams(dimension_semantics=('arbitrary',), vmem_limit_bytes=VMEM_LIMIT),
        name='decode_nsa',
    )(page_table.reshape(-1), _pad_rows(fs['nsa_q'], bd, t_dec), _pad_rows(fs['nsa_q_rot'], bd, t_dec),
      _pad_rows(fs['small'], bd, t_dec), pool, _new_cols(fs['nsa_t'], bd, t_dec), wst,
      _new_cols(fs['win_t'], bd, t_dec), *cw)
    return out[:, :t_dec].reshape(bd * t_dec, W_NQ)


def _decode_dsa(fs, cache_dsa, cache_idx, page_table, past, t_dec):
    bd, n_pages = page_table.shape
    kpool = _feature_major_pages(cache_dsa)
    ipool = _feature_major_pages(cache_idx)
    page = kpool.shape[2]
    n_keys = past + LANES
    topk = min(DSA_TOPK_MAX, (past + t_dec) // 4)
    per_b = lambda *shape: pl.BlockSpec((None,) + shape, lambda b, pt: (b,) + (0,) * len(shape))
    grid_spec = pltpu.PrefetchScalarGridSpec(
        num_scalar_prefetch=1,
        grid=(bd,),
        in_specs=[per_b(DEC_ROWS, W_IQ), per_b(DEC_ROWS, LANES), per_b(DEC_ROWS, W_DQ),
                  pl.BlockSpec(memory_space=pl.ANY), pl.BlockSpec(memory_space=pl.ANY),
                  per_b(IDX_DIM, LANES), per_b(W_DKV, LANES)],
        out_specs=per_b(DEC_ROWS, W_DQ),
        scratch_shapes=[pltpu.VMEM((2, IDX_DIM, n_keys), F32), pltpu.VMEM((2, W_DKV, n_keys), F32),
                        pltpu.SemaphoreType.DMA((2,)), pltpu.SemaphoreType.DMA((2,))])
    out = pl.pallas_call(
        functools.partial(_decode_dsa_kernel, past=past, t_dec=t_dec, n_pages=n_pages, page=page, topk=topk),
        grid_spec=grid_spec,
        out_shape=jax.ShapeDtypeStruct((bd, DEC_ROWS, W_DQ), F32),
        compiler_params=pltpu.CompilerParams(dimension_semantics=('arbitrary',), vmem_limit_bytes=VMEM_LIMIT),
        name='decode_dsa',
    )(page_table.reshape(-1), _pad_rows(fs['idx_q'], bd, t_dec), _pad_rows(fs['small'], bd, t_dec),
      _pad_rows(fs['dsa_q'], bd, t_dec), ipool, kpool, _new_cols(fs['idx_t'], bd, t_dec),
      _new_cols(fs['dsa_t'], bd, t_dec))
    return out[:, :t_dec].reshape(bd * t_dec, W_DQ)


def _rows_from_t(a_t, bx, t, lead):
    return a_t.reshape(lead + (bx, t)).transpose((len(lead), len(lead) + 1) + tuple(range(len(lead))))


def _rows_from_bt(a_bt, lead):
    bx, _, t = a_bt.shape
    n = len(lead)
    return a_bt.reshape((bx,) + lead + (t,)).transpose((0, n + 1) + tuple(range(1, n + 1)))


def kernel(x_prompt, x_sample, cache_nsa_kv, cache_dsa_kv, cache_dsa_idx_k, state_nsa_win_kv, page_table,
           c_prompt, c_sample, norm_g, w_ada, b_ada, w_in, cmp_pe, cmp_w1, cmp_w2,
           w_up_nsa, w_up_dsa, w_out, final_g):
    depth = norm_g.shape[0]
    assert depth == 1
    B, T, D = x_prompt.shape
    Bd, Td, _ = x_sample.shape
    l = 0
    past = page_table.shape[1] * cache_nsa_kv.shape[2]

    c_all = jnp.concatenate([c_prompt, c_sample], axis=0)
    pad = (-c_all.shape[0]) % 8
    c_all = jnp.pad(c_all, ((0, pad), (0, 0)))
    mod = _ada_mod(c_all, w_ada[l], b_ada[l])
    shift, scale, gate = mod[:, 0:D], mod[:, D:2 * D], mod[:, 2 * D:3 * D]

    weights = _prep_inproj_weights(w_in[l], D)

    tm = PROJ_ROWS
    fp = _in_projection_prompt(x_prompt.reshape(B * T, D), norm_g[l], scale[:B], shift[:B], T,
                               _prep_inproj_weights_prompt(w_in[l], D), tm)
    nsa_rows = _rows_from_bt(fp['nsa_t'], (4, NSA_KV_HEADS, HEAD_DIM))
    wb_p = min(WINDOW, T)
    new_win_prompt = _rows_from_bt(fp['win_t'][:, :, T - wb_p:], (2, NSA_KV_HEADS, HEAD_DIM))
    dsa_rows = _rows_from_bt(fp['dsa_t'], (2, DSA_KV_HEADS, HEAD_DIM))
    idx_rows = _rows_from_bt(fp['idx_t'], (IDX_DIM,))
    cw = _prep_compress_weights(cmp_pe[l], cmp_w1[l], cmp_w2[l])
    ck, cv = _compress_prompt(fp['cmp_k'], fp['cmp_v'], cw, B, T)
    kc = min(KEY_CHUNK, T)
    o_n = _nsa_prompt(fp, ck, cv, B, T, Q_TILE, kc, min(WINDOW + Q_TILE, T))
    o_d = _dsa_prompt(fp, B, T, Q_TILE, kc)
    y_prompt = _mixer_out(x_prompt.reshape(B * T, D), o_n, o_d, fp, gate[:B], T, final_g,
                          w_up_nsa[l], w_up_dsa[l], w_out[l], tm).reshape(B, T, D)

    Rs = Bd * Td
    tms = min(PROJ_ROWS, Rs)
    pos_s = past + jnp.arange(Td, dtype=jnp.int32)
    pos_rows = jnp.tile(pos_s, tms // Td)
    rep = lambda a: jnp.repeat(a[B:B + Bd], Td, axis=0)
    fs = _in_projection(x_sample.reshape(Rs, D), norm_g[l], rep(scale), rep(shift), 1, pos_rows, weights, tms, 1)
    for name in ('nsa_t', 'win_t', 'dsa_t', 'idx_t'):
        fs[name] = fs[name][0]
    s_nsa_rows = _rows_from_t(fs['nsa_t'], Bd, Td, (4, NSA_KV_HEADS, HEAD_DIM))
    s_win_rows = _rows_from_t(fs['win_t'], Bd, Td, (2, NSA_KV_HEADS, HEAD_DIM))
    s_dsa_rows = _rows_from_t(fs['dsa_t'], Bd, Td, (2, DSA_KV_HEADS, HEAD_DIM))
    s_idx_rows = _rows_from_t(fs['idx_t'], Bd, Td, (IDX_DIM,))

    win_state = state_nsa_win_kv[l]
    so_n = _decode_nsa(fs, cache_nsa_kv[l], win_state, page_table, cw, past, Td)
    so_d = _decode_dsa(fs, cache_dsa_kv[l], cache_dsa_idx_k[l], page_table, past, Td)
    y_sample = _mixer_out(x_sample.reshape(Rs, D), so_n, so_d, fs,
                          rep(gate), 1, final_g, w_up_nsa[l], w_up_dsa[l], w_out[l], tms).reshape(Bd, Td, D)
    new_win_sample = jnp.concatenate([win_state, s_win_rows], axis=1)[:, Td:]

    return (y_prompt, y_sample, nsa_rows[None], dsa_rows[None], idx_rows[None], new_win_prompt[None],
            s_nsa_rows[None], s_dsa_rows[None], s_idx_rows[None], new_win_sample[None])
```

```python
import functools

import numpy as np
import jax
import jax.numpy as jnp
from jax import lax
from jax.experimental import pallas as pl
from jax.experimental.pallas import tpu as pltpu

F32 = jnp.float32
BF16 = jnp.bfloat16

HEAD_DIM = 64
NSA_HEADS = 8
NSA_KV_HEADS = 2
DSA_HEADS = 8
DSA_KV_HEADS = 2
N_NSA_KV = 6
CMP_BLOCK = 32
CMP_STRIDE = 16
SEL_BLOCK = 64
N_SEL = 16
WINDOW = 512
IDX_HEADS = 4
IDX_DIM = 64
DSA_TOPK_MAX = 256
ROPE_THETA = 10000.0
EPS = 1e-6
NEG = -1e30
FORCED_SCORE = 1e4

LANES = 128
VMEM_LIMIT = 56 * 1024 * 1024

PROJ_ROWS = 256
Q_TILE = LANES
KEY_CHUNK = 1024

W_NQ = NSA_HEADS * HEAD_DIM
W_NKV = N_NSA_KV * NSA_KV_HEADS * HEAD_DIM
W_NG = NSA_HEADS * 3
W_DQ = DSA_HEADS * HEAD_DIM
W_DKV = 2 * DSA_KV_HEADS * HEAD_DIM
W_IQ = IDX_HEADS * IDX_DIM


def _cuts(d_model):
    widths = (W_NQ, W_NKV, W_NG, W_NQ, W_DQ, W_DKV, W_IQ, IDX_DIM, IDX_HEADS, W_DQ, 2 * d_model)
    c = np.concatenate([[0], np.cumsum(widths)])
    names = ('nq', 'nkv', 'ng', 'nz', 'dq', 'dkv', 'iq', 'ik', 'iw', 'dz', 'mg')
    return {n: (int(c[i]), int(c[i + 1])) for i, n in enumerate(names)}


def _split3_rows(w):
    wh = w.astype(BF16)
    wl = (w - wh.astype(F32)).astype(BF16)
    return jnp.concatenate([wh, wl, wh], axis=0)


def _split3_cols(x):
    xh = x.astype(BF16)
    xl = (x - xh.astype(F32)).astype(BF16)
    return jnp.concatenate([xh, xh, xl], axis=1)


def _silu(v):
    return v * (1.0 / (1.0 + jnp.exp(-v)))


def _sigmoid(v):
    return 1.0 / (1.0 + jnp.exp(-v))


def _ada_kernel(c_ref, w_ref, b_ref, o_ref):
    c3 = _split3_cols(_silu(c_ref[...]))
    o_ref[...] = jnp.dot(c3, w_ref[...], preferred_element_type=F32) + b_ref[...]


def _ada_mod(c_all, w_ada, b_ada):
    m, d = c_all.shape
    n = w_ada.shape[1]
    tn = 512
    w3 = _split3_rows(w_ada)
    return pl.pallas_call(
        _ada_kernel,
        grid=(n // tn,),
        in_specs=[pl.BlockSpec((m, d), lambda j: (0, 0)),
                  pl.BlockSpec((3 * d, tn), lambda j: (0, j)),
                  pl.BlockSpec((1, tn), lambda j: (0, j))],
        out_specs=pl.BlockSpec((m, tn), lambda j: (0, j)),
        out_shape=jax.ShapeDtypeStruct((m, n), F32),
        name='ada_mod',
    )(c_all, w3, b_ada.reshape(1, n))


def _rope_tf(v, cos, sin_signed):
    first_half = (lax.broadcasted_iota(jnp.int32, (1, LANES), 1) % HEAD_DIM) < (HEAD_DIM // 2)
    outs = []
    for c in range(v.shape[1] // LANES):
        vc = v[:, c * LANES:(c + 1) * LANES]
        sw = jnp.where(first_half, pltpu.roll(vc, LANES - HEAD_DIM // 2, 1), pltpu.roll(vc, HEAD_DIM // 2, 1))
        outs.append(vc * cos + sw * sin_signed)
    return jnp.concatenate(outs, axis=1) if len(outs) > 1 else outs[0]


def _rope_ft(v, cos_t, sin_t):
    half = HEAD_DIM // 2
    outs = []
    for hd in range(v.shape[0] // HEAD_DIM):
        x1 = v[hd * HEAD_DIM:hd * HEAD_DIM + half]
        x2 = v[hd * HEAD_DIM + half:(hd + 1) * HEAD_DIM]
        outs.append(x1 * cos_t - x2 * sin_t)
        outs.append(x2 * cos_t + x1 * sin_t)
    return jnp.concatenate(outs, axis=0)


def _inproj_kernel(x_ref, g_ref, sc_ref, sh_ref, cos_ref, sin_ref, cost_ref, sint_ref,
                   wp_ref, wa_ref, wt_ref, wti_ref,
                   q_ref, qrot_ref, iq_ref, small_ref, ck_ref,
                   cv_ref, snz_ref, sdz_ref, dq_ref, mg_ref,
                   nsat_ref, wint_ref, dsat_ref, idxt_ref, idx3_ref):
    x = x_ref[...]
    y = x * lax.rsqrt(jnp.mean(x * x, axis=-1, keepdims=True) + EPS) * g_ref[...]
    h = y * (1.0 + sc_ref[...]) + sh_ref[...]
    hh = h.astype(BF16)
    hl = (h - hh.astype(F32)).astype(BF16)
    h3 = jnp.concatenate([hh, hh, hl], axis=1)
    cos = cos_ref[...]
    sin = sin_ref[...]
    cos_t = cost_ref[...]
    sin_t = sint_ref[...]

    p = jnp.dot(h3, wp_ref[...], preferred_element_type=F32)
    q = p[:, 0:W_NQ]
    q_ref[...] = q
    qrot_ref[...] = _rope_tf(q, cos, sin)
    iq_ref[...] = _rope_tf(p[:, W_NQ:W_NQ + W_IQ], cos, sin)
    small_ref[...] = p[:, W_NQ + W_IQ:W_NQ + W_IQ + LANES]
    ck_ref[...] = p[:, W_NQ + W_IQ + LANES:W_NQ + W_IQ + 2 * LANES]

    a = jnp.dot(hh, wa_ref[...], preferred_element_type=F32)
    cv_ref[...] = a[:, 0:LANES]
    o = LANES
    snz_ref[...] = _silu(a[:, o:o + W_NQ])
    o += W_NQ
    sdz_ref[...] = _silu(a[:, o:o + W_DQ])
    o += W_DQ
    dq_ref[...] = _rope_tf(a[:, o:o + W_DQ], cos, sin)
    o += W_DQ
    mg_ref[...] = _sigmoid(a[:, o:])

    nt = (((1,), (1,)), ((), ()))
    t = lax.dot_general(wt_ref[...], hh, nt, preferred_element_type=F32)
    kv = 2 * HEAD_DIM
    nsat_ref[0:2 * kv, :] = t[0:2 * kv]
    nsat_ref[2 * kv:3 * kv, :] = _rope_ft(t[2 * kv:3 * kv], cos_t, sin_t)
    nsat_ref[3 * kv:4 * kv, :] = t[3 * kv:4 * kv]
    wint_ref[0:kv, :] = _rope_ft(t[4 * kv:5 * kv], cos_t, sin_t)
    wint_ref[kv:2 * kv, :] = t[5 * kv:6 * kv]
    dsat_ref[0:kv, :] = _rope_ft(t[6 * kv:7 * kv], cos_t, sin_t)
    dsat_ref[kv:2 * kv, :] = t[7 * kv:8 * kv]
    ti = lax.dot_general(wti_ref[...], h3, nt, preferred_element_type=F32)
    ti = _rope_ft(ti, cos_t, sin_t)
    idxt_ref[...] = ti
    ti_hi = ti.astype(BF16)
    ti_lo = (ti - ti_hi.astype(F32)).astype(BF16)
    idx3_ref[...] = jnp.concatenate([ti_hi, ti_lo, ti_hi], axis=0)


def _prep_inproj_weights(w_in, d_model):
    c = _cuts(d_model)
    sl = lambda name: w_in[:, c[name][0]:c[name][1]]
    nkv = sl('nkv')
    kv = NSA_KV_HEADS * HEAD_DIM
    small = jnp.concatenate([sl('iw'), sl('ng'), jnp.zeros((d_model, LANES - IDX_HEADS - W_NG), F32)], axis=1)
    wp = jnp.concatenate([sl('nq'), sl('iq'), small, nkv[:, 0:kv]], axis=1)
    wa = jnp.concatenate([nkv[:, kv:2 * kv], sl('nz'), sl('dz'), sl('dq'), sl('mg')], axis=1)
    wt = jnp.concatenate([nkv, sl('dkv')], axis=1).T
    wti = sl('ik')
    return _split3_rows(wp), wa.astype(BF16), wt.astype(BF16), _split3_rows(wti).T


def _rope_tables(pos):
    half = HEAD_DIM // 2
    inv = ROPE_THETA ** (-jnp.arange(half, dtype=F32) / half)
    ang = pos.astype(F32)[:, None] * inv[None, :]
    cos = jnp.cos(ang)
    sin = jnp.sin(ang)
    cos_tf = jnp.tile(cos, (1, LANES // half))
    sin_tf = jnp.tile(jnp.concatenate([-sin, sin], axis=1), (1, LANES // HEAD_DIM))
    return cos_tf, sin_tf, cos.T, sin.T


def _in_projection(x2d, norm_g, scale, shift, rows_per_mod, pos, weights, tm, nb):
    r, d = x2d.shape
    tpb = (r // nb) // tm
    wp, wa, wt, wti = weights
    n_tiles = r // tm
    cos_tf, sin_tf, cos_t, sin_t = _rope_tables(pos)
    p_tiles = pos.shape[0] // tm
    if rows_per_mod == 1:
        mod_spec = pl.BlockSpec((tm, d), lambda i: (i, 0))
        sc, sh = scale, shift
    else:
        per = rows_per_mod // tm
        mod_spec = pl.BlockSpec((None, 1, d), lambda i: (i // per, 0, 0))
        sc, sh = scale[:, None, :], shift[:, None, :]
    const = lambda shape: pl.BlockSpec(shape, lambda i: (0,) * len(shape), pipeline_mode=pl.Buffered(1))
    tf = lambda w: pl.BlockSpec((tm, w), lambda i: (i, 0))
    ft = lambda w: pl.BlockSpec((None, w, tm), lambda i: (i // tpb, 0, i % tpb))
    tf_widths = (W_NQ, W_NQ, W_IQ, LANES, LANES, LANES, W_NQ, W_DQ, W_DQ, 2 * d)
    ft_widths = (4 * LANES, 2 * LANES, 2 * LANES, IDX_DIM)
    outs = pl.pallas_call(
        _inproj_kernel,
        grid=(n_tiles,),
        in_specs=[pl.BlockSpec((tm, d), lambda i: (i, 0)),
                  const((1, d)), mod_spec, mod_spec,
                  pl.BlockSpec((tm, LANES), lambda i: (i % p_tiles, 0)),
                  pl.BlockSpec((tm, LANES), lambda i: (i % p_tiles, 0)),
                  pl.BlockSpec((HEAD_DIM // 2, tm), lambda i: (0, i % p_tiles)),
                  pl.BlockSpec((HEAD_DIM // 2, tm), lambda i: (0, i % p_tiles)),
                  const(wp.shape), const(wa.shape), const(wt.shape), const(wti.shape)],
        out_specs=[tf(w) for w in tf_widths] + [ft(w) for w in ft_widths] + [ft(3 * IDX_DIM)],
        out_shape=[jax.ShapeDtypeStruct((r, w), F32) for w in tf_widths]
        + [jax.ShapeDtypeStruct((nb, w, r // nb), F32) for w in ft_widths]
        + [jax.ShapeDtypeStruct((nb, 3 * IDX_DIM, r // nb), BF16)],
        compiler_params=pltpu.CompilerParams(dimension_semantics=('arbitrary',), vmem_limit_bytes=VMEM_LIMIT),
        name='in_projection',
    )(x2d, norm_g.reshape(1, d), sc, sh, cos_tf, sin_tf, cos_t, sin_t, wp, wa, wt, wti)
    names = ('nsa_q', 'nsa_q_rot', 'idx_q', 'small', 'cmp_k', 'cmp_v', 'silu_nz', 'silu_dz', 'dsa_q', 'merge',
             'nsa_t', 'win_t', 'dsa_t', 'idx_t', 'idx3_t')
    return dict(zip(names, outs))


SMALL_ROWS = 32


def _inproj_prompt_kernel(x_ref, g_ref, sc_ref, sh_ref, cos_ref, sin_ref, cost_ref, sint_ref,
                          wp_ref, wa_ref, wtp_ref, wt_ref,
                          ck_ref, idx3_ref,
                          cv_ref, slck_ref, wink_ref, dsak_ref, snz_ref, sdz_ref, mg_ref,
                          qt_ref, iqt_ref, smallt_ref, qrt_ref, dqt_ref, nsat_ref, wint_ref, dsat_ref, idxt_ref):
    x = x_ref[...]
    y = x * lax.rsqrt(jnp.mean(x * x, axis=-1, keepdims=True) + EPS) * g_ref[...]
    h = y * (1.0 + sc_ref[...]) + sh_ref[...]
    hh = h.astype(BF16)
    hl = (h - hh.astype(F32)).astype(BF16)
    h3 = jnp.concatenate([hh, hh, hl], axis=1)
    cos = cos_ref[...]
    sin = sin_ref[...]
    cos_t = cost_ref[...]
    sin_t = sint_ref[...]
    nt = (((1,), (1,)), ((), ()))
    kv = NSA_KV_HEADS * HEAD_DIM

    p = jnp.dot(h3, wp_ref[...], preferred_element_type=F32)
    ck_ref[...] = p[:, 0:LANES]
    ik = _rope_tf(p[:, LANES:2 * LANES], cos, sin)
    ik_hi = ik.astype(BF16)
    ik_lo = (ik - ik_hi.astype(F32)).astype(BF16)
    first = lax.broadcasted_iota(jnp.int32, (1, LANES), 1) < IDX_DIM
    idx3_ref[...] = jnp.concatenate([jnp.where(first, ik_hi, ik_lo),
                                     jnp.where(first, ik_hi, jnp.zeros_like(ik_hi))], axis=1)

    a = jnp.dot(hh, wa_ref[...], preferred_element_type=F32)
    cv_ref[...] = a[:, 0:kv]
    slck_ref[...] = _rope_tf(a[:, kv:2 * kv], cos, sin).astype(BF16)
    wink_ref[...] = _rope_tf(a[:, 2 * kv:3 * kv], cos, sin).astype(BF16)
    dsak_ref[...] = _rope_tf(a[:, 3 * kv:4 * kv], cos, sin).astype(BF16)
    o = 4 * kv
    snz_ref[...] = _silu(a[:, o:o + W_NQ])
    o += W_NQ
    sdz_ref[...] = _silu(a[:, o:o + W_DQ])
    o += W_DQ
    mg_ref[...] = _sigmoid(a[:, o:])

    tp = lax.dot_general(wtp_ref[...], h3, nt, preferred_element_type=F32)
    qt_ref[...] = tp[0:W_NQ]
    iqt_ref[...] = _rope_ft(tp[W_NQ:W_NQ + W_IQ], cos_t, sin_t)
    smallt_ref[...] = tp[W_NQ + W_IQ:W_NQ + W_IQ + SMALL_ROWS]

    t = lax.dot_general(wt_ref[...], hh, nt, preferred_element_type=F32)
    qrt_ref[...] = _rope_ft(t[0:W_NQ], cos_t, sin_t)
    o = W_NQ
    dqt_ref[...] = _rope_ft(t[o:o + W_DQ], cos_t, sin_t)
    o += W_DQ
    nsat_ref[0:2 * kv, :] = t[o:o + 2 * kv]
    nsat_ref[2 * kv:3 * kv, :] = _rope_ft(t[o + 2 * kv:o + 3 * kv], cos_t, sin_t)
    nsat_ref[3 * kv:4 * kv, :] = t[o + 3 * kv:o + 4 * kv]
    wint_ref[0:kv, :] = _rope_ft(t[o + 4 * kv:o + 5 * kv], cos_t, sin_t)
    wint_ref[kv:2 * kv, :] = t[o + 5 * kv:o + 6 * kv]
    dsat_ref[0:kv, :] = _rope_ft(t[o + 6 * kv:o + 7 * kv], cos_t, sin_t)
    dsat_ref[kv:2 * kv, :] = t[o + 7 * kv:o + 8 * kv]
    idxt_ref[...] = _rope_ft(t[o + 8 * kv:o + 8 * kv + IDX_DIM], cos_t, sin_t)


def _prep_inproj_weights_prompt(w_in, d_model):
    c = _cuts(d_model)
    sl = lambda name: w_in[:, c[name][0]:c[name][1]]
    nkv = sl('nkv')
    kv = NSA_KV_HEADS * HEAD_DIM
    zeros = lambda n: jnp.zeros((d_model, n), F32)
    small = jnp.concatenate([sl('iw'), sl('ng')], axis=1)
    wp = jnp.concatenate([nkv[:, 0:kv], sl('ik'), sl('ik')], axis=1)
    wa = jnp.concatenate([nkv[:, kv:2 * kv], nkv[:, 2 * kv:3 * kv], nkv[:, 4 * kv:5 * kv], sl('dkv')[:, 0:kv],
                          sl('nz'), sl('dz'), sl('mg')], axis=1)
    wtp = jnp.concatenate([sl('nq'), sl('iq'), small, zeros(SMALL_ROWS - small.shape[1])], axis=1)
    wt = jnp.concatenate([sl('nq'), sl('dq'), nkv, sl('dkv'), sl('ik')], axis=1)
    return _split3_rows(wp), wa.astype(BF16), _split3_rows(wtp).T, wt.T.astype(BF16)


def _in_projection_prompt(x2d, norm_g, scale, shift, t_len, weights, tm):
    r, d = x2d.shape
    nb = r // t_len
    tpb = t_len // tm
    wp, wa, wtp, wt = weights
    cos_tf, sin_tf, cos_t, sin_t = _rope_tables(jnp.arange(t_len, dtype=jnp.int32))
    mod_spec = pl.BlockSpec((None, 1, d), lambda i: (i // tpb, 0, 0))
    const = lambda shape: pl.BlockSpec(shape, lambda i: (0,) * len(shape), pipeline_mode=pl.Buffered(1))
    tf = lambda w: pl.BlockSpec((tm, w), lambda i: (i, 0))
    ft = lambda w: pl.BlockSpec((None, w, tm), lambda i: (i // tpb, 0, i % tpb))
    kv = NSA_KV_HEADS * HEAD_DIM
    tf_outs = (('cmp_k', kv, F32), ('idx3', 2 * LANES, BF16),
               ('cmp_v', kv, F32), ('slc_k', kv, BF16), ('win_k', kv, BF16), ('dsa_k', kv, BF16),
               ('silu_nz', W_NQ, F32), ('silu_dz', W_DQ, F32), ('merge', 2 * d, F32))
    ft_outs = (('nsa_q_t', W_NQ), ('idx_q_t', W_IQ), ('small_t', SMALL_ROWS), ('nsa_q_rot_t', W_NQ), ('dsa_q_t', W_DQ),
               ('nsa_t', 4 * kv), ('win_t', 2 * kv), ('dsa_t', W_DKV), ('idx_t', IDX_DIM))
    outs = pl.pallas_call(
        _inproj_prompt_kernel,
        grid=(r // tm,),
        in_specs=[pl.BlockSpec((tm, d), lambda i: (i, 0)),
                  const((1, d)), mod_spec, mod_spec,
                  pl.BlockSpec((tm, LANES), lambda i: (i % tpb, 0)),
                  pl.BlockSpec((tm, LANES), lambda i: (i % tpb, 0)),
                  pl.BlockSpec((HEAD_DIM // 2, tm), lambda i: (0, i % tpb)),
                  pl.BlockSpec((HEAD_DIM // 2, tm), lambda i: (0, i % tpb)),
                  const(wp.shape), const(wa.shape), const(wtp.shape), const(wt.shape)],
        out_specs=[tf(w) for _, w, _ in tf_outs] + [ft(w) for _, w in ft_outs],
        out_shape=[jax.ShapeDtypeStruct((r, w), dt) for _, w, dt in tf_outs]
        + [jax.ShapeDtypeStruct((nb, w, t_len), F32) for _, w in ft_outs],
        compiler_params=pltpu.CompilerParams(dimension_semantics=('arbitrary',), vmem_limit_bytes=VMEM_LIMIT),
        name='in_projection_prompt',
    )(x2d, norm_g.reshape(1, d), scale[:, None, :], shift[:, None, :], cos_tf, sin_tf, cos_t, sin_t,
      wp, wa, wtp, wt)
    return dict(zip([n for n, _, _ in tf_outs] + [n for n, _ in ft_outs], outs))


def _mixer_out_kernel(x_ref, on_ref, od_ref, snz_ref, sdz_ref, mg_ref, ga_ref, fg_ref,
                      wun_ref, wud_ref, wo_ref, y_ref):
    d = x_ref.shape[1]
    u_n = jnp.dot((on_ref[...] * snz_ref[...]).astype(BF16), wun_ref[...], preferred_element_type=F32)
    u_d = jnp.dot((od_ref[...] * sdz_ref[...]).astype(BF16), wud_ref[...], preferred_element_type=F32)
    mg = mg_ref[...]
    mix = mg[:, 0:d] * u_n + mg[:, d:2 * d] * u_d
    z = x_ref[...] + ga_ref[...] * jnp.dot(mix.astype(BF16), wo_ref[...], preferred_element_type=F32)
    y_ref[...] = z * lax.rsqrt(jnp.mean(z * z, axis=-1, keepdims=True) + EPS) * fg_ref[...]


def _mixer_out(x2d, o_n, o_d, f, gate, rows_per_mod, final_g, w_up_nsa, w_up_dsa, w_out, tm):
    r, d = x2d.shape
    if rows_per_mod == 1:
        mod_spec = pl.BlockSpec((tm, d), lambda i: (i, 0))
        ga = gate
    else:
        per = rows_per_mod // tm
        mod_spec = pl.BlockSpec((None, 1, d), lambda i: (i // per, 0, 0))
        ga = gate[:, None, :]
    const = lambda shape: pl.BlockSpec(shape, lambda i: (0,) * len(shape), pipeline_mode=pl.Buffered(1))
    tf = lambda w: pl.BlockSpec((tm, w), lambda i: (i, 0))
    return pl.pallas_call(
        _mixer_out_kernel,
        grid=(r // tm,),
        in_specs=[tf(d), tf(W_NQ), tf(W_DQ), tf(W_NQ), tf(W_DQ), tf(2 * d), mod_spec, const((1, d)),
                  const(w_up_nsa.shape), const(w_up_dsa.shape), const(w_out.shape)],
        out_specs=tf(d),
        out_shape=jax.ShapeDtypeStruct((r, d), F32),
        compiler_params=pltpu.CompilerParams(dimension_semantics=('arbitrary',), vmem_limit_bytes=VMEM_LIMIT),
        name='mixer_out',
    )(x2d, o_n, o_d, f['silu_nz'], f['silu_dz'], f['merge'], ga, final_g.reshape(1, d),
      w_up_nsa.astype(BF16), w_up_dsa.astype(BF16), w_out.astype(BF16))


POS_INF = float('inf')


_MAG_BITS = 0x7FFFFFFF


def _ordered_key(f):
    b = lax.bitcast_convert_type(f, jnp.int32)
    return jnp.where(b < 0, b ^ _MAG_BITS, b)


def _from_ordered_key(key):
    return lax.bitcast_convert_type(jnp.where(key < 0, key ^ _MAG_BITS, key), F32)


VALUE_MID_ROUNDS = 12


def _bisect_topk(count_ge, count_zero, v_min, v_max, n_valid, k):
    kf = float(k)

    def probe(lo, hi, cnt_lo, cnt_hi, x):
        c = count_ge(_from_ordered_key(x))
        up = c >= kf
        return (jnp.where(up, x, lo), jnp.where(up, hi, x), jnp.where(up, c, cnt_lo), jnp.where(up, cnt_hi, c))

    def key_mid(lo, hi):
        return (lo >> 1) + (hi >> 1) + (lo & hi & 1)

    def next_probe(lo, hi, it):
        guess = _ordered_key(0.5 * _from_ordered_key(lo) + 0.5 * _from_ordered_key(hi))
        use = jnp.logical_and(it < VALUE_MID_ROUNDS, jnp.logical_and(guess > lo, guess < hi))
        return jnp.where(use, guess, key_mid(lo, hi))

    def cond(st):
        _, _, _, _, done, it = st
        return jnp.logical_and(it < 48, jnp.min(done) < 0.5)

    def body(st):
        lo, hi, cnt_lo, cnt_hi, _, it = st
        lo, hi, cnt_lo, cnt_hi = probe(lo, hi, cnt_lo, cnt_hi, next_probe(lo, hi, it))
        lo, hi, cnt_lo, cnt_hi = probe(lo, hi, cnt_lo, cnt_hi, next_probe(lo, hi, it))
        done = jnp.where(jnp.logical_or(cnt_lo <= kf, key_mid(lo, hi) == lo), 1.0, 0.0)
        return lo, hi, cnt_lo, cnt_hi, done, it + 1

    hi0 = _ordered_key(v_max)
    lo, hi, cnt_lo, cnt_hi = probe(_ordered_key(v_min), hi0, n_valid, jnp.zeros_like(n_valid), hi0)
    is_open = key_mid(lo, hi) != lo
    c_ge0, c_gt0 = count_zero()
    zero_tie = jnp.logical_and(is_open, jnp.logical_and(c_gt0 < kf, c_ge0 >= kf))
    raise_lo = jnp.logical_and(is_open, jnp.logical_and(c_gt0 >= kf, lo < 0))
    lower_hi = jnp.logical_and(is_open, jnp.logical_and(c_ge0 < kf, hi > 0))
    to_zero_lo = jnp.logical_or(zero_tie, raise_lo)
    lo = jnp.where(to_zero_lo, 0, lo)
    cnt_lo = jnp.where(to_zero_lo, c_ge0, cnt_lo)
    hi = jnp.where(jnp.logical_or(zero_tie, lower_hi), 0, hi)
    cnt_hi = jnp.where(lower_hi, c_ge0, cnt_hi)
    done0 = jnp.where(jnp.logical_or(n_valid <= kf, jnp.logical_or(cnt_lo <= kf, key_mid(lo, hi) == lo)),
                      1.0, 0.0)
    lo, _, cnt_lo, _, _, _ = lax.while_loop(cond, body, (lo, hi, cnt_lo, cnt_hi, done0, jnp.int32(0)))
    return jnp.where(n_valid <= kf, NEG, _from_ordered_key(lo)), cnt_lo


def _topk_cap_rows(s, row_min, row_max, n_valid, k):
    rows, n = s.shape
    kf = float(k)
    count_ge = lambda x: jnp.sum(jnp.where(s >= x, 1.0, 0.0), axis=1, keepdims=True)
    count_zero = lambda: (jnp.sum(jnp.where(s >= 0.0, 1.0, 0.0), axis=1, keepdims=True),
                          jnp.sum(jnp.where(s > 0.0, 1.0, 0.0), axis=1, keepdims=True))
    t, _ = _bisect_topk(count_ge, count_zero, row_min, row_max, n_valid, k)
    eq = jnp.where(s == t, 1.0, 0.0)
    need = kf - jnp.sum(jnp.where(s > t, 1.0, 0.0), axis=1, keepdims=True)
    n_grp = n // LANES
    r_i = lax.broadcasted_iota(jnp.int32, (LANES, LANES), 0)
    c_i = lax.broadcasted_iota(jnp.int32, (LANES, LANES), 1)
    upper = jnp.where(r_i < c_i, 1.0, 0.0).astype(BF16)
    stacked = jnp.concatenate([eq[:, j * LANES:(j + 1) * LANES] for j in range(n_grp)], axis=0)
    within = jnp.dot(stacked.astype(BF16), upper, preferred_element_type=F32)
    run = jnp.zeros((rows, 1), F32)
    caps = []
    for j in range(n_grp):
        sl = slice(j * LANES, (j + 1) * LANES)
        before = run + within[j * rows:(j + 1) * rows]
        sj = s[:, sl]
        keep = jnp.logical_or(sj > t, jnp.logical_and(sj == t, before < need))
        caps.append(jnp.where(keep, POS_INF, NEG))
        run = run + jnp.sum(eq[:, sl], axis=1, keepdims=True)
    return jnp.concatenate(caps, axis=1)


ACC_ROWS = 64
TIE_BLOCK = 256


def _col_reduce(x, op):
    n, w = x.shape
    if n > ACC_ROWS and n % ACC_ROWS == 0:
        x = op(x.reshape(n // ACC_ROWS, ACC_ROWS, w), axis=0)
    return op(x, axis=0, keepdims=True)


def _count_ge_t(st_scr, x, nk, kc):
    tq = st_scr.shape[1]

    def body(c, acc):
        k0 = pl.multiple_of(c * kc, kc)
        hit = jnp.where(st_scr[pl.ds(k0, kc), :] >= x, 1.0, 0.0)
        return acc + jnp.sum(hit.reshape(kc // ACC_ROWS, ACC_ROWS, tq), axis=0)

    acc = lax.fori_loop(0, nk, body, jnp.zeros((ACC_ROWS, tq), F32))
    return jnp.sum(acc, axis=0, keepdims=True)


def _topk_threshold_t(st_scr, col_min, col_max, n_valid, k, nk, kc):
    tq = st_scr.shape[1]

    def count_zero():
        def body(c, carry):
            ge, gt = carry
            s = st_scr[pl.ds(pl.multiple_of(c * kc, kc), kc), :]
            ge = ge + jnp.sum(jnp.where(s >= 0.0, 1.0, 0.0).reshape(kc // ACC_ROWS, ACC_ROWS, tq), axis=0)
            gt = gt + jnp.sum(jnp.where(s > 0.0, 1.0, 0.0).reshape(kc // ACC_ROWS, ACC_ROWS, tq), axis=0)
            return ge, gt
        z = jnp.zeros((ACC_ROWS, tq), F32)
        ge, gt = lax.fori_loop(0, nk, body, (z, z))
        return jnp.sum(ge, axis=0, keepdims=True), jnp.sum(gt, axis=0, keepdims=True)

    return _bisect_topk(lambda x: _count_ge_t(st_scr, x, nk, kc), count_zero, col_min, col_max, n_valid, k)


def _write_topk_cap_t(st_scr, t, cnt_ge, k, nk, kc):
    kf = float(k)
    tq = st_scr.shape[1]
    blk = min(TIE_BLOCK, kc)
    has_tie = jnp.max(cnt_ge) > kf

    @pl.when(jnp.logical_not(has_tie))
    def _():
        def body(c, _):
            sl = pl.ds(pl.multiple_of(c * kc, kc), kc)
            st_scr[sl, :] = jnp.where(st_scr[sl, :] >= t, POS_INF, NEG)
            return 0
        lax.fori_loop(0, nk, body, 0)

    @pl.when(has_tie)
    def _():
        def cnt_body(c, acc):
            sl = pl.ds(pl.multiple_of(c * kc, kc), kc)
            hit = jnp.where(st_scr[sl, :] > t, 1.0, 0.0)
            return acc + jnp.sum(hit.reshape(kc // ACC_ROWS, ACC_ROWS, tq), axis=0)
        cnt_gt = jnp.sum(lax.fori_loop(0, nk, cnt_body, jnp.zeros((ACC_ROWS, tq), F32)), axis=0, keepdims=True)
        need = kf - cnt_gt
        r_i = lax.broadcasted_iota(jnp.int32, (blk, blk), 0)
        c_i = lax.broadcasted_iota(jnp.int32, (blk, blk), 1)
        lower = jnp.where(c_i < r_i, 1.0, 0.0).astype(BF16)

        def body(c, run):
            k0 = pl.multiple_of(c * kc, kc)
            for b in range(kc // blk):
                sl = pl.ds(k0 + b * blk, blk)
                s = st_scr[sl, :]
                eq = jnp.where(s == t, 1.0, 0.0)
                before = run + jnp.dot(lower, eq.astype(BF16), preferred_element_type=F32)
                keep = jnp.logical_or(s > t, jnp.logical_and(s == t, before < need))
                st_scr[sl, :] = jnp.where(keep, POS_INF, NEG)
                run = run + _col_reduce(eq, jnp.sum)
            return run
        lax.fori_loop(0, nk, body, jnp.zeros_like(t))


LOG2_E = float(np.log2(np.e))


def _q_pad(qt, g, hg, scale):
    qg = jnp.concatenate([qt[(g * hg + j) * HEAD_DIM:(g * hg + j + 1) * HEAD_DIM, :] for j in range(hg)],
                         axis=1) * (scale * LOG2_E)
    z = jnp.zeros_like(qg)
    parts = [z] * NSA_KV_HEADS
    parts[g] = qg
    return jnp.concatenate(parts, axis=0).astype(BF16)


def _flash_t(q_pads, k_ref, vt_ref, vrow0s, cap_fns, k_base, n_chunks, kc, n_rep):
    n = q_pads[0].shape[1]
    n_g = len(q_pads)
    ones = jnp.ones((8, kc), BF16)

    def body(c, carry):
        k0 = pl.multiple_of(k_base + c * kc, LANES)
        k_chunk = k_ref[pl.ds(k0, kc), :]
        out = []
        for g in range(n_g):
            m, acc = carry[g]
            s = jnp.dot(k_chunk, q_pads[g], preferred_element_type=F32)
            s = jnp.minimum(s, jnp.concatenate([cap_fns[g](k0)] * n_rep, axis=1))
            m_new = jnp.maximum(m, _col_reduce(s, jnp.max))
            p = jnp.exp2((s - m_new).astype(BF16))
            vt = jnp.concatenate([vt_ref[pl.ds(vrow0s[g], HEAD_DIM), pl.ds(k0, kc)].astype(BF16), ones], axis=0)
            acc = jnp.exp2(m - m_new) * acc + jnp.dot(vt, p, preferred_element_type=F32)
            out.append((m_new, acc))
        return tuple(out)

    init = tuple((jnp.full((1, n), NEG, F32), jnp.zeros((HEAD_DIM + 8, n), F32)) for _ in range(n_g))
    fin = lax.fori_loop(0, n_chunks, body, init)
    return [acc[0:HEAD_DIM] / jnp.maximum(acc[HEAD_DIM:HEAD_DIM + 1], 1e-30) for _, acc in fin]


def _store_heads(o_ref, o_t, g, hg, tq):
    for j in range(hg):
        h = g * hg + j
        o_ref[:, h * HEAD_DIM:(h + 1) * HEAD_DIM] = o_t[:, j * tq:(j + 1) * tq].T


def _dsa_prompt_kernel(iqt_ref, smallt_ref, dqt_ref, idx3_ref, k_ref, kvt_ref, o_ref, st_scr,
                       *, tq, kc, topk):
    qs = pl.program_id(1) * tq
    nk = (qs + tq + kc - 1) // kc
    qpos = qs + lax.broadcasted_iota(jnp.int32, (1, tq), 1)

    iqt = iqt_ref[...]
    q3 = []
    for h in range(IDX_HEADS):
        qh = iqt[h * IDX_DIM:(h + 1) * IDX_DIM, :]
        q_hi = qh.astype(BF16)
        q_lo = (qh - q_hi.astype(F32)).astype(BF16)
        q3.append(jnp.concatenate([q_hi, q_hi, q_lo, jnp.zeros_like(q_hi)], axis=0))
    q3_pairs = [jnp.concatenate(q3[h:h + 2], axis=1) for h in range(0, IDX_HEADS, 2)]
    wt = smallt_ref[...]
    out_scale = (IDX_DIM ** -0.5) * (IDX_HEADS ** -0.5)

    def score_body(c, carry):
        mn, mx = carry
        k0 = pl.multiple_of(c * kc, kc)
        k3 = idx3_ref[pl.ds(k0, kc), :]
        sc = jnp.zeros((kc, tq), F32)
        for hp, q3_pair in enumerate(q3_pairs):
            d = jnp.dot(k3, q3_pair, preferred_element_type=F32)
            for j in range(2):
                h = 2 * hp + j
                sc = sc + jnp.maximum(d[:, j * tq:(j + 1) * tq], 0.0) * wt[h:h + 1, :]
        sc = sc * out_scale
        ok = (k0 + lax.broadcasted_iota(jnp.int32, (kc, 1), 0)) <= qpos
        st_scr[pl.ds(k0, kc), :] = jnp.where(ok, sc, -POS_INF)
        mn = jnp.minimum(mn, _col_reduce(jnp.where(ok, sc, POS_INF), jnp.min))
        mx = jnp.maximum(mx, _col_reduce(jnp.where(ok, sc, NEG), jnp.max))
        return mn, mx

    mn, mx = lax.fori_loop(0, nk, score_body,
                           (jnp.full((1, tq), POS_INF, F32), jnp.full((1, tq), NEG, F32)))
    t, cnt = _topk_threshold_t(st_scr, mn, mx, (qpos + 1).astype(F32), topk, nk, kc)
    _write_topk_cap_t(st_scr, t, cnt, topk, nk, kc)

    hg = DSA_HEADS // DSA_KV_HEADS
    dqt = dqt_ref[...]
    groups = range(DSA_KV_HEADS)
    cap_fn = lambda k0: st_scr[pl.ds(k0, kc), :]
    o_ts = _flash_t([_q_pad(dqt, g, hg, HEAD_DIM ** -0.5) for g in groups], k_ref, kvt_ref,
                    [(DSA_KV_HEADS + g) * HEAD_DIM for g in groups], [cap_fn] * DSA_KV_HEADS, 0, nk, kc, hg)
    for g in groups:
        _store_heads(o_ref, o_ts[g], g, hg, tq)


def _dsa_prompt(f, bx, t_len, tq, kc):
    topk = min(DSA_TOPK_MAX, t_len // 4)
    nq = t_len // tq
    hg = DSA_HEADS // DSA_KV_HEADS
    row = lambda w: pl.BlockSpec((tq, w), lambda b, i: (b * nq + i, 0))
    qt = lambda rows: pl.BlockSpec((None, rows, tq), lambda b, i: (b, 0, i))
    keys = lambda w: pl.BlockSpec((t_len, w), lambda b, i: (b, 0))
    per_b = lambda rows: pl.BlockSpec((None, rows, t_len), lambda b, i: (b, 0, 0))
    return pl.pallas_call(
        functools.partial(_dsa_prompt_kernel, tq=tq, kc=kc, topk=topk),
        grid=(bx, nq),
        in_specs=[qt(W_IQ), qt(SMALL_ROWS), qt(W_DQ), keys(2 * LANES), keys(DSA_KV_HEADS * HEAD_DIM),
                  per_b(W_DKV)],
        out_specs=row(W_DQ),
        out_shape=jax.ShapeDtypeStruct((bx * t_len, W_DQ), F32),
        scratch_shapes=[pltpu.VMEM((t_len, tq), F32)],
        compiler_params=pltpu.CompilerParams(dimension_semantics=('arbitrary', 'arbitrary'),
                                             vmem_limit_bytes=VMEM_LIMIT),
        name='dsa_prompt',
    )(f['idx_q_t'], f['small_t'], f['dsa_q_t'], f['idx3'], f['dsa_k'], f['dsa_t'])


GROUP_TOKENS = CMP_STRIDE
N_HALVES = CMP_BLOCK // CMP_STRIDE
CMP_HID = 2 * HEAD_DIM


def _prep_compress_weights(cmp_pe, cmp_w1, cmp_w2):
    g_n = NSA_KV_HEADS
    eye = jnp.eye(g_n, dtype=F32)
    out = []
    for s in range(2):
        w1 = cmp_w1[s].reshape(N_HALVES, GROUP_TOKENS, HEAD_DIM, CMP_HID)
        w1e = jnp.einsum('ardj,gh->arghdj', w1, eye)
        w1e = w1e.transpose(0, 1, 2, 4, 3, 5).reshape(N_HALVES, GROUP_TOKENS * g_n * HEAD_DIM, g_n * CMP_HID)
        pe = cmp_pe[s].reshape(N_HALVES, GROUP_TOKENS, 1, HEAD_DIM)
        pe = jnp.broadcast_to(pe, (N_HALVES, GROUP_TOKENS, g_n, HEAD_DIM)).reshape(N_HALVES, 1, -1)
        w2e = jnp.einsum('jd,gh->gjhd', cmp_w2[s], eye).reshape(g_n * CMP_HID, g_n * HEAD_DIM)
        out.append((w1e, pe, w2e))
    (w1k, pek, w2k), (w1v, pev, w2v) = out
    w1k3 = jnp.stack([_split3_rows(w1k[a]) for a in range(N_HALVES)])
    return (w1k3, pek, _split3_rows(w2k), w1v.astype(BF16), pev, w2v.astype(BF16))


def _compress_slot(x_ref, m, pe_ref, w1_ref, w2_ref, shift_scr, precise):
    xr = jnp.concatenate([x_ref[pl.ds(r, m, stride=GROUP_TOKENS), :] for r in range(GROUP_TOKENS)], axis=1)
    halves = []
    for a in range(N_HALVES):
        xa = xr + pe_ref[a]
        lhs = _split3_cols(xa) if precise else xa.astype(BF16)
        halves.append(jnp.dot(lhs, w1_ref[a], preferred_element_type=F32))
    shift_scr[0:m, :] = halves[1]
    shift_scr[m:m + 8, :] = jnp.zeros((8, shift_scr.shape[1]), F32)
    hid = _silu(halves[0] + shift_scr[pl.ds(1, m), :])
    lhs = _split3_cols(hid) if precise else hid.astype(BF16)
    return jnp.dot(lhs, w2_ref[...], preferred_element_type=F32)


def _compress_kernel(xk_ref, xv_ref, w1k_ref, pek_ref, w2k_ref, w1v_ref, pev_ref, w2v_ref,
                     ck_ref, cv_ref, shift_scr, *, m):
    n_pad = ck_ref.shape[0]
    if n_pad > m:
        ck_ref[m:n_pad, :] = jnp.zeros((n_pad - m, ck_ref.shape[1]), F32)
        cv_ref[m:n_pad, :] = jnp.zeros((n_pad - m, cv_ref.shape[1]), F32)
    ck_ref[0:m, :] = _compress_slot(xk_ref, m, pek_ref, w1k_ref, w2k_ref, shift_scr, True)
    cv_ref[0:m, :] = _compress_slot(xv_ref, m, pev_ref, w1v_ref, w2v_ref, shift_scr, False)


def _compress_prompt(cmp_k, cmp_v, cw, bx, t_len):
    m = t_len // GROUP_TOKENS
    n_pad = -(-m // LANES) * LANES
    kvw = NSA_KV_HEADS * HEAD_DIM
    const = lambda a: pl.BlockSpec(a.shape, lambda b: (0,) * a.ndim, pipeline_mode=pl.Buffered(1))
    x_spec = pl.BlockSpec((t_len, kvw), lambda b: (b, 0))
    o_spec = pl.BlockSpec((None, n_pad, kvw), lambda b: (b, 0, 0))
    return pl.pallas_call(
        functools.partial(_compress_kernel, m=m),
        grid=(bx,),
        in_specs=[x_spec, x_spec] + [const(a) for a in cw],
        out_specs=[o_spec, o_spec],
        out_shape=[jax.ShapeDtypeStruct((bx, n_pad, kvw), F32)] * 2,
        scratch_shapes=[pltpu.VMEM((m + 8, NSA_KV_HEADS * CMP_HID), F32)],
        compiler_params=pltpu.CompilerParams(dimension_semantics=('arbitrary',), vmem_limit_bytes=VMEM_LIMIT),
        name='compress_prompt',
    )(cmp_k, cmp_v, *cw)


SEL_SHIFT = 6


def _stack_heads(x, g, hg, scale):
    return jnp.concatenate(
        [x[:, (g * hg + j) * HEAD_DIM:(g * hg + j + 1) * HEAD_DIM] for j in range(hg)], axis=0) * scale


def _cmp_branch(q_rows, ck_g, cv_g, qpos_rep, hg, tq):
    n_pad = ck_g.shape[0]
    nt = (((1,), (1,)), ((), ()))
    ck_hi = ck_g.astype(BF16)
    ck_lo = (ck_g - ck_hi.astype(F32)).astype(BF16)
    ck3 = jnp.concatenate([ck_hi, ck_lo, ck_hi], axis=1)
    s = lax.dot_general(_split3_cols(q_rows), ck3, nt, preferred_element_type=F32)
    cmp_end = lax.broadcasted_iota(jnp.int32, (1, n_pad), 1) * CMP_STRIDE + (CMP_BLOCK - 1)
    vis = cmp_end <= qpos_rep
    s = jnp.where(vis, s, NEG)
    p = jnp.where(vis, jnp.exp(s - jnp.max(s, axis=1, keepdims=True)), 0.0)
    p = p / jnp.maximum(jnp.sum(p, axis=1, keepdims=True), 1e-30)
    o_cmp = jnp.dot(p.astype(BF16), cv_g.astype(BF16), preferred_element_type=F32)
    p_sum = p[0:tq]
    for j in range(1, hg):
        p_sum = p_sum + p[j * tq:(j + 1) * tq]
    return o_cmp, p_sum


def _select_blocks(p_sum, qpos, n_slc, n_sel):
    tq, n_pad = p_sum.shape
    n_i = lax.broadcasted_iota(jnp.int32, (n_pad, LANES), 0) * CMP_STRIDE
    j_i = lax.broadcasted_iota(jnp.int32, (n_pad, LANES), 1)
    overlap = jnp.logical_and(n_i < (j_i + 1) * SEL_BLOCK, n_i + CMP_BLOCK > j_i * SEL_BLOCK)
    overlap = jnp.where(jnp.logical_and(overlap, j_i < n_slc), 1.0, 0.0).astype(BF16)
    p_hi = p_sum.astype(BF16)
    r1 = p_sum - p_hi.astype(F32)
    p_mid = r1.astype(BF16)
    p_lo = (r1 - p_mid.astype(F32)).astype(BF16)
    imp = (jnp.dot(p_hi, overlap, preferred_element_type=F32)
           + jnp.dot(p_mid, overlap, preferred_element_type=F32)
           + jnp.dot(p_lo, overlap, preferred_element_type=F32))
    jb = lax.broadcasted_iota(jnp.int32, (1, LANES), 1)
    cur = qpos >> SEL_SHIFT
    forced = jnp.logical_or(jb == 0, jnp.logical_or(jb == cur, jb == cur - 1))
    imp = jnp.where(forced, FORCED_SCORE, imp)
    imp = jnp.where(jb > cur, NEG, imp)
    rank = jnp.zeros((tq, LANES), F32)
    for i in range(n_slc):
        vi = imp[:, i:i + 1]
        ahead = jnp.logical_or(vi > imp, jnp.logical_and(vi == imp, jb > i))
        rank = rank + jnp.where(ahead, 1.0, 0.0)
    return jnp.where(jnp.logical_and(rank < float(n_sel), imp > 0.5 * NEG), 1.0, 0.0)


def _cmp_branch_t(qt, ck_g, cvt_g, qpos_t, g, hg, scale):
    n_pad = ck_g.shape[0]
    tq = qt.shape[1]
    q3 = []
    for j in range(hg):
        qh = qt[(g * hg + j) * HEAD_DIM:(g * hg + j + 1) * HEAD_DIM, :] * scale
        q_hi = qh.astype(BF16)
        q_lo = (qh - q_hi.astype(F32)).astype(BF16)
        q3.append(jnp.concatenate([q_hi, q_hi, q_lo], axis=0))
    ck_hi = ck_g.astype(BF16)
    ck_lo = (ck_g - ck_hi.astype(F32)).astype(BF16)
    ck3 = jnp.concatenate([ck_hi, ck_lo, ck_hi], axis=1)
    s = jnp.dot(ck3, jnp.concatenate(q3, axis=1), preferred_element_type=F32)
    cmp_end = lax.broadcasted_iota(jnp.int32, (n_pad, 1), 0) * CMP_STRIDE + (CMP_BLOCK - 1)
    vis = jnp.concatenate([cmp_end <= qpos_t] * hg, axis=1)
    s = jnp.where(vis, s, NEG)
    p = jnp.where(vis, jnp.exp(s - _col_reduce(s, jnp.max)), 0.0)
    p = p / jnp.maximum(_col_reduce(p, jnp.sum), 1e-30)
    o_t = jnp.dot(cvt_g.astype(BF16), p.astype(BF16), preferred_element_type=F32)
    p_sum = p[:, 0:tq]
    for j in range(1, hg):
        p_sum = p_sum + p[:, j * tq:(j + 1) * tq]
    return o_t, p_sum


def _select_blocks_t(p_t, qpos_t, n_slc, n_sel):
    n_pad, tq = p_t.shape
    nb = -(-n_slc // 8) * 8
    j_i = lax.broadcasted_iota(jnp.int32, (nb, n_pad), 0)
    n_i = lax.broadcasted_iota(jnp.int32, (nb, n_pad), 1) * CMP_STRIDE
    overlap = jnp.logical_and(n_i < (j_i + 1) * SEL_BLOCK, n_i + CMP_BLOCK > j_i * SEL_BLOCK)
    overlap = jnp.where(jnp.logical_and(overlap, j_i < n_slc), 1.0, 0.0).astype(BF16)
    p_hi = p_t.astype(BF16)
    r1 = p_t - p_hi.astype(F32)
    p_mid = r1.astype(BF16)
    p_lo = (r1 - p_mid.astype(F32)).astype(BF16)
    imp = (jnp.dot(overlap, p_hi, preferred_element_type=F32)
           + jnp.dot(overlap, p_mid, preferred_element_type=F32)
           + jnp.dot(overlap, p_lo, preferred_element_type=F32))
    jb = lax.broadcasted_iota(jnp.int32, (nb, 1), 0)
    cur = qpos_t >> SEL_SHIFT
    forced = jnp.logical_or(jb == 0, jnp.logical_or(jb == cur, jb == cur - 1))
    imp = jnp.where(forced, FORCED_SCORE, imp)
    imp = jnp.where(jnp.logical_or(jb > cur, jb >= n_slc), NEG, imp)
    rank = jnp.zeros((nb, tq), F32)
    for i in range(n_slc):
        vi = imp[i:i + 1, :]
        ahead = jnp.logical_or(vi > imp, jnp.logical_and(vi == imp, jb > i))
        rank = rank + jnp.where(ahead, 1.0, 0.0)
    return jnp.where(jnp.logical_and(rank < float(n_sel), imp > 0.5 * NEG), 1.0, 0.0)


def _nsa_prompt_kernel(qt_ref, qrt_ref, smallt_ref, ck_ref, cv_ref, slck_ref, wink_ref, slct_ref, wint_ref,
                       o_ref, capt_scr, *, tq, kc, kcw, n_slc, n_sel):
    qs = pl.program_id(1) * tq
    nk = (qs + tq + kc - 1) // kc
    hg = NSA_HEADS // NSA_KV_HEADS
    kvw = NSA_KV_HEADS * HEAD_DIM
    scale = HEAD_DIM ** -0.5
    qpos_t = qs + lax.broadcasted_iota(jnp.int32, (1, tq), 1)
    qt = qt_ref[...]
    qrt = qrt_ref[...]
    gates_t = _sigmoid(smallt_ref[...])
    ck = ck_ref[...]
    cv_t = cv_ref[...].T
    nb = -(-n_slc // 8) * 8
    blk_l = lax.broadcasted_iota(jnp.int32, (1, nb), 1)

    def win_cap(k0):
        dist = qpos_t - (k0 + lax.broadcasted_iota(jnp.int32, (kcw, 1), 0))
        return jnp.where(jnp.logical_and(dist >= 0, dist < WINDOW), POS_INF, NEG)

    win_base = jnp.maximum(qs + tq - kcw, 0)

    groups = range(NSA_KV_HEADS)
    o_cmps, sel_ts = [], []
    for g in groups:
        o_cmp, p_sum = _cmp_branch_t(qt, ck[:, g * HEAD_DIM:(g + 1) * HEAD_DIM],
                                     cv_t[g * HEAD_DIM:(g + 1) * HEAD_DIM, :], qpos_t, g, hg, scale)
        o_cmps.append(o_cmp)
        sel_ts.append(_select_blocks_t(p_sum, qpos_t, n_slc, n_sel).astype(BF16))

    def cap_body(c, _):
        k0 = pl.multiple_of(c * kc, kc)
        kpos = k0 + lax.broadcasted_iota(jnp.int32, (kc, 1), 0)
        expand = jnp.where((kpos >> SEL_SHIFT) == blk_l, 1.0, 0.0).astype(BF16)
        causal = kpos <= qpos_t
        for g in groups:
            sel_k = jnp.dot(expand, sel_ts[g], preferred_element_type=F32)
            capt_scr[g, pl.ds(k0, kc), :] = jnp.where(jnp.logical_and(sel_k > 0.5, causal), POS_INF, NEG)
        return 0

    lax.fori_loop(0, nk, cap_body, 0)
    q_pads = [_q_pad(qrt, g, hg, scale) for g in groups]
    vrows = [kvw + g * HEAD_DIM for g in groups]
    o_slcs = _flash_t(q_pads, slck_ref, slct_ref, vrows,
                      [functools.partial(lambda g, k0: capt_scr[g, pl.ds(k0, kc), :], g) for g in groups],
                      0, nk, kc, hg)
    o_wins = _flash_t(q_pads, wink_ref, wint_ref, vrows, [win_cap] * NSA_KV_HEADS, win_base, 1, kcw, hg)
    for g in groups:
        for j in range(hg):
            h = g * hg + j
            r = IDX_HEADS + 3 * h
            cols = slice(j * tq, (j + 1) * tq)
            o_t = (gates_t[r:r + 1, :] * o_cmps[g][:, cols] + gates_t[r + 1:r + 2, :] * o_slcs[g][:, cols]
                   + gates_t[r + 2:r + 3, :] * o_wins[g][:, cols])
            o_ref[:, h * HEAD_DIM:(h + 1) * HEAD_DIM] = o_t.T


def _nsa_prompt(f, ck, cv, bx, t_len, tq, kc, kcw):
    nq = t_len // tq
    hg = NSA_HEADS // NSA_KV_HEADS
    n_pad = ck.shape[1]
    n_slc = -(-t_len // SEL_BLOCK)
    kvw = NSA_KV_HEADS * HEAD_DIM
    row = lambda w: pl.BlockSpec((tq, w), lambda b, i: (b * nq + i, 0))
    cmp_spec = pl.BlockSpec((None, n_pad, kvw), lambda b, i: (b, 0, 0))
    keys = pl.BlockSpec((t_len, kvw), lambda b, i: (b, 0))
    return pl.pallas_call(
        functools.partial(_nsa_prompt_kernel, tq=tq, kc=kc, kcw=kcw, n_slc=n_slc, n_sel=min(N_SEL, n_slc)),
        grid=(bx, nq),
        in_specs=[pl.BlockSpec((None, W_NQ, tq), lambda b, i: (b, 0, i)),
                  pl.BlockSpec((None, W_NQ, tq), lambda b, i: (b, 0, i)),
                  pl.BlockSpec((None, SMALL_ROWS, tq), lambda b, i: (b, 0, i)),
                  cmp_spec, cmp_spec, keys, keys,
                  pl.BlockSpec((None, 2 * kvw, t_len), lambda b, i: (b, 1, 0)),
                  pl.BlockSpec((None, 2 * kvw, t_len), lambda b, i: (b, 0, 0))],
        out_specs=row(W_NQ),
        out_shape=jax.ShapeDtypeStruct((bx * t_len, W_NQ), F32),
        scratch_shapes=[pltpu.VMEM((NSA_KV_HEADS, t_len, tq), F32)],
        compiler_params=pltpu.CompilerParams(dimension_semantics=('arbitrary', 'arbitrary'),
                                             vmem_limit_bytes=VMEM_LIMIT),
        name='nsa_prompt',
    )(f['nsa_q_t'], f['nsa_q_rot_t'], f['small_t'], ck, cv, f['slc_k'], f['win_k'], f['nsa_t'], f['win_t'])


DEC_ROWS = 8


def _page_copies(cache_hbm, pt_ref, b, n_pages, page, buf, sem, slot):
    return [pltpu.make_async_copy(cache_hbm.at[pt_ref[b * n_pages + j]],
                                  buf.at[slot, :, pl.ds(j * page, page)], sem.at[slot])
            for j in range(n_pages)]


def _gather_step(caches, pt_ref, n_pages, page, bufs, sems):
    b = pl.program_id(0)
    nb = pl.num_programs(0)
    slot = b % 2

    @pl.when(b == 0)
    def _():
        for cache, buf, sem in zip(caches, bufs, sems):
            for cp in _page_copies(cache, pt_ref, 0, n_pages, page, buf, sem, 0):
                cp.start()

    @pl.when(b + 1 < nb)
    def _():
        for cache, buf, sem in zip(caches, bufs, sems):
            for cp in _page_copies(cache, pt_ref, b + 1, n_pages, page, buf, sem, 1 - slot):
                cp.start()

    for cache, buf, sem in zip(caches, bufs, sems):
        for cp in _page_copies(cache, pt_ref, b, n_pages, page, buf, sem, slot):
            cp.wait()
    return slot


def _attend_full(q_rows, kt, vt, cap, n_rep):
    nt = (((1,), (1,)), ((), ()))
    s = jnp.dot(q_rows, kt.astype(BF16), preferred_element_type=F32)
    s = jnp.minimum(s, jnp.concatenate([cap] * n_rep, axis=0))
    p = jnp.exp(s - jnp.max(s, axis=1, keepdims=True))
    den = jnp.maximum(jnp.sum(p, axis=1, keepdims=True), 1e-30)
    return lax.dot_general(p.astype(BF16), vt.astype(BF16), nt, preferred_element_type=F32) / den


def _decode_nsa_kernel(pt_ref, q_ref, qr_ref, small_ref, cache_hbm, new_ref, wst_ref, wnew_ref,
                       w1k_ref, pek_ref, w2k_ref, w1v_ref, pev_ref, w2v_ref, o_ref,
                       buf, sem, xk_scr, xv_scr, shift_scr, *, past, t_dec, n_pages, page):
    slot = _gather_step([cache_hbm], pt_ref, n_pages, page, [buf], [sem])
    rows = DEC_ROWS
    hg = NSA_HEADS // NSA_KV_HEADS
    kvw = NSA_KV_HEADS * HEAD_DIM
    scale = HEAD_DIM ** -0.5
    n_keys = buf.shape[2]
    buf[slot, :, pl.ds(past, LANES)] = new_ref[...]
    qpos = past + jnp.minimum(lax.broadcasted_iota(jnp.int32, (rows, 1), 0), t_dec - 1)
    qpos_rep = jnp.concatenate([qpos] * hg, axis=0)

    for j in range(n_pages):
        xk_scr[j * page:(j + 1) * page, :] = buf[slot, 0:kvw, j * page:(j + 1) * page].T
        xv_scr[j * page:(j + 1) * page, :] = buf[slot, kvw:2 * kvw, j * page:(j + 1) * page].T
    m = past // GROUP_TOKENS
    ck = _compress_slot(xk_scr, m, pek_ref, w1k_ref, w2k_ref, shift_scr, True)
    cv = _compress_slot(xv_scr, m, pev_ref, w1v_ref, w2v_ref, shift_scr, False)

    q = q_ref[...]
    qr = qr_ref[...]
    gates = _sigmoid(small_ref[...])
    n_slc = -(-(past + t_dec) // SEL_BLOCK)
    kpos = lax.broadcasted_iota(jnp.int32, (1, n_keys), 1)
    blk_i = lax.broadcasted_iota(jnp.int32, (LANES, 1), 0)
    expand = jnp.where((kpos >> SEL_SHIFT) == blk_i, 1.0, 0.0).astype(BF16)
    wb = wst_ref.shape[1]
    wpos = past - wb + lax.broadcasted_iota(jnp.int32, (1, wb + LANES), 1)
    wdist = qpos - wpos
    win_cap = jnp.where(jnp.logical_and(wdist >= 0, wdist < WINDOW), POS_INF, NEG)

    for g in range(NSA_KV_HEADS):
        gs = slice(g * HEAD_DIM, (g + 1) * HEAD_DIM)
        o_cmp, p_sum = _cmp_branch(_stack_heads(q, g, hg, scale), ck[:, gs], cv[:, gs], qpos_rep, hg, rows)
        sel = _select_blocks(p_sum, qpos, n_slc, min(N_SEL, n_slc)).astype(BF16)
        sel_k = jnp.dot(sel, expand, preferred_element_type=F32)
        slc_cap = jnp.where(jnp.logical_and(sel_k > 0.5, kpos <= qpos), POS_INF, NEG)
        qr_rows = _stack_heads(qr, g, hg, scale).astype(BF16)
        o_slc = _attend_full(qr_rows, buf[slot, 2 * kvw + g * HEAD_DIM:2 * kvw + (g + 1) * HEAD_DIM, :],
                             buf[slot, 3 * kvw + g * HEAD_DIM:3 * kvw + (g + 1) * HEAD_DIM, :], slc_cap, hg)
        wk = jnp.concatenate([wst_ref[g * HEAD_DIM:(g + 1) * HEAD_DIM, :],
                              wnew_ref[g * HEAD_DIM:(g + 1) * HEAD_DIM, :]], axis=1)
        wv = jnp.concatenate([wst_ref[kvw + g * HEAD_DIM:kvw + (g + 1) * HEAD_DIM, :],
                              wnew_ref[kvw + g * HEAD_DIM:kvw + (g + 1) * HEAD_DIM, :]], axis=1)
        o_win = _attend_full(qr_rows, wk, wv, win_cap, hg)
        for j in range(hg):
            h = g * hg + j
            lane = IDX_HEADS + 3 * h
            rs = slice(j * rows, (j + 1) * rows)
            o_ref[:, h * HEAD_DIM:(h + 1) * HEAD_DIM] = (gates[:, lane:lane + 1] * o_cmp[rs]
                                                         + gates[:, lane + 1:lane + 2] * o_slc[rs]
                                                         + gates[:, lane + 2:lane + 3] * o_win[rs])


def _decode_dsa_kernel(pt_ref, iq_ref, small_ref, dq_ref, idx_hbm, kv_hbm, inew_ref, kvnew_ref, o_ref,
                       ibuf, kbuf, isem, ksem, *, past, t_dec, n_pages, page, topk):
    slot = _gather_step([idx_hbm, kv_hbm], pt_ref, n_pages, page, [ibuf, kbuf], [isem, ksem])
    rows = DEC_ROWS
    hg = DSA_HEADS // DSA_KV_HEADS
    scale = HEAD_DIM ** -0.5
    n_keys = ibuf.shape[2]
    ibuf[slot, :, pl.ds(past, LANES)] = inew_ref[...]
    kbuf[slot, :, pl.ds(past, LANES)] = kvnew_ref[...]
    qpos = past + jnp.minimum(lax.broadcasted_iota(jnp.int32, (rows, 1), 0), t_dec - 1)

    ik = ibuf[slot]
    ik_hi = ik.astype(BF16)
    ik_lo = (ik - ik_hi.astype(F32)).astype(BF16)
    k3 = jnp.concatenate([ik_hi, ik_lo, ik_hi], axis=0)
    iq = iq_ref[...]
    w = small_ref[...]
    sc = jnp.zeros((rows, n_keys), F32)
    for h in range(IDX_HEADS):
        d = jnp.dot(_split3_cols(iq[:, h * IDX_DIM:(h + 1) * IDX_DIM]), k3, preferred_element_type=F32)
        sc = sc + jnp.maximum(d, 0.0) * w[:, h:h + 1]
    sc = sc * ((IDX_DIM ** -0.5) * (IDX_HEADS ** -0.5))
    kpos = lax.broadcasted_iota(jnp.int32, (1, n_keys), 1)
    real_row = lax.broadcasted_iota(jnp.int32, (rows, 1), 0) < t_dec
    ok = jnp.logical_and(kpos <= qpos, jnp.logical_or(real_row, kpos == 0))
    mn = jnp.min(jnp.where(ok, sc, POS_INF), axis=1, keepdims=True)
    mx = jnp.max(jnp.where(ok, sc, NEG), axis=1, keepdims=True)
    n_valid = jnp.where(real_row, qpos + 1, 1).astype(F32)
    cap = _topk_cap_rows(jnp.where(ok, sc, -POS_INF), mn, mx, n_valid, topk)

    dq = dq_ref[...]
    for g in range(DSA_KV_HEADS):
        q_rows = _stack_heads(dq, g, hg, scale).astype(BF16)
        o = _attend_full(q_rows, kbuf[slot, g * HEAD_DIM:(g + 1) * HEAD_DIM, :],
                         kbuf[slot, (DSA_KV_HEADS + g) * HEAD_DIM:(DSA_KV_HEADS + g + 1) * HEAD_DIM, :], cap, hg)
        for j in range(hg):
            h = g * hg + j
            o_ref[:, h * HEAD_DIM:(h + 1) * HEAD_DIM] = o[j * rows:(j + 1) * rows]


def _pad_rows(a2d, bd, t_dec):
    return jnp.pad(a2d.reshape(bd, t_dec, -1), ((0, 0), (0, DEC_ROWS - t_dec), (0, 0)))


def _new_cols(a_t, bd, t_dec):
    f = a_t.shape[0]
    return jnp.pad(a_t.reshape(f, bd, t_dec).transpose(1, 0, 2), ((0, 0), (0, 0), (0, LANES - t_dec)))


def _feature_major_pages(pool):
    n_pool, page = pool.shape[:2]
    nd = pool.ndim
    return pool.transpose((0,) + tuple(range(2, nd)) + (1,)).reshape(n_pool, -1, page)


def _decode_nsa(fs, cache_nsa, win_state, page_table, cw, past, t_dec):
    bd, n_pages = page_table.shape
    pool = _feature_major_pages(cache_nsa)
    page = pool.shape[2]
    f = pool.shape[1]
    n_keys = past + LANES
    wst = _feature_major_pages(win_state)
    kvw = NSA_KV_HEADS * HEAD_DIM
    per_b = lambda *shape: pl.BlockSpec((None,) + shape, lambda b, pt: (b,) + (0,) * len(shape))
    const = lambda a: pl.BlockSpec(a.shape, lambda b, pt: (0,) * a.ndim, pipeline_mode=pl.Buffered(1))
    grid_spec = pltpu.PrefetchScalarGridSpec(
        num_scalar_prefetch=1,
        grid=(bd,),
        in_specs=[per_b(DEC_ROWS, W_NQ), per_b(DEC_ROWS, W_NQ), per_b(DEC_ROWS, LANES),
                  pl.BlockSpec(memory_space=pl.ANY), per_b(f, LANES),
                  per_b(2 * kvw, wst.shape[2]), per_b(2 * kvw, LANES)] + [const(a) for a in cw],
        out_specs=per_b(DEC_ROWS, W_NQ),
        scratch_shapes=[pltpu.VMEM((2, f, n_keys), F32), pltpu.SemaphoreType.DMA((2,)),
                        pltpu.VMEM((past, kvw), F32), pltpu.VMEM((past, kvw), F32),
                        pltpu.VMEM((past // GROUP_TOKENS + 8, NSA_KV_HEADS * CMP_HID), F32)])
    out = pl.pallas_call(
        functools.partial(_decode_nsa_kernel, past=past, t_dec=t_dec, n_pages=n_pages, page=page),
        grid_spec=grid_spec,
        out_shape=jax.ShapeDtypeStruct((bd, DEC_ROWS, W_NQ), F32),
        compiler_params=pltpu.CompilerParams(dimension_semantics=('arbitrary',), vmem_limit_bytes=VMEM_LIMIT),
        name='decode_nsa',
    )(page_table.reshape(-1), _pad_rows(fs['nsa_q'], bd, t_dec), _pad_rows(fs['nsa_q_rot'], bd, t_dec),
      _pad_rows(fs['small'], bd, t_dec), pool, _new_cols(fs['nsa_t'], bd, t_dec), wst,
      _new_cols(fs['win_t'], bd, t_dec), *cw)
    return out[:, :t_dec].reshape(bd * t_dec, W_NQ)


def _decode_dsa(fs, cache_dsa, cache_idx, page_table, past, t_dec):
    bd, n_pages = page_table.shape
    kpool = _feature_major_pages(cache_dsa)
    ipool = _feature_major_pages(cache_idx)
    page = kpool.shape[2]
    n_keys = past + LANES
    topk = min(DSA_TOPK_MAX, (past + t_dec) // 4)
    per_b = lambda *shape: pl.BlockSpec((None,) + shape, lambda b, pt: (b,) + (0,) * len(shape))
    grid_spec = pltpu.PrefetchScalarGridSpec(
        num_scalar_prefetch=1,
        grid=(bd,),
        in_specs=[per_b(DEC_ROWS, W_IQ), per_b(DEC_ROWS, LANES), per_b(DEC_ROWS, W_DQ),
                  pl.BlockSpec(memory_space=pl.ANY), pl.BlockSpec(memory_space=pl.ANY),
                  per_b(IDX_DIM, LANES), per_b(W_DKV, LANES)],
        out_specs=per_b(DEC_ROWS, W_DQ),
        scratch_shapes=[pltpu.VMEM((2, IDX_DIM, n_keys), F32), pltpu.VMEM((2, W_DKV, n_keys), F32),
                        pltpu.SemaphoreType.DMA((2,)), pltpu.SemaphoreType.DMA((2,))])
    out = pl.pallas_call(
        functools.partial(_decode_dsa_kernel, past=past, t_dec=t_dec, n_pages=n_pages, page=page, topk=topk),
        grid_spec=grid_spec,
        out_shape=jax.ShapeDtypeStruct((bd, DEC_ROWS, W_DQ), F32),
        compiler_params=pltpu.CompilerParams(dimension_semantics=('arbitrary',), vmem_limit_bytes=VMEM_LIMIT),
        name='decode_dsa',
    )(page_table.reshape(-1), _pad_rows(fs['idx_q'], bd, t_dec), _pad_rows(fs['small'], bd, t_dec),
      _pad_rows(fs['dsa_q'], bd, t_dec), ipool, kpool, _new_cols(fs['idx_t'], bd, t_dec),
      _new_cols(fs['dsa_t'], bd, t_dec))
    return out[:, :t_dec].reshape(bd * t_dec, W_DQ)


def _rows_from_t(a_t, bx, t, lead):
    return a_t.reshape(lead + (bx, t)).transpose((len(lead), len(lead) + 1) + tuple(range(len(lead))))


def _rows_from_bt(a_bt, lead):
    bx, _, t = a_bt.shape
    n = len(lead)
    return a_bt.reshape((bx,) + lead + (t,)).transpose((0, n + 1) + tuple(range(1, n + 1)))


def kernel(x_prompt, x_sample, cache_nsa_kv, cache_dsa_kv, cache_dsa_idx_k, state_nsa_win_kv, page_table,
           c_prompt, c_sample, norm_g, w_ada, b_ada, w_in, cmp_pe, cmp_w1, cmp_w2,
           w_up_nsa, w_up_dsa, w_out, final_g):
    depth = norm_g.shape[0]
    assert depth == 1
    B, T, D = x_prompt.shape
    Bd, Td, _ = x_sample.shape
    l = 0
    past = page_table.shape[1] * cache_nsa_kv.shape[2]

    c_all = jnp.concatenate([c_prompt, c_sample], axis=0)
    pad = (-c_all.shape[0]) % 8
    c_all = jnp.pad(c_all, ((0, pad), (0, 0)))
    mod = _ada_mod(c_all, w_ada[l], b_ada[l])
    shift, scale, gate = mod[:, 0:D], mod[:, D:2 * D], mod[:, 2 * D:3 * D]

    weights = _prep_inproj_weights(w_in[l], D)

    tm = PROJ_ROWS
    fp = _in_projection_prompt(x_prompt.reshape(B * T, D), norm_g[l], scale[:B], shift[:B], T,
                               _prep_inproj_weights_prompt(w_in[l], D), tm)
    nsa_rows = _rows_from_bt(fp['nsa_t'], (4, NSA_KV_HEADS, HEAD_DIM))
    wb_p = min(WINDOW, T)
    new_win_prompt = _rows_from_bt(fp['win_t'][:, :, T - wb_p:], (2, NSA_KV_HEADS, HEAD_DIM))
    dsa_rows = _rows_from_bt(fp['dsa_t'], (2, DSA_KV_HEADS, HEAD_DIM))
    idx_rows = _rows_from_bt(fp['idx_t'], (IDX_DIM,))
    cw = _prep_compress_weights(cmp_pe[l], cmp_w1[l], cmp_w2[l])
    ck, cv = _compress_prompt(fp['cmp_k'], fp['cmp_v'], cw, B, T)
    kc = min(KEY_CHUNK, T)
    o_n = _nsa_prompt(fp, ck, cv, B, T, Q_TILE, kc, min(WINDOW + Q_TILE, T))
    o_d = _dsa_prompt(fp, B, T, Q_TILE, kc)
    y_prompt = _mixer_out(x_prompt.reshape(B * T, D), o_n, o_d, fp, gate[:B], T, final_g,
                          w_up_nsa[l], w_up_dsa[l], w_out[l], tm).reshape(B, T, D)

    Rs = Bd * Td
    tms = min(PROJ_ROWS, Rs)
    pos_s = past + jnp.arange(Td, dtype=jnp.int32)
    pos_rows = jnp.tile(pos_s, tms // Td)
    rep = lambda a: jnp.repeat(a[B:B + Bd], Td, axis=0)
    fs = _in_projection(x_sample.reshape(Rs, D), norm_g[l], rep(scale), rep(shift), 1, pos_rows, weights, tms, 1)
    for name in ('nsa_t', 'win_t', 'dsa_t', 'idx_t'):
        fs[name] = fs[name][0]
    s_nsa_rows = _rows_from_t(fs['nsa_t'], Bd, Td, (4, NSA_KV_HEADS, HEAD_DIM))
    s_win_rows = _rows_from_t(fs['win_t'], Bd, Td, (2, NSA_KV_HEADS, HEAD_DIM))
    s_dsa_rows = _rows_from_t(fs['dsa_t'], Bd, Td, (2, DSA_KV_HEADS, HEAD_DIM))
    s_idx_rows = _rows_from_t(fs['idx_t'], Bd, Td, (IDX_DIM,))

    win_state = state_nsa_win_kv[l]
    so_n = _decode_nsa(fs, cache_nsa_kv[l], win_state, page_table, cw, past, Td)
    so_d = _decode_dsa(fs, cache_dsa_kv[l], cache_dsa_idx_k[l], page_table, past, Td)
    y_sample = _mixer_out(x_sample.reshape(Rs, D), so_n, so_d, fs,
                          rep(gate), 1, final_g, w_up_nsa[l], w_up_dsa[l], w_out[l], tms).reshape(Bd, Td, D)
    new_win_sample = jnp.concatenate([win_state, s_win_rows], axis=1)[:, Td:]

    return (y_prompt, y_sample, nsa_rows[None], dsa_rows[None], idx_rows[None], new_win_prompt[None],
            s_nsa_rows[None], s_dsa_rows[None], s_idx_rows[None], new_win_sample[None])
```
